```python
import jax, jax.numpy as jnp
from jax import lax
import numpy as np

D_MODEL = 1024
BATCH = 16
SEQ = 2048
DEPTH = 2

GRID_W = 64
CTX_LEN = 256
N_MIXERS = 2
N_HEADS = 16
HEAD_DIM = D_MODEL // N_HEADS
WIN_ROWS = 8
WIN_COLS = 16
CONV_WIDTH = 3
MLP_MULT = 4
D_FF = MLP_MULT * D_MODEL
N_CONV_LAYERS = (DEPTH + 1) // 2
N_ATTN_LAYERS = DEPTH // 2
RMS_EPS = 1e-6
NEG_INF = -1e30
ATTN_SCALE = HEAD_DIM ** -0.5

kernel_name = "hybrid_conv_natten_dit_block"


def rmsnorm(x, g):
    xf = x.astype(jnp.float32)
    inv = lax.rsqrt(jnp.mean(xf * xf, axis=-1, keepdims=True) + RMS_EPS)
    return (xf * inv).astype(x.dtype) * g


def ada_terms(cond, w, b):
    m = jax.nn.silu(cond) @ w + b
    return jnp.split(m[..., None, :], 6, axis=-1)


def modulate(h, shift, scale):
    return h * (1.0 + scale) + shift


def squared_relu_mlp(h, w1, w2):
    return jnp.square(jax.nn.relu(h @ w1)) @ w2


def depthwise_conv_centred(u, w):
    pad = CONV_WIDTH // 2
    L = u.shape[1]
    up = jnp.pad(u, ((0, 0), (pad, pad), (0, 0)))
    out = up[:, 0:L] * w[0]
    for j in range(1, CONV_WIDTH):
        out = out + up[:, j:j + L] * w[j]
    return out


def short_conv_mixer(h, w_in, w_conv, w_out):
    b_gate, c_gate, v = jnp.split(h @ w_in, 3, axis=-1)
    return (b_gate * depthwise_conv_centred(c_gate * v, w_conv)) @ w_out


def split_heads(t):
    B, L, _ = t.shape
    return t.reshape(B, L, N_HEADS, HEAD_DIM).transpose(0, 2, 1, 3)


def merge_heads(t):
    B, H, L, dh = t.shape
    return t.transpose(0, 2, 1, 3).reshape(B, L, H * dh)


def dense_ctx_attention(qc, kc, vc):
    s = jnp.einsum('bhqd,bhkd->bhqk', qc, kc).astype(jnp.float32) * ATTN_SCALE
    p = jax.nn.softmax(s, axis=-1).astype(vc.dtype)
    return jnp.einsum('bhqk,bhkd->bhqd', p, vc)


def neighbourhood_latent_attention(q, k, v, kc, vc, rpb):
    B, H, S, dh = q.shape
    rows = S // GRID_W
    kh = min(WIN_ROWS, rows)
    q = q.reshape(B, H, rows, GRID_W, dh) * ATTN_SCALE
    k = k.reshape(B, H, rows, GRID_W, dh)
    v = v.reshape(B, H, rows, GRID_W, dh)

    cols = jnp.arange(GRID_W)
    col_start = jnp.clip(cols - WIN_COLS // 2, 0, GRID_W - WIN_COLS)
    col_mask = (cols[None, :] >= col_start[:, None]) & (cols[None, :] < col_start[:, None] + WIN_COLS)
    dc = jnp.clip(cols[None, :] - cols[:, None] + (WIN_COLS - 1), 0, 2 * WIN_COLS - 2)

    def one_row(r):
        rs = jnp.clip(r - kh // 2, 0, rows - kh)
        q_r = lax.dynamic_index_in_dim(q, r, axis=2, keepdims=False)
        k_rb = lax.dynamic_slice_in_dim(k, rs, kh, axis=2)
        v_rb = lax.dynamic_slice_in_dim(v, rs, kh, axis=2)
        dr = rs + jnp.arange(kh) - r + (WIN_ROWS - 1)
        bias = rpb[:, dr[:, None, None], dc[None, :, :]].transpose(0, 2, 1, 3)
        s_loc = jnp.einsum('bhqd,bhrkd->bhqrk', q_r, k_rb).astype(jnp.float32) + bias.astype(jnp.float32)
        s_loc = jnp.where(col_mask[None, None, :, None, :], s_loc, NEG_INF)
        s_ctx = jnp.einsum('bhqd,bhkd->bhqk', q_r, kc).astype(jnp.float32)
        s_all = jnp.concatenate([s_loc.reshape(B, H, GRID_W, kh * GRID_W), s_ctx], axis=-1)
        p = jax.nn.softmax(s_all, axis=-1).astype(v.dtype)
        p_loc = p[..., :kh * GRID_W].reshape(B, H, GRID_W, kh, GRID_W)
        p_ctx = p[..., kh * GRID_W:]
        return (jnp.einsum('bhqrk,bhrkd->bhqd', p_loc, v_rb)
                + jnp.einsum('bhqk,bhkd->bhqd', p_ctx, vc))

    out = lax.map(one_row, jnp.arange(rows))
    return out.transpose(1, 0, 3, 2, 4).reshape(B, S, H * dh)


def neighbourhood_attention_mixer(h, hc, w_qkv, rpb, w_out, with_ctx_queries):
    D = h.shape[-1]
    q, k, v = [split_heads(t) for t in jnp.split(h @ w_qkv, 3, axis=-1)]
    if with_ctx_queries:
        qc, kc, vc = [split_heads(t) for t in jnp.split(hc @ w_qkv, 3, axis=-1)]
    else:
        kc, vc = [split_heads(t) for t in jnp.split(hc @ w_qkv[:, D:], 2, axis=-1)]
    y = neighbourhood_latent_attention(q, k, v, kc, vc, rpb) @ w_out
    yc = merge_heads(dense_ctx_attention(qc, kc, vc)) @ w_out if with_ctx_queries else None
    return y, yc


def setup_inputs(seed: int = 0) -> dict:
    key = jax.random.key(seed)
    ks = jax.random.split(key, 17)
    D = D_MODEL

    def nrm(k, shape, s):
        return jax.random.normal(k, shape, jnp.float32) * s

    return {
        "x": nrm(ks[0], (BATCH, SEQ, D), 1.0),
        "c": nrm(ks[1], (BATCH, D), 1.0),
        "ctx": nrm(ks[2], (BATCH, CTX_LEN, D), 1.0),
        "c_ctx": nrm(ks[3], (D,), 1.0),
        "norm1_g": 1.0 + nrm(ks[4], (DEPTH, D), 0.02),
        "norm2_g": 1.0 + nrm(ks[5], (DEPTH, D), 0.02),
        "ada_w": nrm(ks[6], (DEPTH, D, 6 * D), D ** -0.5),
        "ada_b": nrm(ks[7], (DEPTH, 6 * D), 0.01),
        "conv_in_w": nrm(ks[8], (N_CONV_LAYERS, D, 3 * D), D ** -0.5),
        "conv_w": nrm(ks[9], (N_CONV_LAYERS, CONV_WIDTH, D), CONV_WIDTH ** -0.5),
        "conv_out_w": nrm(ks[10], (N_CONV_LAYERS, D, D), D ** -0.5),
        "attn_qkv_w": nrm(ks[11], (N_ATTN_LAYERS, D, 3 * D), D ** -0.5),
        "attn_rpb": nrm(ks[12], (N_ATTN_LAYERS, N_HEADS, 2 * WIN_ROWS - 1, 2 * WIN_COLS - 1), 0.1),
        "attn_out_w": nrm(ks[13], (N_ATTN_LAYERS, D, D), D ** -0.5),
        "mlp_w1": nrm(ks[14], (DEPTH, D, D_FF), D ** -0.5),
        "mlp_w2": nrm(ks[15], (DEPTH, D_FF, D), D_FF ** -0.5),
        "final_g": 1.0 + nrm(ks[16], (D,), 0.02),
    }


def reference(x, c, ctx, c_ctx, norm1_g, norm2_g, ada_w, ada_b, conv_in_w, conv_w, conv_out_w,
              attn_qkv_w, attn_rpb, attn_out_w, mlp_w1, mlp_w2, final_g):
    for i in range(DEPTH):
        last = i == DEPTH - 1
        j = i // N_MIXERS
        sh1, sc1, gt1, sh2, sc2, gt2 = ada_terms(c, ada_w[i], ada_b[i])
        csh1, csc1, cgt1, csh2, csc2, cgt2 = ada_terms(c_ctx, ada_w[i], ada_b[i])
        h = modulate(rmsnorm(x, norm1_g[i]), sh1, sc1)
        if i % N_MIXERS == 0:
            y = short_conv_mixer(h, conv_in_w[j], conv_w[j], conv_out_w[j])
            if not last:
                hc = modulate(rmsnorm(ctx, norm1_g[i]), csh1, csc1)
                yc = short_conv_mixer(hc, conv_in_w[j], conv_w[j], conv_out_w[j])
        else:
            hc = modulate(rmsnorm(ctx, norm1_g[i]), csh1, csc1)
            y, yc = neighbourhood_attention_mixer(h, hc, attn_qkv_w[j], attn_rpb[j], attn_out_w[j],
                                                  with_ctx_queries=not last)
        x = x + gt1 * y
        x = x + gt2 * squared_relu_mlp(modulate(rmsnorm(x, norm2_g[i]), sh2, sc2), mlp_w1[i], mlp_w2[i])
        if not last:
            ctx = ctx + cgt1 * yc
            ctx = ctx + cgt2 * squared_relu_mlp(modulate(rmsnorm(ctx, norm2_g[i]), csh2, csc2),
                                                mlp_w1[i], mlp_w2[i])
    return rmsnorm(x, final_g)
```

```python
import functools

import numpy as np
import jax
import jax.numpy as jnp
from jax import lax
from jax.experimental import pallas as pl
from jax.experimental.pallas import tpu as pltpu

GRID_W = 64
N_HEADS = 16
WIN_ROWS = 8
WIN_COLS = 16
RMS_EPS = 1e-6
NEG_INF = -1e30

HALO = 8
HEADS_PER_STEP = 2
Q_ROWS = 4
K_ROWS = 12
VMEM_LIMIT = 56 * 1024 * 1024

BF16 = jnp.bfloat16
F32 = jnp.float32


def _norm_mod(v, g, shift, scale):
    inv = lax.rsqrt(jnp.mean(v * v, axis=-1, keepdims=True) + RMS_EPS)
    return ((v * inv) * g) * (1.0 + scale) + shift


def _ada_kernel(cond_ref, w_ref, b_ref, o_ref):
    cnd = cond_ref[...]
    s = (cnd / (1.0 + jnp.exp(-cnd))).astype(BF16)
    o_ref[...] = jnp.dot(s, w_ref[...].astype(BF16), preferred_element_type=F32) + b_ref[...]


def _ada_call(cond, ada_w, ada_b):
    depth, d, n = ada_w.shape
    rows = cond.shape[0]
    bn = d
    return pl.pallas_call(
        _ada_kernel,
        grid=(depth, n // bn),
        in_specs=[
            pl.BlockSpec((rows, d), lambda i, j: (0, 0)),
            pl.BlockSpec((None, d, bn), lambda i, j: (i, 0, j)),
            pl.BlockSpec((None, 1, bn), lambda i, j: (i, 0, j)),
        ],
        out_specs=pl.BlockSpec((None, rows, bn), lambda i, j: (i, 0, j)),
        out_shape=jax.ShapeDtypeStruct((depth, rows, n), F32),
        compiler_params=pltpu.CompilerParams(
            dimension_semantics=("arbitrary", "arbitrary"), vmem_limit_bytes=VMEM_LIMIT),
    )(cond, ada_w, ada_b.reshape(depth, 1, n))


def _mod_spec(mod):
    if mod.shape[0] == 1:
        return pl.BlockSpec((None, 6, mod.shape[2]), lambda b, t: (0, 0, 0))
    return pl.BlockSpec((None, 6, mod.shape[2]), lambda b, t: (b, 0, 0))


def _const_spec(shape):
    nd = len(shape)
    return pl.BlockSpec(shape, lambda b, t: (0,) * nd, pipeline_mode=pl.Buffered(1))


def _conv_pre_kernel(x_ref, xp_ref, xn_ref, mod_ref, g_ref, w_ref, cw_ref, o_ref, *, tm, nt, d):
    t = pl.program_id(1)
    shift = mod_ref[0:1, :]
    scale = mod_ref[1:2, :]
    xe = jnp.concatenate([xp_ref[...], x_ref[...], xn_ref[...]], axis=0)
    he = _norm_mod(xe, g_ref[...], shift, scale).astype(BF16)
    cv = jnp.dot(he, w_ref[:, d:], preferred_element_type=F32)
    u = cv[:, :d] * cv[:, d:]
    row = lax.broadcasted_iota(jnp.int32, (tm + 2 * HALO, 1), 0)
    inside = ((row >= HALO) | (t > 0)) & ((row < tm + HALO) | (t < nt - 1))
    u = jnp.where(inside, u, 0.0)
    n_ext = tm + 2 * HALO
    u_prev = pltpu.roll(u, 1, axis=0)[HALO:HALO + tm]
    u_next = pltpu.roll(u, n_ext - 1, axis=0)[HALO:HALO + tm]
    conv = u_prev * cw_ref[0:1, :] + u[HALO:HALO + tm] * cw_ref[1:2, :] + u_next * cw_ref[2:3, :]
    bg = jnp.dot(he[HALO:HALO + tm], w_ref[:, :d], preferred_element_type=F32)
    o_ref[...] = (bg * conv).astype(BF16)


def _conv_pre_call(x, mod, g, w_in, conv_w, tm):
    b, l, d = x.shape
    nt = l // tm
    hb = tm // HALO
    last_hb = l // HALO - 1
    kern = functools.partial(_conv_pre_kernel, tm=tm, nt=nt, d=d)
    return pl.pallas_call(
        kern,
        grid=(b, nt),
        in_specs=[
            pl.BlockSpec((None, tm, d), lambda i, t: (i, t, 0)),
            pl.BlockSpec((None, HALO, d), lambda i, t: (i, jnp.maximum(t * hb - 1, 0), 0)),
            pl.BlockSpec((None, HALO, d), lambda i, t: (i, jnp.minimum((t + 1) * hb, last_hb), 0)),
            _mod_spec(mod),
            _const_spec((1, d)),
            _const_spec((d, 3 * d)),
            _const_spec((3, d)),
        ],
        out_specs=pl.BlockSpec((None, tm, d), lambda i, t: (i, t, 0)),
        out_shape=jax.ShapeDtypeStruct((b, l, d), BF16),
        compiler_params=pltpu.CompilerParams(
            dimension_semantics=("arbitrary", "arbitrary"), vmem_limit_bytes=VMEM_LIMIT),
    )(x, x, x, mod, g, w_in, conv_w)


def _proj_pre_kernel(x_ref, mod_ref, g_ref, w_ref, o_ref, *, n_scaled, q_scale):
    h = _norm_mod(x_ref[...], g_ref[...], mod_ref[0:1, :], mod_ref[1:2, :]).astype(BF16)
    if n_scaled:
        o_ref[:, :n_scaled] = (
            jnp.dot(h, w_ref[:, :n_scaled], preferred_element_type=F32) * q_scale).astype(BF16)
    o_ref[:, n_scaled:] = jnp.dot(h, w_ref[:, n_scaled:], preferred_element_type=F32).astype(BF16)


def _proj_pre_call(x, mod, g, w, tm, n_scaled, q_scale):
    b, l, d = x.shape
    n = w.shape[1]
    kern = functools.partial(_proj_pre_kernel, n_scaled=n_scaled, q_scale=q_scale)
    return pl.pallas_call(
        kern,
        grid=(b, l // tm),
        in_specs=[
            pl.BlockSpec((None, tm, d), lambda i, t: (i, t, 0)),
            _mod_spec(mod),
            _const_spec((1, d)),
            _const_spec((d, n)),
        ],
        out_specs=pl.BlockSpec((None, tm, n), lambda i, t: (i, t, 0)),
        out_shape=jax.ShapeDtypeStruct((b, l, n), BF16),
        compiler_params=pltpu.CompilerParams(
            dimension_semantics=("arbitrary", "arbitrary"), vmem_limit_bytes=VMEM_LIMIT),
    )(x, mod, g, w)


def _post_kernel(x_ref, y_ref, mod_ref, g_ref, wo_ref, w1_ref, w2_ref, *rest, ff_chunk, final):
    if final:
        fg_ref, o_ref = rest
    else:
        (o_ref,) = rest
    gate1 = mod_ref[2:3, :]
    shift2 = mod_ref[3:4, :]
    scale2 = mod_ref[4:5, :]
    gate2 = mod_ref[5:6, :]
    x1 = x_ref[...] + gate1 * jnp.dot(y_ref[...], wo_ref[...], preferred_element_type=F32)
    h = _norm_mod(x1, g_ref[...], shift2, scale2).astype(BF16)
    ff = w1_ref.shape[1]
    acc = None
    for f in range(0, ff, ff_chunk):
        a = jnp.dot(h, w1_ref[:, f:f + ff_chunk], preferred_element_type=F32)
        a = jnp.maximum(a, 0.0)
        a = (a * a).astype(BF16)
        part = jnp.dot(a, w2_ref[f:f + ff_chunk, :], preferred_element_type=F32)
        acc = part if acc is None else acc + part
    x2 = x1 + gate2 * acc
    if final:
        inv = lax.rsqrt(jnp.mean(x2 * x2, axis=-1, keepdims=True) + RMS_EPS)
        x2 = (x2 * inv) * fg_ref[...]
    o_ref[...] = x2


def _post_call(x, y, mod, g, w_out, w1, w2, final_g, tm, ff_chunk=1024):
    b, l, d = x.shape
    ff = w1.shape[1]
    final = final_g is not None
    kern = functools.partial(_post_kernel, ff_chunk=ff_chunk, final=final)
    in_specs = [
        pl.BlockSpec((None, tm, d), lambda i, t: (i, t, 0)),
        pl.BlockSpec((None, tm, d), lambda i, t: (i, t, 0)),
        _mod_spec(mod),
        _const_spec((1, d)),
        _const_spec((d, d)),
        _const_spec((d, ff)),
        _const_spec((ff, d)),
    ]
    args = [x, y, mod, g, w_out, w1, w2]
    if final:
        in_specs.append(_const_spec((1, d)))
        args.append(final_g)
    return pl.pallas_call(
        kern,
        grid=(b, l // tm),
        in_specs=in_specs,
        out_specs=pl.BlockSpec((None, tm, d), lambda i, t: (i, t, 0)),
        out_shape=jax.ShapeDtypeStruct((b, l, d), F32),
        compiler_params=pltpu.CompilerParams(
            dimension_semantics=("arbitrary", "arbitrary"), vmem_limit_bytes=VMEM_LIMIT),
    )(*args)


def _group_window(g, n_groups):
    return min(max(g - 1, 0), n_groups - K_ROWS // Q_ROWS)


def _bias_table(rpb, rows):
    n_groups = rows // Q_ROWS
    tabs = []
    for g in (0, 1, n_groups - 1):
        ws = _group_window(g, n_groups)
        qr = np.arange(Q_ROWS)[:, None, None, None] + Q_ROWS * g
        qc = np.arange(GRID_W)[None, :, None, None]
        kr = np.arange(K_ROWS)[None, None, :, None] + Q_ROWS * ws
        kc = np.arange(GRID_W)[None, None, None, :]
        rs = np.clip(qr - WIN_ROWS // 2, 0, rows - WIN_ROWS)
        cs = np.clip(qc - WIN_COLS // 2, 0, GRID_W - WIN_COLS)
        ok = (kr >= rs) & (kr < rs + WIN_ROWS) & (kc >= cs) & (kc < cs + WIN_COLS)
        dr = np.clip(kr - qr + (WIN_ROWS - 1), 0, 2 * WIN_ROWS - 2)
        dc = np.clip(kc - qc + (WIN_COLS - 1), 0, 2 * WIN_COLS - 2)
        shape = (Q_ROWS, GRID_W, K_ROWS, GRID_W)
        ok, dr, dc = (np.broadcast_to(a, shape).reshape(Q_ROWS * GRID_W, K_ROWS * GRID_W)
                      for a in (ok, dr, dc))
        vals = rpb[:, dr, dc].astype(F32)
        tabs.append(jnp.where(ok[None], vals, NEG_INF))
    return jnp.stack(tabs, axis=0)


def _attn_kernel(q_ref, k_ref, v_ref, kc_ref, vc_ref, bias_ref, o_ref, *, n_groups, head_dim):
    tq = Q_ROWS * GRID_W
    tk = K_ROWS * GRID_W
    lane = lax.broadcasted_iota(jnp.int32, (1, HEADS_PER_STEP * head_dim), 1)
    nt = (((1,), (1,)), ((), ()))
    kc = kc_ref[...]
    vc = vc_ref[...]
    for g in range(n_groups):
        ws = _group_window(g, n_groups)
        cls = 0 if g == 0 else (2 if g == n_groups - 1 else 1)
        q = q_ref[g * tq:(g + 1) * tq, :]
        kw = k_ref[ws * tq:ws * tq + tk, :]
        vw = v_ref[ws * tq:ws * tq + tk, :]
        out = None
        for hh in range(HEADS_PER_STEP):
            in_head = (lane >= hh * head_dim) & (lane < (hh + 1) * head_dim)
            qh = jnp.where(in_head, q, jnp.zeros_like(q))
            s_loc = lax.dot_general(qh, kw, nt, preferred_element_type=F32) + bias_ref[cls, hh]
            s_ctx = lax.dot_general(qh, kc, nt, preferred_element_type=F32)
            m = jnp.maximum(jnp.max(s_loc, axis=-1, keepdims=True),
                            jnp.max(s_ctx, axis=-1, keepdims=True))
            p_loc = jnp.exp(s_loc - m)
            p_ctx = jnp.exp(s_ctx - m)
            denom = jnp.sum(p_loc, axis=-1, keepdims=True) + jnp.sum(p_ctx, axis=-1, keepdims=True)
            vh = jnp.where(in_head, vw, jnp.zeros_like(vw))
            vch = jnp.where(in_head, vc, jnp.zeros_like(vc))
            o = (jnp.dot(p_loc.astype(BF16), vh, preferred_element_type=F32)
                 + jnp.dot(p_ctx.astype(BF16), vch, preferred_element_type=F32)) / denom
            out = o if out is None else out + o
        o_ref[g * tq:(g + 1) * tq, :] = out.astype(BF16)


def _attn_call(qkv, ckv, bias):
    b, s, d3 = qkv.shape
    d = d3 // 3
    ctx_len = ckv.shape[1]
    head_dim = d // N_HEADS
    w = HEADS_PER_STEP * head_dim
    n_hp = d // w
    rows = s // GRID_W
    n_groups = rows // Q_ROWS
    tq = Q_ROWS * GRID_W
    tk = K_ROWS * GRID_W
    kern = functools.partial(_attn_kernel, n_groups=n_groups, head_dim=head_dim)
    return pl.pallas_call(
        kern,
        grid=(n_hp, b),
        in_specs=[
            pl.BlockSpec((None, s, w), lambda h, i: (i, 0, h)),
            pl.BlockSpec((None, s, w), lambda h, i: (i, 0, n_hp + h)),
            pl.BlockSpec((None, s, w), lambda h, i: (i, 0, 2 * n_hp + h)),
            pl.BlockSpec((None, ctx_len, w), lambda h, i: (i, 0, h)),
            pl.BlockSpec((None, ctx_len, w), lambda h, i: (i, 0, n_hp + h)),
            pl.BlockSpec((3, HEADS_PER_STEP, tq, tk), lambda h, i: (0, h, 0, 0)),
        ],
        out_specs=pl.BlockSpec((None, s, w), lambda h, i: (i, 0, h)),
        out_shape=jax.ShapeDtypeStruct((b, s, d), BF16),
        compiler_params=pltpu.CompilerParams(
            dimension_semantics=("arbitrary", "arbitrary"), vmem_limit_bytes=VMEM_LIMIT),
    )(qkv, qkv, qkv, ckv, ckv, bias)


def kernel(x, c, ctx, c_ctx, norm1_g, norm2_g, ada_w, ada_b, conv_in_w, conv_w, conv_out_w,
           attn_qkv_w, attn_rpb, attn_out_w, mlp_w1, mlp_w2, final_g):
    b, s, d = x.shape
    ctx_len = ctx.shape[1]
    depth = ada_w.shape[0]
    assert depth == 2 and s % (GRID_W * Q_ROWS) == 0 and d % (N_HEADS * HEADS_PER_STEP) == 0
    head_dim = d // N_HEADS
    tm = 512
    tmc = ctx_len

    pad = (-(b + 1)) % 8
    cond = jnp.concatenate([c, c_ctx[None, :], jnp.zeros((pad, d), F32)], axis=0)
    mods = _ada_call(cond, ada_w, ada_b)
    mod_lat = mods[:, :b].reshape(depth, b, 6, d)
    mod_ctx = mods[:, b:b + 1].reshape(depth, 1, 6, d)

    n1 = norm1_g.reshape(depth, 1, d)
    n2 = norm2_g.reshape(depth, 1, d)
    w1 = mlp_w1.astype(BF16)
    w2 = mlp_w2.astype(BF16)

    w_in = conv_in_w[0].astype(BF16)
    w_co = conv_out_w[0].astype(BF16)
    y = _conv_pre_call(x, mod_lat[0], n1[0], w_in, conv_w[0], tm)
    yc = _conv_pre_call(ctx, mod_ctx[0], n1[0], w_in, conv_w[0], tmc)
    x = _post_call(x, y, mod_lat[0], n2[0], w_co, w1[0], w2[0], None, tm)
    ctx = _post_call(ctx, yc, mod_ctx[0], n2[0], w_co, w1[0], w2[0], None, tmc)

    w_qkv = attn_qkv_w[0].astype(BF16)
    w_ao = attn_out_w[0].astype(BF16)
    qkv = _proj_pre_call(x, mod_lat[1], n1[1], w_qkv, tm, d, head_dim ** -0.5)
    ckv = _proj_pre_call(ctx, mod_ctx[1], n1[1], w_qkv[:, d:], tmc, 0, 1.0)
    bias = _bias_table(attn_rpb[0], s // GRID_W)
    o = _attn_call(qkv, ckv, bias)
    return _post_call(x, o, mod_lat[1], n2[1], w_ao, w1[1], w2[1], final_g.reshape(1, d), tm)
```

```python
import functools

import jax
import jax.numpy as jnp
from jax import lax
from jax.experimental import pallas as pl
from jax.experimental.pallas import tpu as pltpu

GRID_W = 64
N_HEADS = 16
WIN_ROWS = 8
WIN_COLS = 16
RMS_EPS = 1e-6
NEG_INF = -1e30

HALO = 8
HEADS_PER_STEP = 2
Q_ROWS = 4
K_ROWS = 12
VMEM_LIMIT = 56 * 1024 * 1024

BF16 = jnp.bfloat16
F32 = jnp.float32


def _norm_mod(v, g, shift, scale):
    inv = lax.rsqrt(jnp.mean(v * v, axis=-1, keepdims=True) + RMS_EPS)
    return ((v * inv) * g) * (1.0 + scale) + shift


def _ada_kernel(cond_ref, w_ref, b_ref, o_ref):
    cnd = cond_ref[...]
    s = (cnd / (1.0 + jnp.exp(-cnd))).astype(BF16)
    o_ref[...] = jnp.dot(s, w_ref[...].astype(BF16), preferred_element_type=F32) + b_ref[...]


def _ada_call(cond, ada_w, ada_b):
    depth, d, n = ada_w.shape
    rows = cond.shape[0]
    bn = d
    return pl.pallas_call(
        _ada_kernel,
        grid=(depth, n // bn),
        in_specs=[
            pl.BlockSpec((rows, d), lambda i, j: (0, 0)),
            pl.BlockSpec((None, d, bn), lambda i, j: (i, 0, j)),
            pl.BlockSpec((None, 1, bn), lambda i, j: (i, 0, j)),
        ],
        out_specs=pl.BlockSpec((None, rows, bn), lambda i, j: (i, 0, j)),
        out_shape=jax.ShapeDtypeStruct((depth, rows, n), F32),
        compiler_params=pltpu.CompilerParams(
            dimension_semantics=("arbitrary", "arbitrary"), vmem_limit_bytes=VMEM_LIMIT),
    )(cond, ada_w, ada_b.reshape(depth, 1, n))


def _mod_spec(mod):
    if mod.shape[0] == 1:
        return pl.BlockSpec((None, 6, mod.shape[2]), lambda b, t: (0, 0, 0))
    return pl.BlockSpec((None, 6, mod.shape[2]), lambda b, t: (b, 0, 0))


def _const_spec(shape):
    nd = len(shape)
    return pl.BlockSpec(shape, lambda b, t: (0,) * nd, pipeline_mode=pl.Buffered(1))


def _conv_pre_kernel(x_ref, xp_ref, xn_ref, mod_ref, g_ref, w_ref, cw_ref, o_ref, *, tm, nt, d):
    t = pl.program_id(1)
    shift = mod_ref[0:1, :]
    scale = mod_ref[1:2, :]
    xe = jnp.concatenate([xp_ref[...], x_ref[...], xn_ref[...]], axis=0)
    he = _norm_mod(xe, g_ref[...], shift, scale).astype(BF16)
    cv = jnp.dot(he, w_ref[:, d:], preferred_element_type=F32)
    u = cv[:, :d] * cv[:, d:]
    row = lax.broadcasted_iota(jnp.int32, (tm + 2 * HALO, 1), 0)
    inside = ((row >= HALO) | (t > 0)) & ((row < tm + HALO) | (t < nt - 1))
    u = jnp.where(inside, u, 0.0)
    n_ext = tm + 2 * HALO
    u_prev = pltpu.roll(u, 1, axis=0)[HALO:HALO + tm]
    u_next = pltpu.roll(u, n_ext - 1, axis=0)[HALO:HALO + tm]
    conv = u_prev * cw_ref[0:1, :] + u[HALO:HALO + tm] * cw_ref[1:2, :] + u_next * cw_ref[2:3, :]
    bg = jnp.dot(he[HALO:HALO + tm], w_ref[:, :d], preferred_element_type=F32)
    o_ref[...] = (bg * conv).astype(BF16)


def _conv_pre_call(x, mod, g, w_in, conv_w, tm):
    b, l, d = x.shape
    nt = l // tm
    hb = tm // HALO
    last_hb = l // HALO - 1
    kern = functools.partial(_conv_pre_kernel, tm=tm, nt=nt, d=d)
    return pl.pallas_call(
        kern,
        grid=(b, nt),
        in_specs=[
            pl.BlockSpec((None, tm, d), lambda i, t: (i, t, 0)),
            pl.BlockSpec((None, HALO, d), lambda i, t: (i, jnp.maximum(t * hb - 1, 0), 0)),
            pl.BlockSpec((None, HALO, d), lambda i, t: (i, jnp.minimum((t + 1) * hb, last_hb), 0)),
            _mod_spec(mod),
            _const_spec((1, d)),
            _const_spec((d, 3 * d)),
            _const_spec((3, d)),
        ],
        out_specs=pl.BlockSpec((None, tm, d), lambda i, t: (i, t, 0)),
        out_shape=jax.ShapeDtypeStruct((b, l, d), BF16),
        compiler_params=pltpu.CompilerParams(
            dimension_semantics=("arbitrary", "arbitrary"), vmem_limit_bytes=VMEM_LIMIT),
    )(x, x, x, mod, g, w_in, conv_w)


def _proj_pre_kernel(x_ref, mod_ref, g_ref, w_ref, o_ref, *, n_scaled, q_scale):
    h = _norm_mod(x_ref[...], g_ref[...], mod_ref[0:1, :], mod_ref[1:2, :]).astype(BF16)
    if n_scaled:
        o_ref[:, :n_scaled] = (
            jnp.dot(h, w_ref[:, :n_scaled], preferred_element_type=F32) * q_scale).astype(BF16)
    o_ref[:, n_scaled:] = jnp.dot(h, w_ref[:, n_scaled:], preferred_element_type=F32).astype(BF16)


def _proj_pre_call(x, mod, g, w, tm, n_scaled, q_scale):
    b, l, d = x.shape
    n = w.shape[1]
    kern = functools.partial(_proj_pre_kernel, n_scaled=n_scaled, q_scale=q_scale)
    return pl.pallas_call(
        kern,
        grid=(b, l // tm),
        in_specs=[
            pl.BlockSpec((None, tm, d), lambda i, t: (i, t, 0)),
            _mod_spec(mod),
            _const_spec((1, d)),
            _const_spec((d, n)),
        ],
        out_specs=pl.BlockSpec((None, tm, n), lambda i, t: (i, t, 0)),
        out_shape=jax.ShapeDtypeStruct((b, l, n), BF16),
        compiler_params=pltpu.CompilerParams(
            dimension_semantics=("arbitrary", "arbitrary"), vmem_limit_bytes=VMEM_LIMIT),
    )(x, mod, g, w)


def _post_kernel(x_ref, y_ref, mod_ref, g_ref, wo_ref, w1_ref, w2_ref, *rest, ff_chunk, final):
    if final:
        fg_ref, o_ref = rest
    else:
        (o_ref,) = rest
    gate1 = mod_ref[2:3, :]
    shift2 = mod_ref[3:4, :]
    scale2 = mod_ref[4:5, :]
    gate2 = mod_ref[5:6, :]
    x1 = x_ref[...] + gate1 * jnp.dot(y_ref[...], wo_ref[...], preferred_element_type=F32)
    h = _norm_mod(x1, g_ref[...], shift2, scale2).astype(BF16)
    ff = w1_ref.shape[1]
    acc = None
    for f in range(0, ff, ff_chunk):
        a = jnp.dot(h, w1_ref[:, f:f + ff_chunk], preferred_element_type=F32)
        a = jnp.maximum(a, 0.0)
        a = (a * a).astype(BF16)
        part = jnp.dot(a, w2_ref[f:f + ff_chunk, :], preferred_element_type=F32)
        acc = part if acc is None else acc + part
    x2 = x1 + gate2 * acc
    if final:
        inv = lax.rsqrt(jnp.mean(x2 * x2, axis=-1, keepdims=True) + RMS_EPS)
        x2 = (x2 * inv) * fg_ref[...]
    o_ref[...] = x2


def _post_call(x, y, mod, g, w_out, w1, w2, final_g, tm, ff_chunk=1024):
    b, l, d = x.shape
    ff = w1.shape[1]
    final = final_g is not None
    kern = functools.partial(_post_kernel, ff_chunk=ff_chunk, final=final)
    in_specs = [
        pl.BlockSpec((None, tm, d), lambda i, t: (i, t, 0)),
        pl.BlockSpec((None, tm, d), lambda i, t: (i, t, 0)),
        _mod_spec(mod),
        _const_spec((1, d)),
        _const_spec((d, d)),
        _const_spec((d, ff)),
        _const_spec((ff, d)),
    ]
    args = [x, y, mod, g, w_out, w1, w2]
    if final:
        in_specs.append(_const_spec((1, d)))
        args.append(final_g)
    return pl.pallas_call(
        kern,
        grid=(b, l // tm),
        in_specs=in_specs,
        out_specs=pl.BlockSpec((None, tm, d), lambda i, t: (i, t, 0)),
        out_shape=jax.ShapeDtypeStruct((b, l, d), F32),
        compiler_params=pltpu.CompilerParams(
            dimension_semantics=("arbitrary", "arbitrary"), vmem_limit_bytes=VMEM_LIMIT),
    )(*args)


def _group_window(g, n_groups):
    return min(max(g - 1, 0), n_groups - K_ROWS // Q_ROWS)


def _rpb_lane_rows(rpb):
    nc = rpb.shape[-1]
    left = jnp.pad(rpb, ((0, 0), (0, 0), (0, 2 * GRID_W - nc)))
    right = jnp.pad(rpb, ((0, 0), (0, 0), (GRID_W, GRID_W - nc)))
    return jnp.stack([left, right], axis=2)


def _fill_bias(rv_ref, bias_ref, *, n_groups):
    rows = n_groups * Q_ROWS
    shape = (GRID_W, 2 * GRID_W)
    qc = lax.broadcasted_iota(jnp.int32, shape, 0)
    lane = lax.broadcasted_iota(jnp.int32, shape, 1)
    kc = lane & (GRID_W - 1)
    cs = jnp.clip(qc - WIN_COLS // 2, 0, GRID_W - WIN_COLS)
    col_ok = (kc >= cs) & (kc < cs + WIN_COLS)
    left = lane < GRID_W
    neg = jnp.full(shape, NEG_INF, F32)
    for cls, g in enumerate((0, 1, n_groups - 1)):
        ws = _group_window(g, n_groups)
        for hh in range(HEADS_PER_STEP):
            for qr in range(Q_ROWS):
                r = Q_ROWS * g + qr
                rs = min(max(r - WIN_ROWS // 2, 0), rows - WIN_ROWS)
                for p in range(K_ROWS // 2):
                    ka = Q_ROWS * ws + 2 * p
                    ok_a = rs <= ka < rs + WIN_ROWS
                    ok_b = rs <= ka + 1 < rs + WIN_ROWS
                    dra = min(max(ka - r + WIN_ROWS - 1, 0), 2 * WIN_ROWS - 2)
                    drb = min(max(ka + 1 - r + WIN_ROWS - 1, 0), 2 * WIN_ROWS - 2)
                    if ok_a or ok_b:
                        v = rv_ref[hh, dra, 0:1, :] + rv_ref[hh, drb, 1:2, :]
                        t = pltpu.roll(jnp.broadcast_to(v, shape), 2 * GRID_W - (WIN_COLS - 1), axis=1,
                                       stride=1, stride_axis=0)
                        ok = col_ok
                        if not ok_b:
                            ok = ok & left
                        if not ok_a:
                            ok = ok & jnp.logical_not(left)
                        tile = jnp.where(ok, t, neg)
                    else:
                        tile = neg
                    bias_ref[cls, hh, qr * GRID_W:(qr + 1) * GRID_W,
                             p * 2 * GRID_W:(p + 1) * 2 * GRID_W] = tile


def _attn_kernel(q_ref, k_ref, v_ref, kc_ref, vc_ref, rv_ref, o_ref, bias_ref, *, n_groups, head_dim):
    tq = Q_ROWS * GRID_W
    tk = K_ROWS * GRID_W

    @pl.when(pl.program_id(1) == 0)
    def _():
        _fill_bias(rv_ref, bias_ref, n_groups=n_groups)

    lane = lax.broadcasted_iota(jnp.int32, (1, HEADS_PER_STEP * head_dim), 1)
    nt = (((1,), (1,)), ((), ()))
    kc = kc_ref[...]
    vc = vc_ref[...]
    for g in range(n_groups):
        ws = _group_window(g, n_groups)
        cls = 0 if g == 0 else (2 if g == n_groups - 1 else 1)
        q = q_ref[g * tq:(g + 1) * tq, :]
        kw = k_ref[ws * tq:ws * tq + tk, :]
        vw = v_ref[ws * tq:ws * tq + tk, :]
        out = None
        for hh in range(HEADS_PER_STEP):
            in_head = (lane >= hh * head_dim) & (lane < (hh + 1) * head_dim)
            qh = jnp.where(in_head, q, jnp.zeros_like(q))
            s_loc = lax.dot_general(qh, kw, nt, preferred_element_type=F32) + bias_ref[cls, hh]
            s_ctx = lax.dot_general(qh, kc, nt, preferred_element_type=F32)
            m = jnp.maximum(jnp.max(s_loc, axis=-1, keepdims=True),
                            jnp.max(s_ctx, axis=-1, keepdims=True))
            p_loc = jnp.exp(s_loc - m)
            p_ctx = jnp.exp(s_ctx - m)
            denom = jnp.sum(p_loc, axis=-1, keepdims=True) + jnp.sum(p_ctx, axis=-1, keepdims=True)
            vh = jnp.where(in_head, vw, jnp.zeros_like(vw))
            vch = jnp.where(in_head, vc, jnp.zeros_like(vc))
            o = (jnp.dot(p_loc.astype(BF16), vh, preferred_element_type=F32)
                 + jnp.dot(p_ctx.astype(BF16), vch, preferred_element_type=F32)) / denom
            out = o if out is None else out + o
        o_ref[g * tq:(g + 1) * tq, :] = out.astype(BF16)


def _attn_call(qkv, ckv, rpb_rows):
    b, s, d3 = qkv.shape
    d = d3 // 3
    ctx_len = ckv.shape[1]
    head_dim = d // N_HEADS
    w = HEADS_PER_STEP * head_dim
    n_hp = d // w
    rows = s // GRID_W
    n_groups = rows // Q_ROWS
    tq = Q_ROWS * GRID_W
    tk = K_ROWS * GRID_W
    n_dr = rpb_rows.shape[1]
    kern = functools.partial(_attn_kernel, n_groups=n_groups, head_dim=head_dim)
    return pl.pallas_call(
        kern,
        grid=(n_hp, b),
        in_specs=[
            pl.BlockSpec((None, s, w), lambda h, i: (i, 0, h)),
            pl.BlockSpec((None, s, w), lambda h, i: (i, 0, n_hp + h)),
            pl.BlockSpec((None, s, w), lambda h, i: (i, 0, 2 * n_hp + h)),
            pl.BlockSpec((None, ctx_len, w), lambda h, i: (i, 0, h)),
            pl.BlockSpec((None, ctx_len, w), lambda h, i: (i, 0, n_hp + h)),
            pl.BlockSpec((HEADS_PER_STEP, n_dr, 2, 2 * GRID_W), lambda h, i: (h, 0, 0, 0)),
        ],
        out_specs=pl.BlockSpec((None, s, w), lambda h, i: (i, 0, h)),
        out_shape=jax.ShapeDtypeStruct((b, s, d), BF16),
        scratch_shapes=[pltpu.VMEM((3, HEADS_PER_STEP, tq, tk), F32)],
        compiler_params=pltpu.CompilerParams(
            dimension_semantics=("arbitrary", "arbitrary"), vmem_limit_bytes=VMEM_LIMIT),
    )(qkv, qkv, qkv, ckv, ckv, rpb_rows)


def kernel(x, c, ctx, c_ctx, norm1_g, norm2_g, ada_w, ada_b, conv_in_w, conv_w, conv_out_w,
           attn_qkv_w, attn_rpb, attn_out_w, mlp_w1, mlp_w2, final_g):
    b, s, d = x.shape
    ctx_len = ctx.shape[1]
    depth = ada_w.shape[0]
    assert depth == 2 and s % (GRID_W * Q_ROWS) == 0 and d % (N_HEADS * HEADS_PER_STEP) == 0
    head_dim = d // N_HEADS
    tm = 512
    tmc = ctx_len

    pad = (-(b + 1)) % 8
    cond = jnp.concatenate([c, c_ctx[None, :], jnp.zeros((pad, d), F32)], axis=0)
    mods = _ada_call(cond, ada_w, ada_b)
    mod_lat = mods[:, :b].reshape(depth, b, 6, d)
    mod_ctx = mods[:, b:b + 1].reshape(depth, 1, 6, d)

    n1 = norm1_g.reshape(depth, 1, d)
    n2 = norm2_g.reshape(depth, 1, d)
    w1 = mlp_w1.astype(BF16)
    w2 = mlp_w2.astype(BF16)

    w_in = conv_in_w[0].astype(BF16)
    w_co = conv_out_w[0].astype(BF16)
    y = _conv_pre_call(x, mod_lat[0], n1[0], w_in, conv_w[0], tm)
    yc = _conv_pre_call(ctx, mod_ctx[0], n1[0], w_in, conv_w[0], tmc)
    x = _post_call(x, y, mod_lat[0], n2[0], w_co, w1[0], w2[0], None, tm)
    ctx = _post_call(ctx, yc, mod_ctx[0], n2[0], w_co, w1[0], w2[0], None, tmc)

    w_qkv = attn_qkv_w[0].astype(BF16)
    w_ao = attn_out_w[0].astype(BF16)
    qkv = _proj_pre_call(x, mod_lat[1], n1[1], w_qkv, tm, d, head_dim ** -0.5)
    ckv = _proj_pre_call(ctx, mod_ctx[1], n1[1], w_qkv[:, d:], tmc, 0, 1.0)
    o = _attn_call(qkv, ckv, _rpb_lane_rows(attn_rpb[0]))
    return _post_call(x, o, mod_lat[1], n2[1], w_ao, w1[1], w2[1], final_g.reshape(1, d), tm)
```

```python
import functools

import jax
import jax.numpy as jnp
from jax import lax
from jax.experimental import pallas as pl
from jax.experimental.pallas import tpu as pltpu

GRID_W = 64
N_HEADS = 16
WIN_ROWS = 8
WIN_COLS = 16
RMS_EPS = 1e-6
NEG_INF = -1e30
LOG2E = 1.4426950408889634

HALO = 8
HEADS_PER_STEP = 2
Q_ROWS = 4
K_ROWS = 12
VMEM_LIMIT = 56 * 1024 * 1024

BF16 = jnp.bfloat16
F32 = jnp.float32


def _norm_mod(v, g, shift, scale):
    inv = lax.rsqrt(jnp.mean(v * v, axis=-1, keepdims=True) + RMS_EPS)
    return ((v * inv) * g) * (1.0 + scale) + shift


def _ada_kernel(cond_ref, w_ref, b_ref, o_ref):
    cnd = cond_ref[...]
    s = (cnd / (1.0 + jnp.exp(-cnd))).astype(BF16)
    o_ref[...] = jnp.dot(s, w_ref[...].astype(BF16), preferred_element_type=F32) + b_ref[...]


def _ada_call(cond, ada_w, ada_b):
    depth, d, n = ada_w.shape
    rows = cond.shape[0]
    bn = d
    return pl.pallas_call(
        _ada_kernel,
        grid=(depth, n // bn),
        in_specs=[
            pl.BlockSpec((rows, d), lambda i, j: (0, 0)),
            pl.BlockSpec((None, d, bn), lambda i, j: (i, 0, j)),
            pl.BlockSpec((None, 1, bn), lambda i, j: (i, 0, j)),
        ],
        out_specs=pl.BlockSpec((None, rows, bn), lambda i, j: (i, 0, j)),
        out_shape=jax.ShapeDtypeStruct((depth, rows, n), F32),
        compiler_params=pltpu.CompilerParams(
            dimension_semantics=("arbitrary", "arbitrary"), vmem_limit_bytes=VMEM_LIMIT),
    )(cond, ada_w, ada_b.reshape(depth, 1, n))


def _mod_spec(mod):
    if mod.shape[0] == 1:
        return pl.BlockSpec((None, 6, mod.shape[2]), lambda b, t: (0, 0, 0))
    return pl.BlockSpec((None, 6, mod.shape[2]), lambda b, t: (b, 0, 0))


def _const_spec(shape):
    nd = len(shape)
    return pl.BlockSpec(shape, lambda b, t: (0,) * nd, pipeline_mode=pl.Buffered(1))


def _conv_pre_kernel(x_ref, xp_ref, xn_ref, mod_ref, g_ref, w_ref, cw_ref, o_ref, *, tm, nt, d):
    t = pl.program_id(1)
    shift = mod_ref[0:1, :]
    scale = mod_ref[1:2, :]
    xe = jnp.concatenate([xp_ref[...], x_ref[...], xn_ref[...]], axis=0)
    he = _norm_mod(xe, g_ref[...], shift, scale).astype(BF16)
    cv = jnp.dot(he, w_ref[:, d:], preferred_element_type=F32)
    u = cv[:, :d] * cv[:, d:]
    row = lax.broadcasted_iota(jnp.int32, (tm + 2 * HALO, 1), 0)
    inside = ((row >= HALO) | (t > 0)) & ((row < tm + HALO) | (t < nt - 1))
    u = jnp.where(inside, u, 0.0)
    n_ext = tm + 2 * HALO
    u_prev = pltpu.roll(u, 1, axis=0)[HALO:HALO + tm]
    u_next = pltpu.roll(u, n_ext - 1, axis=0)[HALO:HALO + tm]
    conv = u_prev * cw_ref[0:1, :] + u[HALO:HALO + tm] * cw_ref[1:2, :] + u_next * cw_ref[2:3, :]
    bg = jnp.dot(he[HALO:HALO + tm], w_ref[:, :d], preferred_element_type=F32)
    o_ref[...] = (bg * conv).astype(BF16)


def _conv_pre_call(x, mod, g, w_in, conv_w, tm):
    b, l, d = x.shape
    nt = l // tm
    hb = tm // HALO
    last_hb = l // HALO - 1
    kern = functools.partial(_conv_pre_kernel, tm=tm, nt=nt, d=d)
    return pl.pallas_call(
        kern,
        grid=(b, nt),
        in_specs=[
            pl.BlockSpec((None, tm, d), lambda i, t: (i, t, 0)),
            pl.BlockSpec((None, HALO, d), lambda i, t: (i, jnp.maximum(t * hb - 1, 0), 0)),
            pl.BlockSpec((None, HALO, d), lambda i, t: (i, jnp.minimum((t + 1) * hb, last_hb), 0)),
            _mod_spec(mod),
            _const_spec((1, d)),
            _const_spec((d, 3 * d)),
            _const_spec((3, d)),
        ],
        out_specs=pl.BlockSpec((None, tm, d), lambda i, t: (i, t, 0)),
        out_shape=jax.ShapeDtypeStruct((b, l, d), BF16),
        compiler_params=pltpu.CompilerParams(
            dimension_semantics=("arbitrary", "arbitrary"), vmem_limit_bytes=VMEM_LIMIT),
    )(x, x, x, mod, g, w_in, conv_w)


def _proj_pre_kernel(x_ref, mod_ref, g_ref, w_ref, o_ref, *, n_scaled, q_scale):
    h = _norm_mod(x_ref[...], g_ref[...], mod_ref[0:1, :], mod_ref[1:2, :]).astype(BF16)
    if n_scaled:
        o_ref[:, :n_scaled] = (
            jnp.dot(h, w_ref[:, :n_scaled], preferred_element_type=F32) * q_scale).astype(BF16)
    o_ref[:, n_scaled:] = jnp.dot(h, w_ref[:, n_scaled:], preferred_element_type=F32).astype(BF16)


def _proj_pre_call(x, mod, g, w, tm, n_scaled, q_scale):
    b, l, d = x.shape
    n = w.shape[1]
    kern = functools.partial(_proj_pre_kernel, n_scaled=n_scaled, q_scale=q_scale)
    return pl.pallas_call(
        kern,
        grid=(b, l // tm),
        in_specs=[
            pl.BlockSpec((None, tm, d), lambda i, t: (i, t, 0)),
            _mod_spec(mod),
            _const_spec((1, d)),
            _const_spec((d, n)),
        ],
        out_specs=pl.BlockSpec((None, tm, n), lambda i, t: (i, t, 0)),
        out_shape=jax.ShapeDtypeStruct((b, l, n), BF16),
        compiler_params=pltpu.CompilerParams(
            dimension_semantics=("arbitrary", "arbitrary"), vmem_limit_bytes=VMEM_LIMIT),
    )(x, mod, g, w)


def _post_kernel(x_ref, y_ref, mod_ref, g_ref, wo_ref, w1_ref, w2_ref, *rest, ff_chunk, final):
    if final:
        fg_ref, o_ref = rest
    else:
        (o_ref,) = rest
    gate1 = mod_ref[2:3, :]
    shift2 = mod_ref[3:4, :]
    scale2 = mod_ref[4:5, :]
    gate2 = mod_ref[5:6, :]
    x1 = x_ref[...] + gate1 * jnp.dot(y_ref[...], wo_ref[...], preferred_element_type=F32)
    h = _norm_mod(x1, g_ref[...], shift2, scale2).astype(BF16)
    ff = w1_ref.shape[1]
    acc = None
    for f in range(0, ff, ff_chunk):
        a = jnp.dot(h, w1_ref[:, f:f + ff_chunk], preferred_element_type=F32)
        a = jnp.maximum(a, 0.0)
        a = (a * a).astype(BF16)
        part = jnp.dot(a, w2_ref[f:f + ff_chunk, :], preferred_element_type=F32)
        acc = part if acc is None else acc + part
    x2 = x1 + gate2 * acc
    if final:
        inv = lax.rsqrt(jnp.mean(x2 * x2, axis=-1, keepdims=True) + RMS_EPS)
        x2 = (x2 * inv) * fg_ref[...]
    o_ref[...] = x2


def _post_call(x, y, mod, g, w_out, w1, w2, final_g, tm, ff_chunk=1024):
    b, l, d = x.shape
    ff = w1.shape[1]
    final = final_g is not None
    kern = functools.partial(_post_kernel, ff_chunk=ff_chunk, final=final)
    in_specs = [
        pl.BlockSpec((None, tm, d), lambda i, t: (i, t, 0)),
        pl.BlockSpec((None, tm, d), lambda i, t: (i, t, 0)),
        _mod_spec(mod),
        _const_spec((1, d)),
        _const_spec((d, d)),
        _const_spec((d, ff)),
        _const_spec((ff, d)),
    ]
    args = [x, y, mod, g, w_out, w1, w2]
    if final:
        in_specs.append(_const_spec((1, d)))
        args.append(final_g)
    return pl.pallas_call(
        kern,
        grid=(b, l // tm),
        in_specs=in_specs,
        out_specs=pl.BlockSpec((None, tm, d), lambda i, t: (i, t, 0)),
        out_shape=jax.ShapeDtypeStruct((b, l, d), F32),
        compiler_params=pltpu.CompilerParams(
            dimension_semantics=("arbitrary", "arbitrary"), vmem_limit_bytes=VMEM_LIMIT),
    )(*args)


def _group_window(g, n_groups):
    return min(max(g - 1, 0), n_groups - K_ROWS // Q_ROWS)


def _rpb_lane_rows(rpb):
    nc = rpb.shape[-1]
    left = jnp.pad(rpb, ((0, 0), (0, 0), (0, 2 * GRID_W - nc)))
    right = jnp.pad(rpb, ((0, 0), (0, 0), (GRID_W, GRID_W - nc)))
    return jnp.stack([left, right], axis=2)


def _fill_bias(rv_ref, bias_ref, *, n_groups):
    rows = n_groups * Q_ROWS
    shape = (GRID_W, 2 * GRID_W)
    qc = lax.broadcasted_iota(jnp.int32, shape, 0)
    lane = lax.broadcasted_iota(jnp.int32, shape, 1)
    kc = lane & (GRID_W - 1)
    cs = jnp.clip(qc - WIN_COLS // 2, 0, GRID_W - WIN_COLS)
    col_ok = (kc >= cs) & (kc < cs + WIN_COLS)
    left = lane < GRID_W
    neg = jnp.full(shape, NEG_INF, F32)
    for cls, g in enumerate((0, 1, n_groups - 1)):
        ws = _group_window(g, n_groups)
        for hh in range(HEADS_PER_STEP):
            for qr in range(Q_ROWS):
                r = Q_ROWS * g + qr
                rs = min(max(r - WIN_ROWS // 2, 0), rows - WIN_ROWS)
                for p in range(K_ROWS // 2):
                    ka = Q_ROWS * ws + 2 * p
                    ok_a = rs <= ka < rs + WIN_ROWS
                    ok_b = rs <= ka + 1 < rs + WIN_ROWS
                    dra = min(max(ka - r + WIN_ROWS - 1, 0), 2 * WIN_ROWS - 2)
                    drb = min(max(ka + 1 - r + WIN_ROWS - 1, 0), 2 * WIN_ROWS - 2)
                    if ok_a or ok_b:
                        v = (rv_ref[hh, dra, 0:1, :] + rv_ref[hh, drb, 1:2, :]) * LOG2E
                        t = pltpu.roll(jnp.broadcast_to(v, shape), 2 * GRID_W - (WIN_COLS - 1), axis=1,
                                       stride=1, stride_axis=0)
                        ok = col_ok
                        if not ok_b:
                            ok = ok & left
                        if not ok_a:
                            ok = ok & jnp.logical_not(left)
                        tile = jnp.where(ok, t, neg)
                    else:
                        tile = neg
                    bias_ref[cls, hh, qr * GRID_W:(qr + 1) * GRID_W,
                             p * 2 * GRID_W:(p + 1) * 2 * GRID_W] = tile


def _attn_kernel(q_ref, k_ref, v_ref, kc_ref, vc_ref, rv_ref, o_ref, bias_ref, *, n_groups, head_dim):
    tq = Q_ROWS * GRID_W
    tk = K_ROWS * GRID_W

    @pl.when(pl.program_id(1) == 0)
    def _():
        _fill_bias(rv_ref, bias_ref, n_groups=n_groups)

    rows = n_groups * Q_ROWS
    lane = lax.broadcasted_iota(jnp.int32, (1, HEADS_PER_STEP * head_dim), 1)
    nt = (((1,), (1,)), ((), ()))
    kc = kc_ref[...]
    vc = vc_ref[...]

    def scores(g, hh):
        ws = _group_window(g, n_groups)
        q = q_ref[g * tq:(g + 1) * tq, :]
        in_head = (lane >= hh * head_dim) & (lane < (hh + 1) * head_dim)
        qh = jnp.where(in_head, q, jnp.zeros_like(q))
        s_loc = lax.dot_general(qh, k_ref[ws * tq:ws * tq + tk, :], nt, preferred_element_type=F32)
        s_ctx = lax.dot_general(qh, kc, nt, preferred_element_type=F32)
        return s_loc, s_ctx

    def attend(g, hh, s_loc, s_ctx):
        ws = _group_window(g, n_groups)
        cls = 0 if g == 0 else (2 if g == n_groups - 1 else 1)
        p_rows, d_rows = [], []
        for qr in range(Q_ROWS):
            r = Q_ROWS * g + qr
            rel = min(max(r - WIN_ROWS // 2, 0), rows - WIN_ROWS) - Q_ROWS * ws
            lo = (rel * GRID_W) // 128 * 128
            hi = -((-(rel + WIN_ROWS) * GRID_W) // 128) * 128
            rsl = slice(qr * GRID_W, (qr + 1) * GRID_W)
            sl = s_loc[rsl, lo:hi] + bias_ref[cls, hh, rsl, lo:hi]
            sc = s_ctx[rsl]
            m = jnp.maximum(jnp.max(sl, axis=-1, keepdims=True), jnp.max(sc, axis=-1, keepdims=True))
            pl_ = jnp.exp2(sl - m)
            pc = jnp.exp2(sc - m)
            d_rows.append(jnp.sum(pl_, axis=-1, keepdims=True) + jnp.sum(pc, axis=-1, keepdims=True))
            pieces = []
            if lo > 0:
                pieces.append(jnp.zeros((GRID_W, lo), BF16))
            pieces.append(pl_.astype(BF16))
            if hi < tk:
                pieces.append(jnp.zeros((GRID_W, tk - hi), BF16))
            pieces.append(pc.astype(BF16))
            p_rows.append(jnp.concatenate(pieces, axis=1))
        p = jnp.concatenate(p_rows, axis=0)
        denom = jnp.concatenate(d_rows, axis=0)
        o = (jnp.dot(p[:, :tk], v_ref[ws * tq:ws * tq + tk, :], preferred_element_type=F32)
             + jnp.dot(p[:, tk:], vc, preferred_element_type=F32))
        return o / denom

    units = [(g, hh) for g in range(n_groups) for hh in range(HEADS_PER_STEP)]
    nxt = scores(*units[0])
    outs = []
    for u, (g, hh) in enumerate(units):
        cur = nxt
        if u + 1 < len(units):
            nxt = scores(*units[u + 1])
        outs.append(attend(g, hh, *cur))
        if hh == HEADS_PER_STEP - 1:
            out = outs[-1]
            for h2 in range(HEADS_PER_STEP - 2, -1, -1):
                out = jnp.where(lane < (h2 + 1) * head_dim, outs[h2], out)
            o_ref[g * tq:(g + 1) * tq, :] = out.astype(BF16)
            outs = []


def _attn_call(qkv, ckv, rpb_rows):
    b, s, d3 = qkv.shape
    d = d3 // 3
    ctx_len = ckv.shape[1]
    head_dim = d // N_HEADS
    w = HEADS_PER_STEP * head_dim
    n_hp = d // w
    rows = s // GRID_W
    n_groups = rows // Q_ROWS
    tq = Q_ROWS * GRID_W
    tk = K_ROWS * GRID_W
    n_dr = rpb_rows.shape[1]
    kern = functools.partial(_attn_kernel, n_groups=n_groups, head_dim=head_dim)
    return pl.pallas_call(
        kern,
        grid=(n_hp, b),
        in_specs=[
            pl.BlockSpec((None, s, w), lambda h, i: (i, 0, h)),
            pl.BlockSpec((None, s, w), lambda h, i: (i, 0, n_hp + h)),
            pl.BlockSpec((None, s, w), lambda h, i: (i, 0, 2 * n_hp + h)),
            pl.BlockSpec((None, ctx_len, w), lambda h, i: (i, 0, h)),
            pl.BlockSpec((None, ctx_len, w), lambda h, i: (i, 0, n_hp + h)),
            pl.BlockSpec((HEADS_PER_STEP, n_dr, 2, 2 * GRID_W), lambda h, i: (h, 0, 0, 0)),
        ],
        out_specs=pl.BlockSpec((None, s, w), lambda h, i: (i, 0, h)),
        out_shape=jax.ShapeDtypeStruct((b, s, d), BF16),
        scratch_shapes=[pltpu.VMEM((3, HEADS_PER_STEP, tq, tk), F32)],
        compiler_params=pltpu.CompilerParams(
            dimension_semantics=("arbitrary", "arbitrary"), vmem_limit_bytes=VMEM_LIMIT),
    )(qkv, qkv, qkv, ckv, ckv, rpb_rows)


def kernel(x, c, ctx, c_ctx, norm1_g, norm2_g, ada_w, ada_b, conv_in_w, conv_w, conv_out_w,
           attn_qkv_w, attn_rpb, attn_out_w, mlp_w1, mlp_w2, final_g):
    b, s, d = x.shape
    ctx_len = ctx.shape[1]
    depth = ada_w.shape[0]
    assert depth == 2 and s % (GRID_W * Q_ROWS) == 0 and d % (N_HEADS * HEADS_PER_STEP) == 0
    head_dim = d // N_HEADS
    tm = 512
    tmc = ctx_len

    pad = (-(b + 1)) % 8
    cond = jnp.concatenate([c, c_ctx[None, :], jnp.zeros((pad, d), F32)], axis=0)
    mods = _ada_call(cond, ada_w, ada_b)
    mod_lat = mods[:, :b].reshape(depth, b, 6, d)
    mod_ctx = mods[:, b:b + 1].reshape(depth, 1, 6, d)

    n1 = norm1_g.reshape(depth, 1, d)
    n2 = norm2_g.reshape(depth, 1, d)
    w1 = mlp_w1.astype(BF16)
    w2 = mlp_w2.astype(BF16)

    w_in = conv_in_w[0].astype(BF16)
    w_co = conv_out_w[0].astype(BF16)
    y = _conv_pre_call(x, mod_lat[0], n1[0], w_in, conv_w[0], tm)
    yc = _conv_pre_call(ctx, mod_ctx[0], n1[0], w_in, conv_w[0], tmc)
    x = _post_call(x, y, mod_lat[0], n2[0], w_co, w1[0], w2[0], None, tm)
    ctx = _post_call(ctx, yc, mod_ctx[0], n2[0], w_co, w1[0], w2[0], None, tmc)

    w_qkv = attn_qkv_w[0].astype(BF16)
    w_ao = attn_out_w[0].astype(BF16)
    qkv = _proj_pre_call(x, mod_lat[1], n1[1], w_qkv, tm, d, head_dim ** -0.5 * LOG2E)
    ckv = _proj_pre_call(ctx, mod_ctx[1], n1[1], w_qkv[:, d:], tmc, 0, 1.0)
    o = _attn_call(qkv, ckv, _rpb_lane_rows(attn_rpb[0]))
    return _post_call(x, o, mod_lat[1], n2[1], w_ao, w1[1], w2[1], final_g.reshape(1, d), tm)
```

```python
import functools

import jax
import jax.numpy as jnp
from jax import lax
from jax.experimental import pallas as pl
from jax.experimental.pallas import tpu as pltpu

GRID_W = 64
N_HEADS = 16
WIN_ROWS = 8
WIN_COLS = 16
RMS_EPS = 1e-6
NEG_INF = -1e30
LOG2E = 1.4426950408889634

HALO = 8
HEADS_PER_STEP = 2
Q_ROWS, Q_COLS = 8, 16
K_ROWS, K_COLS = 16, 32
ROWS_PER_TILE = 128 // K_COLS
BLOCK_ROWS = 4
VMEM_LIMIT = 56 * 1024 * 1024

BF16 = jnp.bfloat16
F32 = jnp.float32


def _norm_mod(v, g, shift, scale):
    inv = lax.rsqrt(jnp.mean(v * v, axis=-1, keepdims=True) + RMS_EPS)
    return ((v * inv) * g) * (1.0 + scale) + shift


def _ada_kernel(cond_ref, w_ref, b_ref, o_ref):
    cnd = cond_ref[...]
    s = (cnd / (1.0 + jnp.exp(-cnd))).astype(BF16)
    o_ref[...] = jnp.dot(s, w_ref[...].astype(BF16), preferred_element_type=F32) + b_ref[...]


def _ada_call(cond, ada_w, ada_b):
    depth, d, n = ada_w.shape
    rows = cond.shape[0]
    bn = d
    return pl.pallas_call(
        _ada_kernel,
        grid=(depth, n // bn),
        in_specs=[
            pl.BlockSpec((rows, d), lambda i, j: (0, 0)),
            pl.BlockSpec((None, d, bn), lambda i, j: (i, 0, j)),
            pl.BlockSpec((None, 1, bn), lambda i, j: (i, 0, j)),
        ],
        out_specs=pl.BlockSpec((None, rows, bn), lambda i, j: (i, 0, j)),
        out_shape=jax.ShapeDtypeStruct((depth, rows, n), F32),
        compiler_params=pltpu.CompilerParams(
            dimension_semantics=("arbitrary", "arbitrary"), vmem_limit_bytes=VMEM_LIMIT),
    )(cond, ada_w, ada_b.reshape(depth, 1, n))


def _mod_spec(mod):
    if mod.shape[0] == 1:
        return pl.BlockSpec((None, 6, mod.shape[2]), lambda b, t: (0, 0, 0))
    return pl.BlockSpec((None, 6, mod.shape[2]), lambda b, t: (b, 0, 0))


def _const_spec(shape):
    nd = len(shape)
    return pl.BlockSpec(shape, lambda b, t: (0,) * nd, pipeline_mode=pl.Buffered(1))


def _conv_pre_kernel(x_ref, xp_ref, xn_ref, mod_ref, g_ref, w_ref, cw_ref, o_ref, *, tm, nt, d):
    t = pl.program_id(1)
    shift = mod_ref[0:1, :]
    scale = mod_ref[1:2, :]
    xe = jnp.concatenate([xp_ref[...], x_ref[...], xn_ref[...]], axis=0)
    he = _norm_mod(xe, g_ref[...], shift, scale).astype(BF16)
    cv = jnp.dot(he, w_ref[:, d:], preferred_element_type=F32)
    u = cv[:, :d] * cv[:, d:]
    row = lax.broadcasted_iota(jnp.int32, (tm + 2 * HALO, 1), 0)
    inside = ((row >= HALO) | (t > 0)) & ((row < tm + HALO) | (t < nt - 1))
    u = jnp.where(inside, u, 0.0)
    n_ext = tm + 2 * HALO
    u_prev = pltpu.roll(u, 1, axis=0)[HALO:HALO + tm]
    u_next = pltpu.roll(u, n_ext - 1, axis=0)[HALO:HALO + tm]
    conv = u_prev * cw_ref[0:1, :] + u[HALO:HALO + tm] * cw_ref[1:2, :] + u_next * cw_ref[2:3, :]
    bg = jnp.dot(he[HALO:HALO + tm], w_ref[:, :d], preferred_element_type=F32)
    o_ref[...] = (bg * conv).astype(BF16)


def _conv_pre_call(x, mod, g, w_in, conv_w, tm):
    b, l, d = x.shape
    nt = l // tm
    hb = tm // HALO
    last_hb = l // HALO - 1
    kern = functools.partial(_conv_pre_kernel, tm=tm, nt=nt, d=d)
    return pl.pallas_call(
        kern,
        grid=(b, nt),
        in_specs=[
            pl.BlockSpec((None, tm, d), lambda i, t: (i, t, 0)),
            pl.BlockSpec((None, HALO, d), lambda i, t: (i, jnp.maximum(t * hb - 1, 0), 0)),
            pl.BlockSpec((None, HALO, d), lambda i, t: (i, jnp.minimum((t + 1) * hb, last_hb), 0)),
            _mod_spec(mod),
            _const_spec((1, d)),
            _const_spec((d, 3 * d)),
            _const_spec((3, d)),
        ],
        out_specs=pl.BlockSpec((None, tm, d), lambda i, t: (i, t, 0)),
        out_shape=jax.ShapeDtypeStruct((b, l, d), BF16),
        compiler_params=pltpu.CompilerParams(
            dimension_semantics=("arbitrary", "arbitrary"), vmem_limit_bytes=VMEM_LIMIT),
    )(x, x, x, mod, g, w_in, conv_w)


def _proj_pre_kernel(x_ref, mod_ref, g_ref, w_ref, o_ref, *, n_scaled, q_scale):
    h = _norm_mod(x_ref[...], g_ref[...], mod_ref[0:1, :], mod_ref[1:2, :]).astype(BF16)
    if n_scaled:
        o_ref[:, :n_scaled] = (
            jnp.dot(h, w_ref[:, :n_scaled], preferred_element_type=F32) * q_scale).astype(BF16)
    o_ref[:, n_scaled:] = jnp.dot(h, w_ref[:, n_scaled:], preferred_element_type=F32).astype(BF16)


def _proj_pre_call(x, mod, g, w, tm, n_scaled, q_scale):
    b, l, d = x.shape
    n = w.shape[1]
    kern = functools.partial(_proj_pre_kernel, n_scaled=n_scaled, q_scale=q_scale)
    return pl.pallas_call(
        kern,
        grid=(b, l // tm),
        in_specs=[
            pl.BlockSpec((None, tm, d), lambda i, t: (i, t, 0)),
            _mod_spec(mod),
            _const_spec((1, d)),
            _const_spec((d, n)),
        ],
        out_specs=pl.BlockSpec((None, tm, n), lambda i, t: (i, t, 0)),
        out_shape=jax.ShapeDtypeStruct((b, l, n), BF16),
        compiler_params=pltpu.CompilerParams(
            dimension_semantics=("arbitrary", "arbitrary"), vmem_limit_bytes=VMEM_LIMIT),
    )(x, mod, g, w)


def _post_kernel(x_ref, y_ref, mod_ref, g_ref, wo_ref, w1_ref, w2_ref, *rest, ff_chunk, final):
    if final:
        fg_ref, o_ref = rest
    else:
        (o_ref,) = rest
    gate1 = mod_ref[2:3, :]
    shift2 = mod_ref[3:4, :]
    scale2 = mod_ref[4:5, :]
    gate2 = mod_ref[5:6, :]
    x1 = x_ref[...] + gate1 * jnp.dot(y_ref[...], wo_ref[...], preferred_element_type=F32)
    h = _norm_mod(x1, g_ref[...], shift2, scale2).astype(BF16)
    ff = w1_ref.shape[1]
    acc = None
    for f in range(0, ff, ff_chunk):
        a = jnp.dot(h, w1_ref[:, f:f + ff_chunk], preferred_element_type=F32)
        a = jnp.maximum(a, 0.0)
        a = (a * a).astype(BF16)
        part = jnp.dot(a, w2_ref[f:f + ff_chunk, :], preferred_element_type=F32)
        acc = part if acc is None else acc + part
    x2 = x1 + gate2 * acc
    if final:
        inv = lax.rsqrt(jnp.mean(x2 * x2, axis=-1, keepdims=True) + RMS_EPS)
        x2 = (x2 * inv) * fg_ref[...]
    o_ref[...] = x2


def _post_call(x, y, mod, g, w_out, w1, w2, final_g, tm, ff_chunk=1024):
    b, l, d = x.shape
    ff = w1.shape[1]
    final = final_g is not None
    kern = functools.partial(_post_kernel, ff_chunk=ff_chunk, final=final)
    in_specs = [
        pl.BlockSpec((None, tm, d), lambda i, t: (i, t, 0)),
        pl.BlockSpec((None, tm, d), lambda i, t: (i, t, 0)),
        _mod_spec(mod),
        _const_spec((1, d)),
        _const_spec((d, d)),
        _const_spec((d, ff)),
        _const_spec((ff, d)),
    ]
    args = [x, y, mod, g, w_out, w1, w2]
    if final:
        in_specs.append(_const_spec((1, d)))
        args.append(final_g)
    return pl.pallas_call(
        kern,
        grid=(b, l // tm),
        in_specs=in_specs,
        out_specs=pl.BlockSpec((None, tm, d), lambda i, t: (i, t, 0)),
        out_shape=jax.ShapeDtypeStruct((b, l, d), F32),
        compiler_params=pltpu.CompilerParams(
            dimension_semantics=("arbitrary", "arbitrary"), vmem_limit_bytes=VMEM_LIMIT),
    )(*args)


def _key_row_start(g, rows):
    return min(max(Q_ROWS * g - WIN_ROWS // 2, 0), rows - K_ROWS)


def _key_col_start(cb):
    lo = min(max(Q_COLS * cb - WIN_COLS // 2, 0), GRID_W - WIN_COLS)
    hi = min(max(Q_COLS * (cb + 1) - 1 - WIN_COLS // 2, 0), GRID_W - WIN_COLS) + WIN_COLS
    start = min(lo // HALO * HALO, GRID_W - K_COLS)
    assert start <= lo and hi <= start + K_COLS
    return start


def _window_row(r, rows):
    return min(max(r - WIN_ROWS // 2, 0), rows - WIN_ROWS)


def _visible_tiles(g, blk, rows):
    rel = [_window_row(Q_ROWS * g + BLOCK_ROWS * blk + i, rows) - _key_row_start(g, rows)
           for i in range(BLOCK_ROWS)]
    return min(rel) // ROWS_PER_TILE, -(-(max(rel) + WIN_ROWS) // ROWS_PER_TILE)


def _rpb_lane_rows(rpb):
    nc = rpb.shape[-1]
    placed = [jnp.pad(rpb, ((0, 0), (0, 0), (K_COLS * j, 128 - K_COLS * j - nc))) for j in range(ROWS_PER_TILE)]
    return jnp.stack(placed, axis=2)


def _fill_bias(rv_ref, bias_ref, *, rows):
    n_groups = rows // Q_ROWS
    shape = (Q_COLS, 128)
    qcl = lax.broadcasted_iota(jnp.int32, shape, 0)
    lane = lax.broadcasted_iota(jnp.int32, shape, 1)
    t = lane & (K_COLS - 1)
    neg = jnp.full(shape, NEG_INF, F32)
    for cls, g in enumerate((0, 1, n_groups - 1)):
        kr0 = _key_row_start(g, rows)
        for cb in range(GRID_W // Q_COLS):
            kstart = _key_col_start(cb)
            cs = jnp.clip(qcl + (Q_COLS * cb - WIN_COLS // 2), 0, GRID_W - WIN_COLS) - kstart
            col_ok = (t >= cs) & (t < cs + WIN_COLS)
            delta = kstart - Q_COLS * cb + WIN_COLS - 1
            for hh in range(HEADS_PER_STEP):
                for qr in range(Q_ROWS):
                    r = Q_ROWS * g + qr
                    rs = _window_row(r, rows)
                    t_lo, t_hi = _visible_tiles(g, qr // BLOCK_ROWS, rows)
                    for tile in range(t_lo, t_hi):
                        krow = [kr0 + ROWS_PER_TILE * tile + j for j in range(ROWS_PER_TILE)]
                        seen = [j for j in range(ROWS_PER_TILE) if rs <= krow[j] < rs + WIN_ROWS]
                        dst = (cls, hh, cb, slice(qr * Q_COLS, (qr + 1) * Q_COLS),
                               slice(tile * 128, (tile + 1) * 128))
                        if not seen:
                            bias_ref[dst] = neg
                            continue
                        v = None
                        for j in range(ROWS_PER_TILE):
                            dr = min(max(krow[j] - r + WIN_ROWS - 1, 0), 2 * WIN_ROWS - 2)
                            piece = rv_ref[hh, dr, j:j + 1, :]
                            v = piece if v is None else v + piece
                        rolled = pltpu.roll(jnp.broadcast_to(v * LOG2E, shape), (-delta) % 128, axis=1,
                                            stride=1, stride_axis=0)
                        ok = col_ok & (lane >= K_COLS * seen[0]) & (lane < K_COLS * (seen[-1] + 1))
                        bias_ref[dst] = jnp.where(ok, rolled, neg)


def _attn_kernel(q_ref, k_ref, v_ref, kc_ref, vc_ref, rv_ref, o_ref, bias_ref, kf_ref, vf_ref, *,
                 rows, head_dim):
    n_groups = rows // Q_ROWS
    n_cb = GRID_W // Q_COLS
    nq = Q_ROWS * Q_COLS
    nk = K_ROWS * K_COLS

    @pl.when(pl.program_id(1) == 0)
    def _():
        _fill_bias(rv_ref, bias_ref, rows=rows)

    kf_ref[...] = k_ref[...].astype(F32)
    vf_ref[...] = v_ref[...].astype(F32)

    lane = lax.broadcasted_iota(jnp.int32, (1, HEADS_PER_STEP * head_dim), 1)
    nt = (((1,), (1,)), ((), ()))
    kc = kc_ref[...]
    vc = vc_ref[...]

    def window(src_ref, g, cb):
        r0 = _key_row_start(g, rows)
        c0 = _key_col_start(cb)
        pieces = [src_ref[(r0 + j) * GRID_W + c0:(r0 + j) * GRID_W + c0 + K_COLS, :] for j in range(K_ROWS)]
        return jnp.concatenate(pieces, axis=0).astype(BF16)

    def scores(g, cb):
        pieces = [q_ref[(Q_ROWS * g + qr) * GRID_W + Q_COLS * cb:
                        (Q_ROWS * g + qr) * GRID_W + Q_COLS * (cb + 1), :] for qr in range(Q_ROWS)]
        q = jnp.concatenate(pieces, axis=0)
        zero = jnp.zeros_like(q)
        qh = jnp.concatenate(
            [jnp.where((lane >= hh * head_dim) & (lane < (hh + 1) * head_dim), q, zero)
             for hh in range(HEADS_PER_STEP)], axis=0)
        s_loc = lax.dot_general(qh, window(kf_ref, g, cb), nt, preferred_element_type=F32)
        s_ctx = lax.dot_general(qh, kc, nt, preferred_element_type=F32)
        return s_loc, s_ctx

    def attend(g, cb, s_loc, s_ctx):
        cls = 0 if g == 0 else (2 if g == n_groups - 1 else 1)
        p_blocks, d_blocks = [], []
        for hh in range(HEADS_PER_STEP):
            for blk in range(Q_ROWS // BLOCK_ROWS):
                t_lo, t_hi = _visible_tiles(g, blk, rows)
                nb = BLOCK_ROWS * Q_COLS
                qsl = slice(blk * nb, (blk + 1) * nb)
                rsl = slice(hh * nq + qsl.start, hh * nq + qsl.stop)
                tiles = [s_loc[rsl, t * 128:(t + 1) * 128] + bias_ref[cls, hh, cb, qsl, t * 128:(t + 1) * 128]
                         for t in range(t_lo, t_hi)]
                tiles += [s_ctx[rsl, t * 128:(t + 1) * 128] for t in range(s_ctx.shape[1] // 128)]
                m = jnp.max(functools.reduce(jnp.maximum, tiles), axis=-1, keepdims=True)
                probs = [jnp.exp2(t - m) for t in tiles]
                d_blocks.append(jnp.sum(functools.reduce(jnp.add, probs), axis=-1, keepdims=True))
                n_loc = t_hi - t_lo
                pieces = [jnp.zeros((nb, 128), BF16)] * t_lo
                pieces += [t.astype(BF16) for t in probs[:n_loc]]
                pieces += [jnp.zeros((nb, 128), BF16)] * (nk // 128 - t_hi)
                pieces += [t.astype(BF16) for t in probs[n_loc:]]
                p_blocks.append(jnp.concatenate(pieces, axis=1))
        p = jnp.concatenate(p_blocks, axis=0)
        denom = jnp.concatenate(d_blocks, axis=0)
        o = (jnp.dot(p[:, :nk], window(vf_ref, g, cb), preferred_element_type=F32)
             + jnp.dot(p[:, nk:], vc, preferred_element_type=F32)) / denom
        out = o[(HEADS_PER_STEP - 1) * nq:]
        for hh in range(HEADS_PER_STEP - 2, -1, -1):
            out = jnp.where(lane < (hh + 1) * head_dim, o[hh * nq:(hh + 1) * nq], out)
        out = out.astype(BF16)
        for qr in range(Q_ROWS):
            tok = (Q_ROWS * g + qr) * GRID_W + Q_COLS * cb
            o_ref[tok:tok + Q_COLS, :] = out[qr * Q_COLS:(qr + 1) * Q_COLS]

    units = [(g, cb) for g in range(n_groups) for cb in range(n_cb)]
    nxt = scores(*units[0])
    for u, unit in enumerate(units):
        cur = nxt
        if u + 1 < len(units):
            nxt = scores(*units[u + 1])
        attend(*unit, *cur)


def _attn_call(qkv, ckv, rpb_rows):
    b, s, d3 = qkv.shape
    d = d3 // 3
    ctx_len = ckv.shape[1]
    head_dim = d // N_HEADS
    w = HEADS_PER_STEP * head_dim
    assert w == 128
    n_hp = d // w
    rows = s // GRID_W
    n_dr = rpb_rows.shape[1]
    kern = functools.partial(_attn_kernel, rows=rows, head_dim=head_dim)
    return pl.pallas_call(
        kern,
        grid=(n_hp, b),
        in_specs=[
            pl.BlockSpec((None, s, w), lambda h, i: (i, 0, h)),
            pl.BlockSpec((None, s, w), lambda h, i: (i, 0, n_hp + h)),
            pl.BlockSpec((None, s, w), lambda h, i: (i, 0, 2 * n_hp + h)),
            pl.BlockSpec((None, ctx_len, w), lambda h, i: (i, 0, h)),
            pl.BlockSpec((None, ctx_len, w), lambda h, i: (i, 0, n_hp + h)),
            pl.BlockSpec((HEADS_PER_STEP, n_dr, ROWS_PER_TILE, 128), lambda h, i: (h, 0, 0, 0)),
        ],
        out_specs=pl.BlockSpec((None, s, w), lambda h, i: (i, 0, h)),
        out_shape=jax.ShapeDtypeStruct((b, s, d), BF16),
        scratch_shapes=[
            pltpu.VMEM((3, HEADS_PER_STEP, GRID_W // Q_COLS, Q_ROWS * Q_COLS, K_ROWS * K_COLS), F32),
            pltpu.VMEM((s, w), F32),
            pltpu.VMEM((s, w), F32),
        ],
        compiler_params=pltpu.CompilerParams(
            dimension_semantics=("arbitrary", "arbitrary"), vmem_limit_bytes=VMEM_LIMIT),
    )(qkv, qkv, qkv, ckv, ckv, rpb_rows)


def kernel(x, c, ctx, c_ctx, norm1_g, norm2_g, ada_w, ada_b, conv_in_w, conv_w, conv_out_w,
           attn_qkv_w, attn_rpb, attn_out_w, mlp_w1, mlp_w2, final_g):
    b, s, d = x.shape
    ctx_len = ctx.shape[1]
    depth = ada_w.shape[0]
    assert depth == 2 and s % (GRID_W * Q_ROWS) == 0 and s // GRID_W >= K_ROWS and d % N_HEADS == 0
    head_dim = d // N_HEADS
    tm = 512
    tmc = ctx_len

    pad = (-(b + 1)) % 8
    cond = jnp.concatenate([c, c_ctx[None, :], jnp.zeros((pad, d), F32)], axis=0)
    mods = _ada_call(cond, ada_w, ada_b)
    mod_lat = mods[:, :b].reshape(depth, b, 6, d)
    mod_ctx = mods[:, b:b + 1].reshape(depth, 1, 6, d)

    n1 = norm1_g.reshape(depth, 1, d)
    n2 = norm2_g.reshape(depth, 1, d)
    w1 = mlp_w1.astype(BF16)
    w2 = mlp_w2.astype(BF16)

    w_in = conv_in_w[0].astype(BF16)
    w_co = conv_out_w[0].astype(BF16)
    y = _conv_pre_call(x, mod_lat[0], n1[0], w_in, conv_w[0], tm)
    yc = _conv_pre_call(ctx, mod_ctx[0], n1[0], w_in, conv_w[0], tmc)
    x = _post_call(x, y, mod_lat[0], n2[0], w_co, w1[0], w2[0], None, tm)
    ctx = _post_call(ctx, yc, mod_ctx[0], n2[0], w_co, w1[0], w2[0], None, tmc)

    w_qkv = attn_qkv_w[0].astype(BF16)
    w_ao = attn_out_w[0].astype(BF16)
    qkv = _proj_pre_call(x, mod_lat[1], n1[1], w_qkv, tm, d, head_dim ** -0.5 * LOG2E)
    ckv = _proj_pre_call(ctx, mod_ctx[1], n1[1], w_qkv[:, d:], tmc, 0, 1.0)
    o = _attn_call(qkv, ckv, _rpb_lane_rows(attn_rpb[0]))
    return _post_call(x, o, mod_lat[1], n2[1], w_ao, w1[1], w2[1], final_g.reshape(1, d), tm)
```

```python
import functools

import jax
import jax.numpy as jnp
from jax import lax
from jax.experimental import pallas as pl
from jax.experimental.pallas import tpu as pltpu

GRID_W = 64
N_HEADS = 16
WIN_ROWS = 8
WIN_COLS = 16
RMS_EPS = 1e-6
NEG_INF = -1e30
LOG2E = 1.4426950408889634

HALO = 8
HEADS_PER_STEP = 2
Q_ROWS, Q_COLS = 8, 16
K_ROWS, K_COLS = 16, 32
ROWS_PER_TILE = 128 // K_COLS
BLOCK_ROWS = 4
ATTN_BATCH = 2
VMEM_LIMIT = 56 * 1024 * 1024

BF16 = jnp.bfloat16
F32 = jnp.float32


def _norm_mod(v, g, shift, scale):
    inv = lax.rsqrt(jnp.mean(v * v, axis=-1, keepdims=True) + RMS_EPS)
    return ((v * inv) * g) * (1.0 + scale) + shift


def _ada_kernel(cond_ref, w_ref, b_ref, o_ref):
    cnd = cond_ref[...]
    s = (cnd / (1.0 + jnp.exp(-cnd))).astype(BF16)
    o_ref[...] = jnp.dot(s, w_ref[...].astype(BF16), preferred_element_type=F32) + b_ref[...]


def _ada_call(cond, ada_w, ada_b):
    depth, d, n = ada_w.shape
    rows = cond.shape[0]
    bn = d
    return pl.pallas_call(
        _ada_kernel,
        grid=(depth, n // bn),
        in_specs=[
            pl.BlockSpec((rows, d), lambda i, j: (0, 0)),
            pl.BlockSpec((None, d, bn), lambda i, j: (i, 0, j)),
            pl.BlockSpec((None, 1, bn), lambda i, j: (i, 0, j)),
        ],
        out_specs=pl.BlockSpec((None, rows, bn), lambda i, j: (i, 0, j)),
        out_shape=jax.ShapeDtypeStruct((depth, rows, n), F32),
        compiler_params=pltpu.CompilerParams(
            dimension_semantics=("arbitrary", "arbitrary"), vmem_limit_bytes=VMEM_LIMIT),
    )(cond, ada_w, ada_b.reshape(depth, 1, n))


def _mod_spec(mod):
    if mod.shape[0] == 1:
        return pl.BlockSpec((None, 6, mod.shape[2]), lambda b, t: (0, 0, 0))
    return pl.BlockSpec((None, 6, mod.shape[2]), lambda b, t: (b, 0, 0))


def _const_spec(shape):
    nd = len(shape)
    return pl.BlockSpec(shape, lambda b, t: (0,) * nd, pipeline_mode=pl.Buffered(1))


def _conv_pre_kernel(x_ref, xp_ref, xn_ref, mod_ref, g_ref, w_ref, cw_ref, o_ref, *, tm, nt, d):
    t = pl.program_id(1)
    shift = mod_ref[0:1, :]
    scale = mod_ref[1:2, :]
    xe = jnp.concatenate([xp_ref[...], x_ref[...], xn_ref[...]], axis=0)
    he = _norm_mod(xe, g_ref[...], shift, scale).astype(BF16)
    cv = jnp.dot(he, w_ref[:, d:], preferred_element_type=F32)
    u = cv[:, :d] * cv[:, d:]
    row = lax.broadcasted_iota(jnp.int32, (tm + 2 * HALO, 1), 0)
    inside = ((row >= HALO) | (t > 0)) & ((row < tm + HALO) | (t < nt - 1))
    u = jnp.where(inside, u, 0.0)
    n_ext = tm + 2 * HALO
    u_prev = pltpu.roll(u, 1, axis=0)[HALO:HALO + tm]
    u_next = pltpu.roll(u, n_ext - 1, axis=0)[HALO:HALO + tm]
    conv = u_prev * cw_ref[0:1, :] + u[HALO:HALO + tm] * cw_ref[1:2, :] + u_next * cw_ref[2:3, :]
    bg = jnp.dot(he[HALO:HALO + tm], w_ref[:, :d], preferred_element_type=F32)
    o_ref[...] = (bg * conv).astype(BF16)


def _conv_pre_call(x, mod, g, w_in, conv_w, tm):
    b, l, d = x.shape
    nt = l // tm
    hb = tm // HALO
    last_hb = l // HALO - 1
    kern = functools.partial(_conv_pre_kernel, tm=tm, nt=nt, d=d)
    return pl.pallas_call(
        kern,
        grid=(b, nt),
        in_specs=[
            pl.BlockSpec((None, tm, d), lambda i, t: (i, t, 0)),
            pl.BlockSpec((None, HALO, d), lambda i, t: (i, jnp.maximum(t * hb - 1, 0), 0)),
            pl.BlockSpec((None, HALO, d), lambda i, t: (i, jnp.minimum((t + 1) * hb, last_hb), 0)),
            _mod_spec(mod),
            _const_spec((1, d)),
            _const_spec((d, 3 * d)),
            _const_spec((3, d)),
        ],
        out_specs=pl.BlockSpec((None, tm, d), lambda i, t: (i, t, 0)),
        out_shape=jax.ShapeDtypeStruct((b, l, d), BF16),
        compiler_params=pltpu.CompilerParams(
            dimension_semantics=("arbitrary", "arbitrary"), vmem_limit_bytes=VMEM_LIMIT),
    )(x, x, x, mod, g, w_in, conv_w)


def _proj_pre_kernel(x_ref, mod_ref, g_ref, w_ref, o_ref, *, n_scaled, q_scale):
    h = _norm_mod(x_ref[...], g_ref[...], mod_ref[0:1, :], mod_ref[1:2, :]).astype(BF16)
    if n_scaled:
        o_ref[:, :n_scaled] = (
            jnp.dot(h, w_ref[:, :n_scaled], preferred_element_type=F32) * q_scale).astype(BF16)
    o_ref[:, n_scaled:] = jnp.dot(h, w_ref[:, n_scaled:], preferred_element_type=F32).astype(BF16)


def _proj_pre_call(x, mod, g, w, tm, n_scaled, q_scale):
    b, l, d = x.shape
    n = w.shape[1]
    kern = functools.partial(_proj_pre_kernel, n_scaled=n_scaled, q_scale=q_scale)
    return pl.pallas_call(
        kern,
        grid=(b, l // tm),
        in_specs=[
            pl.BlockSpec((None, tm, d), lambda i, t: (i, t, 0)),
            _mod_spec(mod),
            _const_spec((1, d)),
            _const_spec((d, n)),
        ],
        out_specs=pl.BlockSpec((None, tm, n), lambda i, t: (i, t, 0)),
        out_shape=jax.ShapeDtypeStruct((b, l, n), BF16),
        compiler_params=pltpu.CompilerParams(
            dimension_semantics=("arbitrary", "arbitrary"), vmem_limit_bytes=VMEM_LIMIT),
    )(x, mod, g, w)


def _post_kernel(x_ref, y_ref, mod_ref, g_ref, wo_ref, w1_ref, w2_ref, *rest, ff_chunk, final):
    if final:
        fg_ref, o_ref = rest
    else:
        (o_ref,) = rest
    gate1 = mod_ref[2:3, :]
    shift2 = mod_ref[3:4, :]
    scale2 = mod_ref[4:5, :]
    gate2 = mod_ref[5:6, :]
    x1 = x_ref[...] + gate1 * jnp.dot(y_ref[...], wo_ref[...], preferred_element_type=F32)
    h = _norm_mod(x1, g_ref[...], shift2, scale2).astype(BF16)
    ff = w1_ref.shape[1]
    acc = None
    for f in range(0, ff, ff_chunk):
        a = jnp.dot(h, w1_ref[:, f:f + ff_chunk], preferred_element_type=F32)
        a = jnp.maximum(a, 0.0)
        a = (a * a).astype(BF16)
        part = jnp.dot(a, w2_ref[f:f + ff_chunk, :], preferred_element_type=F32)
        acc = part if acc is None else acc + part
    x2 = x1 + gate2 * acc
    if final:
        inv = lax.rsqrt(jnp.mean(x2 * x2, axis=-1, keepdims=True) + RMS_EPS)
        x2 = (x2 * inv) * fg_ref[...]
    o_ref[...] = x2


def _post_call(x, y, mod, g, w_out, w1, w2, final_g, tm, ff_chunk=1024):
    b, l, d = x.shape
    ff = w1.shape[1]
    final = final_g is not None
    kern = functools.partial(_post_kernel, ff_chunk=ff_chunk, final=final)
    in_specs = [
        pl.BlockSpec((None, tm, d), lambda i, t: (i, t, 0)),
        pl.BlockSpec((None, tm, d), lambda i, t: (i, t, 0)),
        _mod_spec(mod),
        _const_spec((1, d)),
        _const_spec((d, d)),
        _const_spec((d, ff)),
        _const_spec((ff, d)),
    ]
    args = [x, y, mod, g, w_out, w1, w2]
    if final:
        in_specs.append(_const_spec((1, d)))
        args.append(final_g)
    return pl.pallas_call(
        kern,
        grid=(b, l // tm),
        in_specs=in_specs,
        out_specs=pl.BlockSpec((None, tm, d), lambda i, t: (i, t, 0)),
        out_shape=jax.ShapeDtypeStruct((b, l, d), F32),
        compiler_params=pltpu.CompilerParams(
            dimension_semantics=("arbitrary", "arbitrary"), vmem_limit_bytes=VMEM_LIMIT),
    )(*args)


def _key_row_start(g, rows):
    return min(max(Q_ROWS * g - WIN_ROWS // 2, 0), rows - K_ROWS)


def _key_col_start(cb):
    lo = min(max(Q_COLS * cb - WIN_COLS // 2, 0), GRID_W - WIN_COLS)
    hi = min(max(Q_COLS * (cb + 1) - 1 - WIN_COLS // 2, 0), GRID_W - WIN_COLS) + WIN_COLS
    start = min(lo // HALO * HALO, GRID_W - K_COLS)
    assert start <= lo and hi <= start + K_COLS
    return start


def _window_row(r, rows):
    return min(max(r - WIN_ROWS // 2, 0), rows - WIN_ROWS)


def _visible_tiles(g, blk, rows):
    rel = [_window_row(Q_ROWS * g + BLOCK_ROWS * blk + i, rows) - _key_row_start(g, rows)
           for i in range(BLOCK_ROWS)]
    return min(rel) // ROWS_PER_TILE, -(-(max(rel) + WIN_ROWS) // ROWS_PER_TILE)


def _rpb_lane_rows(rpb):
    nc = rpb.shape[-1]
    placed = [jnp.pad(rpb, ((0, 0), (0, 0), (K_COLS * j, 128 - K_COLS * j - nc))) for j in range(ROWS_PER_TILE)]
    return jnp.stack(placed, axis=2)


def _fill_bias(rv_ref, bias_ref, *, rows):
    n_groups = rows // Q_ROWS
    shape = (Q_COLS, 128)
    qcl = lax.broadcasted_iota(jnp.int32, shape, 0)
    lane = lax.broadcasted_iota(jnp.int32, shape, 1)
    t = lane & (K_COLS - 1)
    neg = jnp.full(shape, NEG_INF, F32)
    for cls, g in enumerate((0, 1, n_groups - 1)):
        kr0 = _key_row_start(g, rows)
        for cb in range(GRID_W // Q_COLS):
            kstart = _key_col_start(cb)
            cs = jnp.clip(qcl + (Q_COLS * cb - WIN_COLS // 2), 0, GRID_W - WIN_COLS) - kstart
            col_ok = (t >= cs) & (t < cs + WIN_COLS)
            delta = kstart - Q_COLS * cb + WIN_COLS - 1
            for hh in range(HEADS_PER_STEP):
                for qr in range(Q_ROWS):
                    r = Q_ROWS * g + qr
                    rs = _window_row(r, rows)
                    t_lo, t_hi = _visible_tiles(g, qr // BLOCK_ROWS, rows)
                    for tile in range(t_lo, t_hi):
                        krow = [kr0 + ROWS_PER_TILE * tile + j for j in range(ROWS_PER_TILE)]
                        seen = [j for j in range(ROWS_PER_TILE) if rs <= krow[j] < rs + WIN_ROWS]
                        dst = (cls, hh, cb, slice(qr * Q_COLS, (qr + 1) * Q_COLS),
                               slice(tile * 128, (tile + 1) * 128))
                        if not seen:
                            bias_ref[dst] = neg
                            continue
                        v = None
                        for j in range(ROWS_PER_TILE):
                            dr = min(max(krow[j] - r + WIN_ROWS - 1, 0), 2 * WIN_ROWS - 2)
                            piece = rv_ref[hh, dr, j:j + 1, :]
                            v = piece if v is None else v + piece
                        rolled = pltpu.roll(jnp.broadcast_to(v * LOG2E, shape), (-delta) % 128, axis=1,
                                            stride=1, stride_axis=0)
                        ok = col_ok & (lane >= K_COLS * seen[0]) & (lane < K_COLS * (seen[-1] + 1))
                        bias_ref[dst] = jnp.where(ok, rolled, neg)


def _attn_kernel(q_ref, k_ref, v_ref, kc_ref, vc_ref, rv_ref, o_ref, bias_ref, kf_ref, vf_ref, *,
                 rows, head_dim):
    n_batch = q_ref.shape[0]
    n_groups = rows // Q_ROWS
    n_cb = GRID_W // Q_COLS
    nq = Q_ROWS * Q_COLS
    nk = K_ROWS * K_COLS

    @pl.when(pl.program_id(1) == 0)
    def _():
        _fill_bias(rv_ref, bias_ref, rows=rows)

    lane = lax.broadcasted_iota(jnp.int32, (1, HEADS_PER_STEP * head_dim), 1)
    nt = (((1,), (1,)), ((), ()))

    converted = {}

    def window(src_ref, dst_ref, bi, g, cb):
        r0 = _key_row_start(g, rows)
        c0 = _key_col_start(cb)
        done = converted.get((id(dst_ref), bi), 0)
        if done < r0 + K_ROWS:
            tok = slice(done * GRID_W, (r0 + K_ROWS) * GRID_W)
            dst_ref[bi, tok, :] = src_ref[bi, tok, :].astype(F32)
            converted[(id(dst_ref), bi)] = r0 + K_ROWS
        pieces = [dst_ref[bi, (r0 + j) * GRID_W + c0:(r0 + j) * GRID_W + c0 + K_COLS, :]
                  for j in range(K_ROWS)]
        return jnp.concatenate(pieces, axis=0).astype(BF16)

    def scores(bi, g, cb):
        pieces = [q_ref[bi, (Q_ROWS * g + qr) * GRID_W + Q_COLS * cb:
                        (Q_ROWS * g + qr) * GRID_W + Q_COLS * (cb + 1), :] for qr in range(Q_ROWS)]
        q = jnp.concatenate(pieces, axis=0)
        zero = jnp.zeros_like(q)
        qh = jnp.concatenate(
            [jnp.where((lane >= hh * head_dim) & (lane < (hh + 1) * head_dim), q, zero)
             for hh in range(HEADS_PER_STEP)], axis=0)
        s_loc = lax.dot_general(qh, window(k_ref, kf_ref, bi, g, cb), nt, preferred_element_type=F32)
        s_ctx = lax.dot_general(qh, kc_ref[bi], nt, preferred_element_type=F32)
        return s_loc, s_ctx

    def attend(bi, g, cb, s_loc, s_ctx):
        cls = 0 if g == 0 else (2 if g == n_groups - 1 else 1)
        p_blocks, d_blocks = [], []
        for hh in range(HEADS_PER_STEP):
            for blk in range(Q_ROWS // BLOCK_ROWS):
                t_lo, t_hi = _visible_tiles(g, blk, rows)
                nb = BLOCK_ROWS * Q_COLS
                qsl = slice(blk * nb, (blk + 1) * nb)
                rsl = slice(hh * nq + qsl.start, hh * nq + qsl.stop)
                tiles = [s_loc[rsl, t * 128:(t + 1) * 128] + bias_ref[cls, hh, cb, qsl, t * 128:(t + 1) * 128]
                         for t in range(t_lo, t_hi)]
                tiles += [s_ctx[rsl, t * 128:(t + 1) * 128] for t in range(s_ctx.shape[1] // 128)]
                m = jnp.max(functools.reduce(jnp.maximum, tiles), axis=-1, keepdims=True)
                probs = [jnp.exp2(t - m) for t in tiles]
                d_blocks.append(jnp.sum(functools.reduce(jnp.add, probs), axis=-1, keepdims=True))
                n_loc = t_hi - t_lo
                pieces = [jnp.zeros((nb, 128), BF16)] * t_lo
                pieces += [t.astype(BF16) for t in probs[:n_loc]]
                pieces += [jnp.zeros((nb, 128), BF16)] * (nk // 128 - t_hi)
                pieces += [t.astype(BF16) for t in probs[n_loc:]]
                p_blocks.append(jnp.concatenate(pieces, axis=1))
        p = jnp.concatenate(p_blocks, axis=0)
        denom = jnp.concatenate(d_blocks, axis=0)
        o = (jnp.dot(p[:, :nk], window(v_ref, vf_ref, bi, g, cb), preferred_element_type=F32)
             + jnp.dot(p[:, nk:], vc_ref[bi], preferred_element_type=F32)) / denom
        out = o[(HEADS_PER_STEP - 1) * nq:]
        for hh in range(HEADS_PER_STEP - 2, -1, -1):
            out = jnp.where(lane < (hh + 1) * head_dim, o[hh * nq:(hh + 1) * nq], out)
        out = out.astype(BF16)
        for qr in range(Q_ROWS):
            tok = (Q_ROWS * g + qr) * GRID_W + Q_COLS * cb
            o_ref[bi, tok:tok + Q_COLS, :] = out[qr * Q_COLS:(qr + 1) * Q_COLS]

    units = [(bi, g, cb) for bi in range(n_batch) for g in range(n_groups) for cb in range(n_cb)]
    nxt = scores(*units[0])
    for u, unit in enumerate(units):
        cur = nxt
        if u + 1 < len(units):
            nxt = scores(*units[u + 1])
        attend(*unit, *cur)


def _attn_call(qkv, ckv, rpb_rows):
    b, s, d3 = qkv.shape
    d = d3 // 3
    ctx_len = ckv.shape[1]
    head_dim = d // N_HEADS
    w = HEADS_PER_STEP * head_dim
    assert w == 128
    n_hp = d // w
    rows = s // GRID_W
    n_dr = rpb_rows.shape[1]
    bb = ATTN_BATCH if b % ATTN_BATCH == 0 else 1
    kern = functools.partial(_attn_kernel, rows=rows, head_dim=head_dim)
    return pl.pallas_call(
        kern,
        grid=(n_hp, b // bb),
        in_specs=[
            pl.BlockSpec((bb, s, w), lambda h, i: (i, 0, h)),
            pl.BlockSpec((bb, s, w), lambda h, i: (i, 0, n_hp + h)),
            pl.BlockSpec((bb, s, w), lambda h, i: (i, 0, 2 * n_hp + h)),
            pl.BlockSpec((bb, ctx_len, w), lambda h, i: (i, 0, h)),
            pl.BlockSpec((bb, ctx_len, w), lambda h, i: (i, 0, n_hp + h)),
            pl.BlockSpec((HEADS_PER_STEP, n_dr, ROWS_PER_TILE, 128), lambda h, i: (h, 0, 0, 0)),
        ],
        out_specs=pl.BlockSpec((bb, s, w), lambda h, i: (i, 0, h)),
        out_shape=jax.ShapeDtypeStruct((b, s, d), BF16),
        scratch_shapes=[
            pltpu.VMEM((3, HEADS_PER_STEP, GRID_W // Q_COLS, Q_ROWS * Q_COLS, K_ROWS * K_COLS), F32),
            pltpu.VMEM((bb, s, w), F32),
            pltpu.VMEM((bb, s, w), F32),
        ],
        compiler_params=pltpu.CompilerParams(
            dimension_semantics=("arbitrary", "arbitrary"), vmem_limit_bytes=VMEM_LIMIT),
    )(qkv, qkv, qkv, ckv, ckv, rpb_rows)


def kernel(x, c, ctx, c_ctx, norm1_g, norm2_g, ada_w, ada_b, conv_in_w, conv_w, conv_out_w,
           attn_qkv_w, attn_rpb, attn_out_w, mlp_w1, mlp_w2, final_g):
    b, s, d = x.shape
    ctx_len = ctx.shape[1]
    depth = ada_w.shape[0]
    assert depth == 2 and s % (GRID_W * Q_ROWS) == 0 and s // GRID_W >= K_ROWS and d % N_HEADS == 0
    head_dim = d // N_HEADS
    tm = 512
    tmc = ctx_len

    pad = (-(b + 1)) % 8
    cond = jnp.concatenate([c, c_ctx[None, :], jnp.zeros((pad, d), F32)], axis=0)
    mods = _ada_call(cond, ada_w, ada_b)
    mod_lat = mods[:, :b].reshape(depth, b, 6, d)
    mod_ctx = mods[:, b:b + 1].reshape(depth, 1, 6, d)

    n1 = norm1_g.reshape(depth, 1, d)
    n2 = norm2_g.reshape(depth, 1, d)
    w1 = mlp_w1.astype(BF16)
    w2 = mlp_w2.astype(BF16)

    w_in = conv_in_w[0].astype(BF16)
    w_co = conv_out_w[0].astype(BF16)
    y = _conv_pre_call(x, mod_lat[0], n1[0], w_in, conv_w[0], tm)
    yc = _conv_pre_call(ctx, mod_ctx[0], n1[0], w_in, conv_w[0], tmc)
    x = _post_call(x, y, mod_lat[0], n2[0], w_co, w1[0], w2[0], None, tm)
    ctx = _post_call(ctx, yc, mod_ctx[0], n2[0], w_co, w1[0], w2[0], None, tmc)

    w_qkv = attn_qkv_w[0].astype(BF16)
    w_ao = attn_out_w[0].astype(BF16)
    qkv = _proj_pre_call(x, mod_lat[1], n1[1], w_qkv, tm, d, head_dim ** -0.5 * LOG2E)
    ckv = _proj_pre_call(ctx, mod_ctx[1], n1[1], w_qkv[:, d:], tmc, 0, 1.0)
    o = _attn_call(qkv, ckv, _rpb_lane_rows(attn_rpb[0]))
    return _post_call(x, o, mod_lat[1], n2[1], w_ao, w1[1], w2[1], final_g.reshape(1, d), tm)
```

```python
import functools

import jax
import jax.numpy as jnp
from jax import lax
from jax.experimental import pallas as pl
from jax.experimental.pallas import tpu as pltpu

GRID_W = 64
N_HEADS = 16
WIN_ROWS = 8
WIN_COLS = 16
RMS_EPS = 1e-6
NEG_INF = -1e30
LOG2E = 1.4426950408889634

HALO = 8
HEADS_PER_STEP = 2
Q_ROWS, Q_COLS = 8, 16
K_ROWS, K_COLS = 16, 32
ROWS_PER_TILE = 128 // K_COLS
BLOCK_ROWS = 4
ATTN_BATCH = 2
VMEM_LIMIT = 56 * 1024 * 1024

BF16 = jnp.bfloat16
F32 = jnp.float32


def _norm_mod(v, g, shift, scale):
    inv = lax.rsqrt(jnp.mean(v * v, axis=-1, keepdims=True) + RMS_EPS)
    return ((v * inv) * g) * (1.0 + scale) + shift


def _ada_kernel(cond_ref, w_ref, b_ref, o_ref):
    cnd = cond_ref[...]
    s = (cnd / (1.0 + jnp.exp(-cnd))).astype(BF16)
    o_ref[...] = jnp.dot(s, w_ref[...].astype(BF16), preferred_element_type=F32) + b_ref[...]


def _ada_call(cond, ada_w, ada_b):
    depth, d, n = ada_w.shape
    rows = cond.shape[0]
    bn = d
    return pl.pallas_call(
        _ada_kernel,
        grid=(depth, n // bn),
        in_specs=[
            pl.BlockSpec((rows, d), lambda i, j: (0, 0)),
            pl.BlockSpec((None, d, bn), lambda i, j: (i, 0, j)),
            pl.BlockSpec((None, 1, bn), lambda i, j: (i, 0, j)),
        ],
        out_specs=pl.BlockSpec((None, rows, bn), lambda i, j: (i, 0, j)),
        out_shape=jax.ShapeDtypeStruct((depth, rows, n), F32),
        compiler_params=pltpu.CompilerParams(
            dimension_semantics=("arbitrary", "arbitrary"), vmem_limit_bytes=VMEM_LIMIT),
    )(cond, ada_w, ada_b.reshape(depth, 1, n))


def _mod_spec(mod):
    if mod.shape[0] == 1:
        return pl.BlockSpec((None, 6, mod.shape[2]), lambda b, t: (0, 0, 0))
    return pl.BlockSpec((None, 6, mod.shape[2]), lambda b, t: (b, 0, 0))


def _const_spec(shape):
    nd = len(shape)
    return pl.BlockSpec(shape, lambda b, t: (0,) * nd, pipeline_mode=pl.Buffered(1))


def _conv_pre_kernel(x_ref, xp_ref, xn_ref, mod_ref, g_ref, w_ref, cw_ref, o_ref, *, tm, nt, d, n_sub):
    t = pl.program_id(1)
    shift = mod_ref[0:1, :]
    scale = mod_ref[1:2, :]
    sub = tm // n_sub
    n_ext = sub + 2 * HALO
    xe = jnp.concatenate([xp_ref[...], x_ref[...], xn_ref[...]], axis=0)
    row = lax.broadcasted_iota(jnp.int32, (n_ext, 1), 0)

    def project(i):
        he = _norm_mod(xe[i * sub:i * sub + n_ext], g_ref[...], shift, scale).astype(BF16)
        return he, jnp.dot(he, w_ref[:, d:], preferred_element_type=F32)

    def gate_conv(i, he, cv):
        u = cv[:, :d] * cv[:, d:]
        if i == 0:
            u = jnp.where((row >= HALO) | (t > 0), u, 0.0)
        if i == n_sub - 1:
            u = jnp.where((row < sub + HALO) | (t < nt - 1), u, 0.0)
        u_prev = pltpu.roll(u, 1, axis=0)[HALO:HALO + sub]
        u_next = pltpu.roll(u, n_ext - 1, axis=0)[HALO:HALO + sub]
        conv = u_prev * cw_ref[0:1, :] + u[HALO:HALO + sub] * cw_ref[1:2, :] + u_next * cw_ref[2:3, :]
        bg = jnp.dot(he[HALO:HALO + sub], w_ref[:, :d], preferred_element_type=F32)
        o_ref[i * sub:(i + 1) * sub, :] = (bg * conv).astype(BF16)

    projected = [project(i) for i in range(n_sub)]
    for i, (he, cv) in enumerate(projected):
        gate_conv(i, he, cv)


def _conv_pre_call(x, mod, g, w_in, conv_w, tm, n_sub):
    b, l, d = x.shape
    nt = l // tm
    hb = tm // HALO
    last_hb = l // HALO - 1
    kern = functools.partial(_conv_pre_kernel, tm=tm, nt=nt, d=d, n_sub=n_sub)
    return pl.pallas_call(
        kern,
        grid=(b, nt),
        in_specs=[
            pl.BlockSpec((None, tm, d), lambda i, t: (i, t, 0)),
            pl.BlockSpec((None, HALO, d), lambda i, t: (i, jnp.maximum(t * hb - 1, 0), 0)),
            pl.BlockSpec((None, HALO, d), lambda i, t: (i, jnp.minimum((t + 1) * hb, last_hb), 0)),
            _mod_spec(mod),
            _const_spec((1, d)),
            _const_spec((d, 3 * d)),
            _const_spec((3, d)),
        ],
        out_specs=pl.BlockSpec((None, tm, d), lambda i, t: (i, t, 0)),
        out_shape=jax.ShapeDtypeStruct((b, l, d), BF16),
        compiler_params=pltpu.CompilerParams(
            dimension_semantics=("arbitrary", "arbitrary"), vmem_limit_bytes=VMEM_LIMIT),
    )(x, x, x, mod, g, w_in, conv_w)


def _proj_pre_kernel(x_ref, mod_ref, g_ref, w_ref, o_ref, *, n_scaled, q_scale, n_sub):
    sub = x_ref.shape[0] // n_sub
    for i in range(n_sub):
        rs = slice(i * sub, (i + 1) * sub)
        h = _norm_mod(x_ref[rs, :], g_ref[...], mod_ref[0:1, :], mod_ref[1:2, :]).astype(BF16)
        if n_scaled:
            o_ref[rs, :n_scaled] = (
                jnp.dot(h, w_ref[:, :n_scaled], preferred_element_type=F32) * q_scale).astype(BF16)
        o_ref[rs, n_scaled:] = jnp.dot(h, w_ref[:, n_scaled:], preferred_element_type=F32).astype(BF16)


def _proj_pre_call(x, mod, g, w, tm, n_scaled, q_scale, n_sub):
    b, l, d = x.shape
    n = w.shape[1]
    kern = functools.partial(_proj_pre_kernel, n_scaled=n_scaled, q_scale=q_scale, n_sub=n_sub)
    return pl.pallas_call(
        kern,
        grid=(b, l // tm),
        in_specs=[
            pl.BlockSpec((None, tm, d), lambda i, t: (i, t, 0)),
            _mod_spec(mod),
            _const_spec((1, d)),
            _const_spec((d, n)),
        ],
        out_specs=pl.BlockSpec((None, tm, n), lambda i, t: (i, t, 0)),
        out_shape=jax.ShapeDtypeStruct((b, l, n), BF16),
        compiler_params=pltpu.CompilerParams(
            dimension_semantics=("arbitrary", "arbitrary"), vmem_limit_bytes=VMEM_LIMIT),
    )(x, mod, g, w)


def _post_kernel(x_ref, y_ref, mod_ref, g_ref, wo_ref, w1_ref, w2_ref, *rest, ff_chunk, final, n_sub):
    if final:
        fg_ref, o_ref = rest
    else:
        (o_ref,) = rest
    gate1 = mod_ref[2:3, :]
    shift2 = mod_ref[3:4, :]
    scale2 = mod_ref[4:5, :]
    gate2 = mod_ref[5:6, :]
    ff = w1_ref.shape[1]
    sub = x_ref.shape[0] // n_sub

    def out_proj(i):
        rs = slice(i * sub, (i + 1) * sub)
        return x_ref[rs, :] + gate1 * jnp.dot(y_ref[rs, :], wo_ref[...], preferred_element_type=F32)

    def mlp(x1):
        h = _norm_mod(x1, g_ref[...], shift2, scale2).astype(BF16)
        acc = None
        for f in range(0, ff, ff_chunk):
            a = jnp.dot(h, w1_ref[:, f:f + ff_chunk], preferred_element_type=F32)
            a = jnp.maximum(a, 0.0)
            a = (a * a).astype(BF16)
            part = jnp.dot(a, w2_ref[f:f + ff_chunk, :], preferred_element_type=F32)
            acc = part if acc is None else acc + part
        return x1 + gate2 * acc

    def finish(i, x2):
        if final:
            inv = lax.rsqrt(jnp.mean(x2 * x2, axis=-1, keepdims=True) + RMS_EPS)
            x2 = (x2 * inv) * fg_ref[...]
        o_ref[i * sub:(i + 1) * sub, :] = x2

    x1s = [out_proj(i) for i in range(n_sub)]
    x2s = [mlp(x1) for x1 in x1s]
    for i, x2 in enumerate(x2s):
        finish(i, x2)


def _post_call(x, y, mod, g, w_out, w1, w2, final_g, tm, ff_chunk=1024, n_sub=2):
    b, l, d = x.shape
    ff = w1.shape[1]
    final = final_g is not None
    kern = functools.partial(_post_kernel, ff_chunk=ff_chunk, final=final, n_sub=n_sub)
    in_specs = [
        pl.BlockSpec((None, tm, d), lambda i, t: (i, t, 0)),
        pl.BlockSpec((None, tm, d), lambda i, t: (i, t, 0)),
        _mod_spec(mod),
        _const_spec((1, d)),
        _const_spec((d, d)),
        _const_spec((d, ff)),
        _const_spec((ff, d)),
    ]
    args = [x, y, mod, g, w_out, w1, w2]
    if final:
        in_specs.append(_const_spec((1, d)))
        args.append(final_g)
    return pl.pallas_call(
        kern,
        grid=(b, l // tm),
        in_specs=in_specs,
        out_specs=pl.BlockSpec((None, tm, d), lambda i, t: (i, t, 0)),
        out_shape=jax.ShapeDtypeStruct((b, l, d), F32),
        compiler_params=pltpu.CompilerParams(
            dimension_semantics=("arbitrary", "arbitrary"), vmem_limit_bytes=VMEM_LIMIT),
    )(*args)


def _key_row_start(g, rows):
    return min(max(Q_ROWS * g - WIN_ROWS // 2, 0), rows - K_ROWS)


def _key_col_start(cb):
    lo = min(max(Q_COLS * cb - WIN_COLS // 2, 0), GRID_W - WIN_COLS)
    hi = min(max(Q_COLS * (cb + 1) - 1 - WIN_COLS // 2, 0), GRID_W - WIN_COLS) + WIN_COLS
    start = min(lo // HALO * HALO, GRID_W - K_COLS)
    assert start <= lo and hi <= start + K_COLS
    return start


def _window_row(r, rows):
    return min(max(r - WIN_ROWS // 2, 0), rows - WIN_ROWS)


def _visible_tiles(g, blk, rows):
    rel = [_window_row(Q_ROWS * g + BLOCK_ROWS * blk + i, rows) - _key_row_start(g, rows)
           for i in range(BLOCK_ROWS)]
    return min(rel) // ROWS_PER_TILE, -(-(max(rel) + WIN_ROWS) // ROWS_PER_TILE)


def _rpb_lane_rows(rpb):
    nc = rpb.shape[-1]
    placed = [jnp.pad(rpb, ((0, 0), (0, 0), (K_COLS * j, 128 - K_COLS * j - nc))) for j in range(ROWS_PER_TILE)]
    return jnp.stack(placed, axis=2)


def _fill_bias(rv_ref, bias_ref, *, rows):
    n_groups = rows // Q_ROWS
    shape = (Q_COLS, 128)
    qcl = lax.broadcasted_iota(jnp.int32, shape, 0)
    lane = lax.broadcasted_iota(jnp.int32, shape, 1)
    t = lane & (K_COLS - 1)
    neg = jnp.full(shape, NEG_INF, F32)
    for cls, g in enumerate((0, 1, n_groups - 1)):
        kr0 = _key_row_start(g, rows)
        for cb in range(GRID_W // Q_COLS):
            kstart = _key_col_start(cb)
            cs = jnp.clip(qcl + (Q_COLS * cb - WIN_COLS // 2), 0, GRID_W - WIN_COLS) - kstart
            col_ok = (t >= cs) & (t < cs + WIN_COLS)
            delta = kstart - Q_COLS * cb + WIN_COLS - 1
            for hh in range(HEADS_PER_STEP):
                for qr in range(Q_ROWS):
                    r = Q_ROWS * g + qr
                    rs = _window_row(r, rows)
                    t_lo, t_hi = _visible_tiles(g, qr // BLOCK_ROWS, rows)
                    for tile in range(t_lo, t_hi):
                        krow = [kr0 + ROWS_PER_TILE * tile + j for j in range(ROWS_PER_TILE)]
                        seen = [j for j in range(ROWS_PER_TILE) if rs <= krow[j] < rs + WIN_ROWS]
                        dst = (cls, hh, cb, slice(qr * Q_COLS, (qr + 1) * Q_COLS),
                               slice(tile * 128, (tile + 1) * 128))
                        if not seen:
                            bias_ref[dst] = neg
                            continue
                        v = None
                        for j in range(ROWS_PER_TILE):
                            dr = min(max(krow[j] - r + WIN_ROWS - 1, 0), 2 * WIN_ROWS - 2)
                            piece = rv_ref[hh, dr, j:j + 1, :]
                            v = piece if v is None else v + piece
                        rolled = pltpu.roll(jnp.broadcast_to(v * LOG2E, shape), (-delta) % 128, axis=1,
                                            stride=1, stride_axis=0)
                        ok = col_ok & (lane >= K_COLS * seen[0]) & (lane < K_COLS * (seen[-1] + 1))
                        bias_ref[dst] = jnp.where(ok, rolled, neg)


def _attn_kernel(q_ref, k_ref, v_ref, kc_ref, vc_ref, rv_ref, o_ref, bias_ref, kf_ref, vf_ref, *,
                 rows, head_dim):
    n_batch = q_ref.shape[0]
    n_groups = rows // Q_ROWS
    n_cb = GRID_W // Q_COLS
    nq = Q_ROWS * Q_COLS
    nk = K_ROWS * K_COLS

    @pl.when(pl.program_id(1) == 0)
    def _():
        _fill_bias(rv_ref, bias_ref, rows=rows)

    lane = lax.broadcasted_iota(jnp.int32, (1, HEADS_PER_STEP * head_dim), 1)
    nt = (((1,), (1,)), ((), ()))

    converted = {}

    def window(src_ref, dst_ref, bi, g, cb):
        r0 = _key_row_start(g, rows)
        c0 = _key_col_start(cb)
        done = converted.get((id(dst_ref), bi), 0)
        if done < r0 + K_ROWS:
            tok = slice(done * GRID_W, (r0 + K_ROWS) * GRID_W)
            dst_ref[bi, tok, :] = src_ref[bi, tok, :].astype(F32)
            converted[(id(dst_ref), bi)] = r0 + K_ROWS
        pieces = [dst_ref[bi, (r0 + j) * GRID_W + c0:(r0 + j) * GRID_W + c0 + K_COLS, :]
                  for j in range(K_ROWS)]
        return jnp.concatenate(pieces, axis=0).astype(BF16)

    def scores(bi, g, cb):
        pieces = [q_ref[bi, (Q_ROWS * g + qr) * GRID_W + Q_COLS * cb:
                        (Q_ROWS * g + qr) * GRID_W + Q_COLS * (cb + 1), :] for qr in range(Q_ROWS)]
        q = jnp.concatenate(pieces, axis=0)
        zero = jnp.zeros_like(q)
        qh = jnp.concatenate(
            [jnp.where((lane >= hh * head_dim) & (lane < (hh + 1) * head_dim), q, zero)
             for hh in range(HEADS_PER_STEP)], axis=0)
        s_loc = lax.dot_general(qh, window(k_ref, kf_ref, bi, g, cb), nt, preferred_element_type=F32)
        s_ctx = lax.dot_general(qh, kc_ref[bi], nt, preferred_element_type=F32)
        return s_loc, s_ctx

    def attend(bi, g, cb, s_loc, s_ctx):
        cls = 0 if g == 0 else (2 if g == n_groups - 1 else 1)
        p_blocks, d_blocks = [], []
        for hh in range(HEADS_PER_STEP):
            for blk in range(Q_ROWS // BLOCK_ROWS):
                t_lo, t_hi = _visible_tiles(g, blk, rows)
                nb = BLOCK_ROWS * Q_COLS
                qsl = slice(blk * nb, (blk + 1) * nb)
                rsl = slice(hh * nq + qsl.start, hh * nq + qsl.stop)
                tiles = [s_loc[rsl, t * 128:(t + 1) * 128] + bias_ref[cls, hh, cb, qsl, t * 128:(t + 1) * 128]
                         for t in range(t_lo, t_hi)]
                tiles += [s_ctx[rsl, t * 128:(t + 1) * 128] for t in range(s_ctx.shape[1] // 128)]
                m = jnp.max(functools.reduce(jnp.maximum, tiles), axis=-1, keepdims=True)
                probs = [jnp.exp2(t - m) for t in tiles]
                d_blocks.append(jnp.sum(functools.reduce(jnp.add, probs), axis=-1, keepdims=True))
                n_loc = t_hi - t_lo
                pieces = [jnp.zeros((nb, 128), BF16)] * t_lo
                pieces += [t.astype(BF16) for t in probs[:n_loc]]
                pieces += [jnp.zeros((nb, 128), BF16)] * (nk // 128 - t_hi)
                pieces += [t.astype(BF16) for t in probs[n_loc:]]
                p_blocks.append(jnp.concatenate(pieces, axis=1))
        p = jnp.concatenate(p_blocks, axis=0)
        denom = jnp.concatenate(d_blocks, axis=0)
        o = (jnp.dot(p[:, :nk], window(v_ref, vf_ref, bi, g, cb), preferred_element_type=F32)
             + jnp.dot(p[:, nk:], vc_ref[bi], preferred_element_type=F32)) / denom
        out = o[(HEADS_PER_STEP - 1) * nq:]
        for hh in range(HEADS_PER_STEP - 2, -1, -1):
            out = jnp.where(lane < (hh + 1) * head_dim, o[hh * nq:(hh + 1) * nq], out)
        out = out.astype(BF16)
        for qr in range(Q_ROWS):
            tok = (Q_ROWS * g + qr) * GRID_W + Q_COLS * cb
            o_ref[bi, tok:tok + Q_COLS, :] = out[qr * Q_COLS:(qr + 1) * Q_COLS]

    units = [(bi, g, cb) for bi in range(n_batch) for g in range(n_groups) for cb in range(n_cb)]
    nxt = scores(*units[0])
    for u, unit in enumerate(units):
        cur = nxt
        if u + 1 < len(units):
            nxt = scores(*units[u + 1])
        attend(*unit, *cur)


def _attn_call(qkv, ckv, rpb_rows):
    b, s, d3 = qkv.shape
    d = d3 // 3
    ctx_len = ckv.shape[1]
    head_dim = d // N_HEADS
    w = HEADS_PER_STEP * head_dim
    assert w == 128
    n_hp = d // w
    rows = s // GRID_W
    n_dr = rpb_rows.shape[1]
    bb = ATTN_BATCH if b % ATTN_BATCH == 0 else 1
    kern = functools.partial(_attn_kernel, rows=rows, head_dim=head_dim)
    return pl.pallas_call(
        kern,
        grid=(n_hp, b // bb),
        in_specs=[
            pl.BlockSpec((bb, s, w), lambda h, i: (i, 0, h)),
            pl.BlockSpec((bb, s, w), lambda h, i: (i, 0, n_hp + h)),
            pl.BlockSpec((bb, s, w), lambda h, i: (i, 0, 2 * n_hp + h)),
            pl.BlockSpec((bb, ctx_len, w), lambda h, i: (i, 0, h)),
            pl.BlockSpec((bb, ctx_len, w), lambda h, i: (i, 0, n_hp + h)),
            pl.BlockSpec((HEADS_PER_STEP, n_dr, ROWS_PER_TILE, 128), lambda h, i: (h, 0, 0, 0)),
        ],
        out_specs=pl.BlockSpec((bb, s, w), lambda h, i: (i, 0, h)),
        out_shape=jax.ShapeDtypeStruct((b, s, d), BF16),
        scratch_shapes=[
            pltpu.VMEM((3, HEADS_PER_STEP, GRID_W // Q_COLS, Q_ROWS * Q_COLS, K_ROWS * K_COLS), F32),
            pltpu.VMEM((bb, s, w), F32),
            pltpu.VMEM((bb, s, w), F32),
        ],
        compiler_params=pltpu.CompilerParams(
            dimension_semantics=("arbitrary", "arbitrary"), vmem_limit_bytes=VMEM_LIMIT),
    )(qkv, qkv, qkv, ckv, ckv, rpb_rows)


def kernel(x, c, ctx, c_ctx, norm1_g, norm2_g, ada_w, ada_b, conv_in_w, conv_w, conv_out_w,
           attn_qkv_w, attn_rpb, attn_out_w, mlp_w1, mlp_w2, final_g):
    b, s, d = x.shape
    ctx_len = ctx.shape[1]
    depth = ada_w.shape[0]
    assert depth == 2 and s % (GRID_W * Q_ROWS) == 0 and s // GRID_W >= K_ROWS and d % N_HEADS == 0
    head_dim = d // N_HEADS
    tm_pre, tm_post, tmc = 1024, 512, ctx_len

    pad = (-(b + 1)) % 8
    cond = jnp.concatenate([c, c_ctx[None, :], jnp.zeros((pad, d), F32)], axis=0)
    mods = _ada_call(cond, ada_w, ada_b)
    mod_lat = mods[:, :b].reshape(depth, b, 6, d)
    mod_ctx = mods[:, b:b + 1].reshape(depth, 1, 6, d)

    n1 = norm1_g.reshape(depth, 1, d)
    n2 = norm2_g.reshape(depth, 1, d)
    w1 = mlp_w1.astype(BF16)
    w2 = mlp_w2.astype(BF16)

    w_in = conv_in_w[0].astype(BF16)
    w_co = conv_out_w[0].astype(BF16)
    y = _conv_pre_call(x, mod_lat[0], n1[0], w_in, conv_w[0], tm_pre, 2)
    yc = _conv_pre_call(ctx, mod_ctx[0], n1[0], w_in, conv_w[0], tmc, 1)
    x = _post_call(x, y, mod_lat[0], n2[0], w_co, w1[0], w2[0], None, tm_post)
    ctx = _post_call(ctx, yc, mod_ctx[0], n2[0], w_co, w1[0], w2[0], None, tmc, n_sub=1)

    w_qkv = attn_qkv_w[0].astype(BF16)
    w_ao = attn_out_w[0].astype(BF16)
    qkv = _proj_pre_call(x, mod_lat[1], n1[1], w_qkv, tm_pre, d, head_dim ** -0.5 * LOG2E, 2)
    ckv = _proj_pre_call(ctx, mod_ctx[1], n1[1], w_qkv[:, d:], tmc, 0, 1.0, 1)
    o = _attn_call(qkv, ckv, _rpb_lane_rows(attn_rpb[0]))
    return _post_call(x, o, mod_lat[1], n2[1], w_ao, w1[1], w2[1], final_g.reshape(1, d), tm_post)
```

```python
import functools

import jax
import jax.numpy as jnp
from jax import lax
from jax.experimental import pallas as pl
from jax.experimental.pallas import tpu as pltpu

GRID_W = 64
N_HEADS = 16
WIN_ROWS = 8
WIN_COLS = 16
RMS_EPS = 1e-6
NEG_INF = -1e30
LOG2E = 1.4426950408889634

HALO = 8
HEADS_PER_STEP = 2
Q_ROWS, Q_COLS = 8, 16
K_ROWS, K_COLS = 16, 32
ROWS_PER_TILE = 128 // K_COLS
BLOCK_ROWS = 4
ATTN_BATCH = 2
VMEM_LIMIT = 56 * 1024 * 1024

BF16 = jnp.bfloat16
F32 = jnp.float32


def _norm_mod(v, g, shift, scale):
    inv = lax.rsqrt(jnp.mean(v * v, axis=-1, keepdims=True) + RMS_EPS)
    return ((v * inv) * g) * (1.0 + scale) + shift


def _ada_kernel(cond_ref, w_ref, b_ref, o_ref):
    cnd = cond_ref[...]
    s = (cnd / (1.0 + jnp.exp(-cnd))).astype(BF16)
    o_ref[...] = jnp.dot(s, w_ref[...].astype(BF16), preferred_element_type=F32) + b_ref[...]


def _ada_call(cond, ada_w, ada_b):
    depth, d, n = ada_w.shape
    rows = cond.shape[0]
    bn = d
    return pl.pallas_call(
        _ada_kernel,
        grid=(depth, n // bn),
        in_specs=[
            pl.BlockSpec((rows, d), lambda i, j: (0, 0)),
            pl.BlockSpec((None, d, bn), lambda i, j: (i, 0, j)),
            pl.BlockSpec((None, 1, bn), lambda i, j: (i, 0, j)),
        ],
        out_specs=pl.BlockSpec((None, rows, bn), lambda i, j: (i, 0, j)),
        out_shape=jax.ShapeDtypeStruct((depth, rows, n), F32),
        compiler_params=pltpu.CompilerParams(
            dimension_semantics=("arbitrary", "arbitrary"), vmem_limit_bytes=VMEM_LIMIT),
    )(cond, ada_w, ada_b.reshape(depth, 1, n))


def _mod_spec(mod, layer):
    if mod.shape[1] == 1:
        return pl.BlockSpec((None, None, 6, mod.shape[3]), lambda b, t: (layer, 0, 0, 0))
    return pl.BlockSpec((None, None, 6, mod.shape[3]), lambda b, t: (layer, b, 0, 0))


def _layer_spec(arr, layer):
    nd = arr.ndim - 1
    return pl.BlockSpec((None,) + arr.shape[1:], lambda b, t: (layer,) + (0,) * nd,
                        pipeline_mode=pl.Buffered(1))


def _conv_pre_kernel(x_ref, xp_ref, xn_ref, mod_ref, g_ref, w_ref, cw_ref, o_ref, *, tm, nt, d, n_sub):
    t = pl.program_id(1)
    shift = mod_ref[0:1, :]
    scale = mod_ref[1:2, :]
    sub = tm // n_sub
    n_ext = sub + 2 * HALO
    xe = jnp.concatenate([xp_ref[...], x_ref[...], xn_ref[...]], axis=0)
    row = lax.broadcasted_iota(jnp.int32, (n_ext, 1), 0)

    def project(i):
        he = _norm_mod(xe[i * sub:i * sub + n_ext], g_ref[...], shift, scale).astype(BF16)
        return he, jnp.dot(he, w_ref[:, d:], preferred_element_type=F32)

    def gate_conv(i, he, cv):
        u = cv[:, :d] * cv[:, d:]
        if i == 0:
            u = jnp.where((row >= HALO) | (t > 0), u, 0.0)
        if i == n_sub - 1:
            u = jnp.where((row < sub + HALO) | (t < nt - 1), u, 0.0)
        u_prev = pltpu.roll(u, 1, axis=0)[HALO:HALO + sub]
        u_next = pltpu.roll(u, n_ext - 1, axis=0)[HALO:HALO + sub]
        conv = u_prev * cw_ref[0:1, :] + u[HALO:HALO + sub] * cw_ref[1:2, :] + u_next * cw_ref[2:3, :]
        bg = jnp.dot(he[HALO:HALO + sub], w_ref[:, :d], preferred_element_type=F32)
        o_ref[i * sub:(i + 1) * sub, :] = (bg * conv).astype(BF16)

    projected = [project(i) for i in range(n_sub)]
    for i, (he, cv) in enumerate(projected):
        gate_conv(i, he, cv)


def _conv_pre_call(x, mod, g, w_in, conv_w, layer, mixer, tm, n_sub):
    b, l, d = x.shape
    nt = l // tm
    hb = tm // HALO
    last_hb = l // HALO - 1
    kern = functools.partial(_conv_pre_kernel, tm=tm, nt=nt, d=d, n_sub=n_sub)
    return pl.pallas_call(
        kern,
        grid=(b, nt),
        in_specs=[
            pl.BlockSpec((None, tm, d), lambda i, t: (i, t, 0)),
            pl.BlockSpec((None, HALO, d), lambda i, t: (i, jnp.maximum(t * hb - 1, 0), 0)),
            pl.BlockSpec((None, HALO, d), lambda i, t: (i, jnp.minimum((t + 1) * hb, last_hb), 0)),
            _mod_spec(mod, layer),
            _layer_spec(g, layer),
            _layer_spec(w_in, mixer),
            _layer_spec(conv_w, mixer),
        ],
        out_specs=pl.BlockSpec((None, tm, d), lambda i, t: (i, t, 0)),
        out_shape=jax.ShapeDtypeStruct((b, l, d), BF16),
        compiler_params=pltpu.CompilerParams(
            dimension_semantics=("arbitrary", "arbitrary"), vmem_limit_bytes=VMEM_LIMIT),
    )(x, x, x, mod, g, w_in, conv_w)


def _proj_pre_kernel(x_ref, mod_ref, g_ref, w_ref, o_ref, *, col0, n_scaled, q_scale, n_sub):
    sub = x_ref.shape[0] // n_sub
    for i in range(n_sub):
        rs = slice(i * sub, (i + 1) * sub)
        h = _norm_mod(x_ref[rs, :], g_ref[...], mod_ref[0:1, :], mod_ref[1:2, :]).astype(BF16)
        if n_scaled:
            o_ref[rs, :n_scaled] = (
                jnp.dot(h, w_ref[:, col0:col0 + n_scaled], preferred_element_type=F32) * q_scale).astype(BF16)
        o_ref[rs, n_scaled:] = jnp.dot(h, w_ref[:, col0 + n_scaled:], preferred_element_type=F32).astype(BF16)


def _proj_pre_call(x, mod, g, w, layer, mixer, tm, col0, n_scaled, q_scale, n_sub):
    b, l, d = x.shape
    n = w.shape[2] - col0
    kern = functools.partial(_proj_pre_kernel, col0=col0, n_scaled=n_scaled, q_scale=q_scale, n_sub=n_sub)
    return pl.pallas_call(
        kern,
        grid=(b, l // tm),
        in_specs=[
            pl.BlockSpec((None, tm, d), lambda i, t: (i, t, 0)),
            _mod_spec(mod, layer),
            _layer_spec(g, layer),
            _layer_spec(w, mixer),
        ],
        out_specs=pl.BlockSpec((None, tm, n), lambda i, t: (i, t, 0)),
        out_shape=jax.ShapeDtypeStruct((b, l, n), BF16),
        compiler_params=pltpu.CompilerParams(
            dimension_semantics=("arbitrary", "arbitrary"), vmem_limit_bytes=VMEM_LIMIT),
    )(x, mod, g, w)


def _post_kernel(x_ref, y_ref, mod_ref, g_ref, wo_ref, w1_ref, w2_ref, *rest, ff_chunk, final, n_sub):
    if final:
        fg_ref, o_ref = rest
    else:
        (o_ref,) = rest
    gate1 = mod_ref[2:3, :]
    shift2 = mod_ref[3:4, :]
    scale2 = mod_ref[4:5, :]
    gate2 = mod_ref[5:6, :]
    ff = w1_ref.shape[1]
    sub = x_ref.shape[0] // n_sub

    def out_proj(i):
        rs = slice(i * sub, (i + 1) * sub)
        return x_ref[rs, :] + gate1 * jnp.dot(y_ref[rs, :], wo_ref[...], preferred_element_type=F32)

    def mlp(x1):
        h = _norm_mod(x1, g_ref[...], shift2, scale2).astype(BF16)
        acc = None
        for f in range(0, ff, ff_chunk):
            a = jnp.dot(h, w1_ref[:, f:f + ff_chunk], preferred_element_type=F32)
            a = jnp.maximum(a, 0.0)
            a = (a * a).astype(BF16)
            part = jnp.dot(a, w2_ref[f:f + ff_chunk, :], preferred_element_type=F32)
            acc = part if acc is None else acc + part
        return x1 + gate2 * acc

    def finish(i, x2):
        if final:
            inv = lax.rsqrt(jnp.mean(x2 * x2, axis=-1, keepdims=True) + RMS_EPS)
            x2 = (x2 * inv) * fg_ref[...]
        o_ref[i * sub:(i + 1) * sub, :] = x2

    x1s = [out_proj(i) for i in range(n_sub)]
    x2s = [mlp(x1) for x1 in x1s]
    for i, x2 in enumerate(x2s):
        finish(i, x2)


def _post_call(x, y, mod, g, w_out, w1, w2, final_g, layer, mixer, tm, ff_chunk=1024, n_sub=2):
    b, l, d = x.shape
    final = final_g is not None
    kern = functools.partial(_post_kernel, ff_chunk=ff_chunk, final=final, n_sub=n_sub)
    in_specs = [
        pl.BlockSpec((None, tm, d), lambda i, t: (i, t, 0)),
        pl.BlockSpec((None, tm, d), lambda i, t: (i, t, 0)),
        _mod_spec(mod, layer),
        _layer_spec(g, layer),
        _layer_spec(w_out, mixer),
        _layer_spec(w1, layer),
        _layer_spec(w2, layer),
    ]
    args = [x, y, mod, g, w_out, w1, w2]
    if final:
        in_specs.append(_layer_spec(final_g, 0))
        args.append(final_g)
    return pl.pallas_call(
        kern,
        grid=(b, l // tm),
        in_specs=in_specs,
        out_specs=pl.BlockSpec((None, tm, d), lambda i, t: (i, t, 0)),
        out_shape=jax.ShapeDtypeStruct((b, l, d), F32),
        compiler_params=pltpu.CompilerParams(
            dimension_semantics=("arbitrary", "arbitrary"), vmem_limit_bytes=VMEM_LIMIT),
    )(*args)


def _key_row_start(g, rows):
    return min(max(Q_ROWS * g - WIN_ROWS // 2, 0), rows - K_ROWS)


def _key_col_start(cb):
    lo = min(max(Q_COLS * cb - WIN_COLS // 2, 0), GRID_W - WIN_COLS)
    hi = min(max(Q_COLS * (cb + 1) - 1 - WIN_COLS // 2, 0), GRID_W - WIN_COLS) + WIN_COLS
    start = min(lo // HALO * HALO, GRID_W - K_COLS)
    assert start <= lo and hi <= start + K_COLS
    return start


def _window_row(r, rows):
    return min(max(r - WIN_ROWS // 2, 0), rows - WIN_ROWS)


def _visible_tiles(g, blk, rows):
    rel = [_window_row(Q_ROWS * g + BLOCK_ROWS * blk + i, rows) - _key_row_start(g, rows)
           for i in range(BLOCK_ROWS)]
    return min(rel) // ROWS_PER_TILE, -(-(max(rel) + WIN_ROWS) // ROWS_PER_TILE)


def _rpb_lane_rows(rpb):
    nc = rpb.shape[-1]
    placed = [jnp.pad(rpb, ((0, 0), (0, 0), (K_COLS * j, 128 - K_COLS * j - nc))) for j in range(ROWS_PER_TILE)]
    return jnp.stack(placed, axis=2)


def _fill_bias(rv_ref, bias_ref, *, rows):
    n_groups = rows // Q_ROWS
    shape = (Q_COLS, 128)
    qcl = lax.broadcasted_iota(jnp.int32, shape, 0)
    lane = lax.broadcasted_iota(jnp.int32, shape, 1)
    t = lane & (K_COLS - 1)
    neg = jnp.full(shape, NEG_INF, F32)
    for cls, g in enumerate((0, 1, n_groups - 1)):
        kr0 = _key_row_start(g, rows)
        for cb in range(GRID_W // Q_COLS):
            kstart = _key_col_start(cb)
            cs = jnp.clip(qcl + (Q_COLS * cb - WIN_COLS // 2), 0, GRID_W - WIN_COLS) - kstart
            col_ok = (t >= cs) & (t < cs + WIN_COLS)
            delta = kstart - Q_COLS * cb + WIN_COLS - 1
            for hh in range(HEADS_PER_STEP):
                for qr in range(Q_ROWS):
                    r = Q_ROWS * g + qr
                    rs = _window_row(r, rows)
                    t_lo, t_hi = _visible_tiles(g, qr // BLOCK_ROWS, rows)
                    for tile in range(t_lo, t_hi):
                        krow = [kr0 + ROWS_PER_TILE * tile + j for j in range(ROWS_PER_TILE)]
                        seen = [j for j in range(ROWS_PER_TILE) if rs <= krow[j] < rs + WIN_ROWS]
                        dst = (cls, hh, cb, slice(qr * Q_COLS, (qr + 1) * Q_COLS),
                               slice(tile * 128, (tile + 1) * 128))
                        if not seen:
                            bias_ref[dst] = neg
                            continue
                        v = None
                        for j in range(ROWS_PER_TILE):
                            dr = min(max(krow[j] - r + WIN_ROWS - 1, 0), 2 * WIN_ROWS - 2)
                            piece = rv_ref[hh, dr, j:j + 1, :]
                            v = piece if v is None else v + piece
                        rolled = pltpu.roll(jnp.broadcast_to(v * LOG2E, shape), (-delta) % 128, axis=1,
                                            stride=1, stride_axis=0)
                        ok = col_ok & (lane >= K_COLS * seen[0]) & (lane < K_COLS * (seen[-1] + 1))
                        bias_ref[dst] = jnp.where(ok, rolled, neg)


def _attn_kernel(q_ref, k_ref, v_ref, kc_ref, vc_ref, rv_ref, o_ref, bias_ref, kf_ref, vf_ref, *,
                 rows, head_dim):
    n_batch = q_ref.shape[0]
    n_groups = rows // Q_ROWS
    n_cb = GRID_W // Q_COLS
    nq = Q_ROWS * Q_COLS
    nk = K_ROWS * K_COLS

    @pl.when(pl.program_id(1) == 0)
    def _():
        _fill_bias(rv_ref, bias_ref, rows=rows)

    lane = lax.broadcasted_iota(jnp.int32, (1, HEADS_PER_STEP * head_dim), 1)
    nt = (((1,), (1,)), ((), ()))

    converted = {}

    def window(src_ref, dst_ref, bi, g, cb):
        r0 = _key_row_start(g, rows)
        c0 = _key_col_start(cb)
        done = converted.get((id(dst_ref), bi), 0)
        if done < r0 + K_ROWS:
            tok = slice(done * GRID_W, (r0 + K_ROWS) * GRID_W)
            dst_ref[bi, tok, :] = src_ref[bi, tok, :].astype(F32)
            converted[(id(dst_ref), bi)] = r0 + K_ROWS
        pieces = [dst_ref[bi, (r0 + j) * GRID_W + c0:(r0 + j) * GRID_W + c0 + K_COLS, :]
                  for j in range(K_ROWS)]
        return jnp.concatenate(pieces, axis=0).astype(BF16)

    def scores(bi, g, cb):
        pieces = [q_ref[bi, (Q_ROWS * g + qr) * GRID_W + Q_COLS * cb:
                        (Q_ROWS * g + qr) * GRID_W + Q_COLS * (cb + 1), :] for qr in range(Q_ROWS)]
        q = jnp.concatenate(pieces, axis=0)
        zero = jnp.zeros_like(q)
        qh = jnp.concatenate(
            [jnp.where((lane >= hh * head_dim) & (lane < (hh + 1) * head_dim), q, zero)
             for hh in range(HEADS_PER_STEP)], axis=0)
        s_loc = lax.dot_general(qh, window(k_ref, kf_ref, bi, g, cb), nt, preferred_element_type=F32)
        s_ctx = lax.dot_general(qh, kc_ref[bi], nt, preferred_element_type=F32)
        return s_loc, s_ctx

    def attend(bi, g, cb, s_loc, s_ctx):
        cls = 0 if g == 0 else (2 if g == n_groups - 1 else 1)
        p_blocks, d_blocks = [], []
        for hh in range(HEADS_PER_STEP):
            for blk in range(Q_ROWS // BLOCK_ROWS):
                t_lo, t_hi = _visible_tiles(g, blk, rows)
                nb = BLOCK_ROWS * Q_COLS
                qsl = slice(blk * nb, (blk + 1) * nb)
                rsl = slice(hh * nq + qsl.start, hh * nq + qsl.stop)
                tiles = [s_loc[rsl, t * 128:(t + 1) * 128] + bias_ref[cls, hh, cb, qsl, t * 128:(t + 1) * 128]
                         for t in range(t_lo, t_hi)]
                tiles += [s_ctx[rsl, t * 128:(t + 1) * 128] for t in range(s_ctx.shape[1] // 128)]
                m = jnp.max(functools.reduce(jnp.maximum, tiles), axis=-1, keepdims=True)
                probs = [jnp.exp2(t - m) for t in tiles]
                d_blocks.append(jnp.sum(functools.reduce(jnp.add, probs), axis=-1, keepdims=True))
                n_loc = t_hi - t_lo
                pieces = [jnp.zeros((nb, 128), BF16)] * t_lo
                pieces += [t.astype(BF16) for t in probs[:n_loc]]
                pieces += [jnp.zeros((nb, 128), BF16)] * (nk // 128 - t_hi)
                pieces += [t.astype(BF16) for t in probs[n_loc:]]
                p_blocks.append(jnp.concatenate(pieces, axis=1))
        p = jnp.concatenate(p_blocks, axis=0)
        denom = jnp.concatenate(d_blocks, axis=0)
        o = (jnp.dot(p[:, :nk], window(v_ref, vf_ref, bi, g, cb), preferred_element_type=F32)
             + jnp.dot(p[:, nk:], vc_ref[bi], preferred_element_type=F32)) / denom
        out = o[(HEADS_PER_STEP - 1) * nq:]
        for hh in range(HEADS_PER_STEP - 2, -1, -1):
            out = jnp.where(lane < (hh + 1) * head_dim, o[hh * nq:(hh + 1) * nq], out)
        out = out.astype(BF16)
        for qr in range(Q_ROWS):
            tok = (Q_ROWS * g + qr) * GRID_W + Q_COLS * cb
            o_ref[bi, tok:tok + Q_COLS, :] = out[qr * Q_COLS:(qr + 1) * Q_COLS]

    units = [(bi, g, cb) for bi in range(n_batch) for g in range(n_groups) for cb in range(n_cb)]
    nxt = scores(*units[0])
    for u, unit in enumerate(units):
        cur = nxt
        if u + 1 < len(units):
            nxt = scores(*units[u + 1])
        attend(*unit, *cur)


def _attn_call(qkv, ckv, rpb_rows):
    b, s, d3 = qkv.shape
    d = d3 // 3
    ctx_len = ckv.shape[1]
    head_dim = d // N_HEADS
    w = HEADS_PER_STEP * head_dim
    assert w == 128
    n_hp = d // w
    rows = s // GRID_W
    n_dr = rpb_rows.shape[1]
    bb = ATTN_BATCH if b % ATTN_BATCH == 0 else 1
    kern = functools.partial(_attn_kernel, rows=rows, head_dim=head_dim)
    return pl.pallas_call(
        kern,
        grid=(n_hp, b // bb),
        in_specs=[
            pl.BlockSpec((bb, s, w), lambda h, i: (i, 0, h)),
            pl.BlockSpec((bb, s, w), lambda h, i: (i, 0, n_hp + h)),
            pl.BlockSpec((bb, s, w), lambda h, i: (i, 0, 2 * n_hp + h)),
            pl.BlockSpec((bb, ctx_len, w), lambda h, i: (i, 0, h)),
            pl.BlockSpec((bb, ctx_len, w), lambda h, i: (i, 0, n_hp + h)),
            pl.BlockSpec((HEADS_PER_STEP, n_dr, ROWS_PER_TILE, 128), lambda h, i: (h, 0, 0, 0)),
        ],
        out_specs=pl.BlockSpec((bb, s, w), lambda h, i: (i, 0, h)),
        out_shape=jax.ShapeDtypeStruct((b, s, d), BF16),
        scratch_shapes=[
            pltpu.VMEM((3, HEADS_PER_STEP, GRID_W // Q_COLS, Q_ROWS * Q_COLS, K_ROWS * K_COLS), F32),
            pltpu.VMEM((bb, s, w), F32),
            pltpu.VMEM((bb, s, w), F32),
        ],
        compiler_params=pltpu.CompilerParams(
            dimension_semantics=("arbitrary", "arbitrary"), vmem_limit_bytes=VMEM_LIMIT),
    )(qkv, qkv, qkv, ckv, ckv, rpb_rows)


def kernel(x, c, ctx, c_ctx, norm1_g, norm2_g, ada_w, ada_b, conv_in_w, conv_w, conv_out_w,
           attn_qkv_w, attn_rpb, attn_out_w, mlp_w1, mlp_w2, final_g):
    b, s, d = x.shape
    ctx_len = ctx.shape[1]
    depth = ada_w.shape[0]
    assert depth == 2 and s % (GRID_W * Q_ROWS) == 0 and s // GRID_W >= K_ROWS and d % N_HEADS == 0
    head_dim = d // N_HEADS
    tm_pre, tm_post, tmc = 1024, 512, ctx_len

    pad = (-(b + 1)) % 8
    cond = jnp.concatenate([c, c_ctx[None, :], jnp.zeros((pad, d), F32)], axis=0)
    mods = _ada_call(cond, ada_w, ada_b).reshape(depth, b + 1 + pad, 6, d)
    mod_lat = mods[:, :b]
    mod_ctx = mods[:, b:b + 1]

    n1 = norm1_g.reshape(depth, 1, d)
    n2 = norm2_g.reshape(depth, 1, d)
    fg = final_g.reshape(1, 1, d)
    w1 = mlp_w1.astype(BF16)
    w2 = mlp_w2.astype(BF16)
    w_in = conv_in_w.astype(BF16)
    w_co = conv_out_w.astype(BF16)
    w_qkv = attn_qkv_w.astype(BF16)
    w_ao = attn_out_w.astype(BF16)

    y = _conv_pre_call(x, mod_lat, n1, w_in, conv_w, 0, 0, tm_pre, 2)
    yc = _conv_pre_call(ctx, mod_ctx, n1, w_in, conv_w, 0, 0, tmc, 1)
    x = _post_call(x, y, mod_lat, n2, w_co, w1, w2, None, 0, 0, tm_post)
    ctx = _post_call(ctx, yc, mod_ctx, n2, w_co, w1, w2, None, 0, 0, tmc, n_sub=1)

    qkv = _proj_pre_call(x, mod_lat, n1, w_qkv, 1, 0, tm_pre, 0, d, head_dim ** -0.5 * LOG2E, 2)
    ckv = _proj_pre_call(ctx, mod_ctx, n1, w_qkv, 1, 0, tmc, d, 0, 1.0, 1)
    o = _attn_call(qkv, ckv, _rpb_lane_rows(attn_rpb[0]))
    return _post_call(x, o, mod_lat, n2, w_ao, w1, w2, fg, 1, 0, tm_post)
```

```python
import functools

import jax
import jax.numpy as jnp
from jax import lax
from jax.experimental import pallas as pl
from jax.experimental.pallas import tpu as pltpu

GRID_W = 64
N_HEADS = 16
WIN_ROWS = 8
WIN_COLS = 16
RMS_EPS = 1e-6
NEG_INF = -1e30
LOG2E = 1.4426950408889634

HALO = 8
HEADS_PER_STEP = 2
Q_ROWS, Q_COLS = 8, 16
K_ROWS, K_COLS = 16, 32
ROWS_PER_TILE = 128 // K_COLS
BLOCK_ROWS = 4
ATTN_BATCH = 2
VMEM_LIMIT = 56 * 1024 * 1024

BF16 = jnp.bfloat16
F32 = jnp.float32


def _norm_mod(v, g, shift, scale):
    inv = lax.rsqrt(jnp.mean(v * v, axis=-1, keepdims=True) + RMS_EPS)
    return ((v * inv) * g) * (1.0 + scale) + shift


def _ada_kernel(cond_ref, w_ref, b_ref, o_ref):
    cnd = cond_ref[...]
    s = (cnd / (1.0 + jnp.exp(-cnd))).astype(BF16)
    o_ref[...] = jnp.dot(s, w_ref[...].astype(BF16), preferred_element_type=F32) + b_ref[...]


def _ada_call(cond, ada_w, ada_b):
    depth, d, n = ada_w.shape
    rows = cond.shape[0]
    bn = d
    return pl.pallas_call(
        _ada_kernel,
        grid=(depth, n // bn),
        in_specs=[
            pl.BlockSpec((rows, d), lambda i, j: (0, 0)),
            pl.BlockSpec((None, d, bn), lambda i, j: (i, 0, j)),
            pl.BlockSpec((None, 1, bn), lambda i, j: (i, 0, j)),
        ],
        out_specs=pl.BlockSpec((None, rows, bn), lambda i, j: (i, 0, j)),
        out_shape=jax.ShapeDtypeStruct((depth, rows, n), F32),
        compiler_params=pltpu.CompilerParams(
            dimension_semantics=("arbitrary", "arbitrary"), vmem_limit_bytes=VMEM_LIMIT),
    )(cond, ada_w, ada_b.reshape(depth, 1, n))


def _mod_spec(mod, layer):
    if mod.shape[1] == 1:
        return pl.BlockSpec((None, None, 6, mod.shape[3]), lambda b, t: (layer, 0, 0, 0))
    return pl.BlockSpec((None, None, 6, mod.shape[3]), lambda b, t: (layer, b, 0, 0))


def _layer_spec(arr, layer):
    nd = arr.ndim - 1
    return pl.BlockSpec((None,) + arr.shape[1:], lambda b, t: (layer,) + (0,) * nd,
                        pipeline_mode=pl.Buffered(1))


def _conv_pre_kernel(x_ref, xp_ref, xn_ref, mod_ref, g_ref, w_ref, cw_ref, o_ref, *, tm, nt, d, n_sub):
    t = pl.program_id(1)
    shift = mod_ref[0:1, :]
    scale = mod_ref[1:2, :]
    sub = tm // n_sub
    n_ext = sub + 2 * HALO
    xe = jnp.concatenate([xp_ref[...], x_ref[...], xn_ref[...]], axis=0)
    row = lax.broadcasted_iota(jnp.int32, (n_ext, 1), 0)

    def project(i):
        he = _norm_mod(xe[i * sub:i * sub + n_ext], g_ref[...], shift, scale).astype(BF16)
        return he, jnp.dot(he, w_ref[:, d:], preferred_element_type=F32)

    def gate_conv(i, he, cv):
        u = cv[:, :d] * cv[:, d:]
        if i == 0:
            u = jnp.where((row >= HALO) | (t > 0), u, 0.0)
        if i == n_sub - 1:
            u = jnp.where((row < sub + HALO) | (t < nt - 1), u, 0.0)
        u_prev = pltpu.roll(u, 1, axis=0)[HALO:HALO + sub]
        u_next = pltpu.roll(u, n_ext - 1, axis=0)[HALO:HALO + sub]
        conv = u_prev * cw_ref[0:1, :] + u[HALO:HALO + sub] * cw_ref[1:2, :] + u_next * cw_ref[2:3, :]
        bg = jnp.dot(he[HALO:HALO + sub], w_ref[:, :d], preferred_element_type=F32)
        o_ref[i * sub:(i + 1) * sub, :] = (bg * conv).astype(BF16)

    projected = [project(i) for i in range(n_sub)]
    for i, (he, cv) in enumerate(projected):
        gate_conv(i, he, cv)


def _conv_pre_call(x, mod, g, w_in, conv_w, layer, mixer, tm, n_sub):
    b, l, d = x.shape
    nt = l // tm
    hb = tm // HALO
    last_hb = l // HALO - 1
    kern = functools.partial(_conv_pre_kernel, tm=tm, nt=nt, d=d, n_sub=n_sub)
    return pl.pallas_call(
        kern,
        grid=(b, nt),
        in_specs=[
            pl.BlockSpec((None, tm, d), lambda i, t: (i, t, 0)),
            pl.BlockSpec((None, HALO, d), lambda i, t: (i, jnp.maximum(t * hb - 1, 0), 0)),
            pl.BlockSpec((None, HALO, d), lambda i, t: (i, jnp.minimum((t + 1) * hb, last_hb), 0)),
            _mod_spec(mod, layer),
            _layer_spec(g, layer),
            _layer_spec(w_in, mixer),
            _layer_spec(conv_w, mixer),
        ],
        out_specs=pl.BlockSpec((None, tm, d), lambda i, t: (i, t, 0)),
        out_shape=jax.ShapeDtypeStruct((b, l, d), BF16),
        compiler_params=pltpu.CompilerParams(
            dimension_semantics=("arbitrary", "arbitrary"), vmem_limit_bytes=VMEM_LIMIT),
    )(x, x, x, mod, g, w_in, conv_w)


def _proj_pre_kernel(x_ref, mod_ref, g_ref, w_ref, o_ref, *, col0, n_scaled, q_scale, n_sub):
    sub = x_ref.shape[0] // n_sub
    for i in range(n_sub):
        rs = slice(i * sub, (i + 1) * sub)
        h = _norm_mod(x_ref[rs, :], g_ref[...], mod_ref[0:1, :], mod_ref[1:2, :]).astype(BF16)
        if n_scaled:
            o_ref[rs, :n_scaled] = (
                jnp.dot(h, w_ref[:, col0:col0 + n_scaled], preferred_element_type=F32) * q_scale).astype(BF16)
        o_ref[rs, n_scaled:] = jnp.dot(h, w_ref[:, col0 + n_scaled:], preferred_element_type=F32).astype(BF16)


def _proj_pre_call(x, mod, g, w, layer, mixer, tm, col0, n_scaled, q_scale, n_sub):
    b, l, d = x.shape
    n = w.shape[2] - col0
    kern = functools.partial(_proj_pre_kernel, col0=col0, n_scaled=n_scaled, q_scale=q_scale, n_sub=n_sub)
    return pl.pallas_call(
        kern,
        grid=(b, l // tm),
        in_specs=[
            pl.BlockSpec((None, tm, d), lambda i, t: (i, t, 0)),
            _mod_spec(mod, layer),
            _layer_spec(g, layer),
            _layer_spec(w, mixer),
        ],
        out_specs=pl.BlockSpec((None, tm, n), lambda i, t: (i, t, 0)),
        out_shape=jax.ShapeDtypeStruct((b, l, n), BF16),
        compiler_params=pltpu.CompilerParams(
            dimension_semantics=("arbitrary", "arbitrary"), vmem_limit_bytes=VMEM_LIMIT),
    )(x, mod, g, w)


def _post_kernel(x_ref, y_ref, mod_ref, g_ref, wo_ref, w1_ref, w2_ref, *rest, ff_chunk, final, n_sub):
    if final:
        fg_ref, o_ref = rest
    else:
        (o_ref,) = rest
    gate1 = mod_ref[2:3, :]
    shift2 = mod_ref[3:4, :]
    scale2 = mod_ref[4:5, :]
    gate2 = mod_ref[5:6, :]
    ff = w1_ref.shape[1]
    sub = x_ref.shape[0] // n_sub

    def out_proj(i):
        rs = slice(i * sub, (i + 1) * sub)
        return x_ref[rs, :] + gate1 * jnp.dot(y_ref[rs, :], wo_ref[...], preferred_element_type=F32)

    def mlp(x1):
        h = _norm_mod(x1, g_ref[...], shift2, scale2).astype(BF16)
        acc = None
        for f in range(0, ff, ff_chunk):
            a = jnp.dot(h, w1_ref[:, f:f + ff_chunk], preferred_element_type=F32)
            a = jnp.maximum(a, 0.0)
            a = (a * a).astype(BF16)
            part = jnp.dot(a, w2_ref[f:f + ff_chunk, :], preferred_element_type=F32)
            acc = part if acc is None else acc + part
        return x1 + gate2 * acc

    def finish(i, x2):
        if final:
            inv = lax.rsqrt(jnp.mean(x2 * x2, axis=-1, keepdims=True) + RMS_EPS)
            x2 = (x2 * inv) * fg_ref[...]
        o_ref[i * sub:(i + 1) * sub, :] = x2

    x1s = [out_proj(i) for i in range(n_sub)]
    x2s = [mlp(x1) for x1 in x1s]
    for i, x2 in enumerate(x2s):
        finish(i, x2)


def _post_call(x, y, mod, g, w_out, w1, w2, final_g, layer, mixer, tm, ff_chunk=1024, n_sub=2):
    b, l, d = x.shape
    final = final_g is not None
    kern = functools.partial(_post_kernel, ff_chunk=ff_chunk, final=final, n_sub=n_sub)
    in_specs = [
        pl.BlockSpec((None, tm, d), lambda i, t: (i, t, 0)),
        pl.BlockSpec((None, tm, d), lambda i, t: (i, t, 0)),
        _mod_spec(mod, layer),
        _layer_spec(g, layer),
        _layer_spec(w_out, mixer),
        _layer_spec(w1, layer),
        _layer_spec(w2, layer),
    ]
    args = [x, y, mod, g, w_out, w1, w2]
    if final:
        in_specs.append(_layer_spec(final_g, 0))
        args.append(final_g)
    return pl.pallas_call(
        kern,
        grid=(b, l // tm),
        in_specs=in_specs,
        out_specs=pl.BlockSpec((None, tm, d), lambda i, t: (i, t, 0)),
        out_shape=jax.ShapeDtypeStruct((b, l, d), F32),
        compiler_params=pltpu.CompilerParams(
            dimension_semantics=("arbitrary", "arbitrary"), vmem_limit_bytes=VMEM_LIMIT),
    )(*args)


def _key_row_start(g, rows):
    return min(max(Q_ROWS * g - WIN_ROWS // 2, 0), rows - K_ROWS)


def _key_col_start(cb):
    lo = min(max(Q_COLS * cb - WIN_COLS // 2, 0), GRID_W - WIN_COLS)
    hi = min(max(Q_COLS * (cb + 1) - 1 - WIN_COLS // 2, 0), GRID_W - WIN_COLS) + WIN_COLS
    start = min(lo // HALO * HALO, GRID_W - K_COLS)
    assert start <= lo and hi <= start + K_COLS
    return start


def _window_row(r, rows):
    return min(max(r - WIN_ROWS // 2, 0), rows - WIN_ROWS)


def _visible_tiles(g, blk, rows):
    rel = [_window_row(Q_ROWS * g + BLOCK_ROWS * blk + i, rows) - _key_row_start(g, rows)
           for i in range(BLOCK_ROWS)]
    return min(rel) // ROWS_PER_TILE, -(-(max(rel) + WIN_ROWS) // ROWS_PER_TILE)


def _rpb_lane_rows(rpb):
    nc = rpb.shape[-1]
    placed = [jnp.pad(rpb, ((0, 0), (0, 0), (K_COLS * j, 128 - K_COLS * j - nc))) for j in range(ROWS_PER_TILE)]
    return jnp.stack(placed, axis=2)


def _fill_bias(rv_ref, bias_ref, *, rows):
    n_groups = rows // Q_ROWS
    shape = (Q_COLS, 128)
    qcl = lax.broadcasted_iota(jnp.int32, shape, 0)
    lane = lax.broadcasted_iota(jnp.int32, shape, 1)
    t = lane & (K_COLS - 1)
    neg = jnp.full(shape, NEG_INF, F32)
    for cls, g in enumerate((0, 1, n_groups - 1)):
        kr0 = _key_row_start(g, rows)
        for cb in range(GRID_W // Q_COLS):
            kstart = _key_col_start(cb)
            cs = jnp.clip(qcl + (Q_COLS * cb - WIN_COLS // 2), 0, GRID_W - WIN_COLS) - kstart
            col_ok = (t >= cs) & (t < cs + WIN_COLS)
            delta = kstart - Q_COLS * cb + WIN_COLS - 1
            for hh in range(HEADS_PER_STEP):
                for qr in range(Q_ROWS):
                    r = Q_ROWS * g + qr
                    rs = _window_row(r, rows)
                    t_lo, t_hi = _visible_tiles(g, qr // BLOCK_ROWS, rows)
                    for tile in range(t_lo, t_hi):
                        krow = [kr0 + ROWS_PER_TILE * tile + j for j in range(ROWS_PER_TILE)]
                        seen = [j for j in range(ROWS_PER_TILE) if rs <= krow[j] < rs + WIN_ROWS]
                        dst = (cls, hh, cb, slice(qr * Q_COLS, (qr + 1) * Q_COLS),
                               slice(tile * 128, (tile + 1) * 128))
                        if not seen:
                            bias_ref[dst] = neg
                            continue
                        v = None
                        for j in range(ROWS_PER_TILE):
                            dr = min(max(krow[j] - r + WIN_ROWS - 1, 0), 2 * WIN_ROWS - 2)
                            piece = rv_ref[hh, dr, j:j + 1, :]
                            v = piece if v is None else v + piece
                        rolled = pltpu.roll(jnp.broadcast_to(v * LOG2E, shape), (-delta) % 128, axis=1,
                                            stride=1, stride_axis=0)
                        ok = col_ok & (lane >= K_COLS * seen[0]) & (lane < K_COLS * (seen[-1] + 1))
                        bias_ref[dst] = jnp.where(ok, rolled, neg)


def _attn_kernel(q_ref, k_ref, v_ref, kc_ref, vc_ref, rv_ref, o_ref, bias_ref, kf_ref, vf_ref, *,
                 rows, head_dim):
    n_batch = q_ref.shape[0]
    n_groups = rows // Q_ROWS
    n_cb = GRID_W // Q_COLS
    nq = Q_ROWS * Q_COLS
    nk = K_ROWS * K_COLS

    @pl.when(pl.program_id(1) == 0)
    def _():
        _fill_bias(rv_ref, bias_ref, rows=rows)

    lane = lax.broadcasted_iota(jnp.int32, (1, HEADS_PER_STEP * head_dim), 1)
    nt = (((1,), (1,)), ((), ()))

    converted = {}

    def window(src_ref, dst_ref, bi, g, cb):
        r0 = _key_row_start(g, rows)
        c0 = _key_col_start(cb)
        done = converted.get((id(dst_ref), bi), 0)
        if done < r0 + K_ROWS:
            tok = slice(done * GRID_W, (r0 + K_ROWS) * GRID_W)
            dst_ref[bi, tok, :] = src_ref[bi, tok, :].astype(F32)
            converted[(id(dst_ref), bi)] = r0 + K_ROWS
        pieces = [dst_ref[bi, (r0 + j) * GRID_W + c0:(r0 + j) * GRID_W + c0 + K_COLS, :]
                  for j in range(K_ROWS)]
        return jnp.concatenate(pieces, axis=0).astype(BF16)

    def scores(bi, g, cb):
        pieces = [q_ref[bi, (Q_ROWS * g + qr) * GRID_W + Q_COLS * cb:
                        (Q_ROWS * g + qr) * GRID_W + Q_COLS * (cb + 1), :] for qr in range(Q_ROWS)]
        q = jnp.concatenate(pieces, axis=0)
        zero = jnp.zeros_like(q)
        qh = jnp.concatenate(
            [jnp.where((lane >= hh * head_dim) & (lane < (hh + 1) * head_dim), q, zero)
             for hh in range(HEADS_PER_STEP)], axis=0)
        s_loc = lax.dot_general(qh, window(k_ref, kf_ref, bi, g, cb), nt, preferred_element_type=F32)
        s_ctx = lax.dot_general(qh, kc_ref[bi], nt, preferred_element_type=F32)
        return s_loc, s_ctx

    def attend(bi, g, cb, s_loc, s_ctx):
        cls = 0 if g == 0 else (2 if g == n_groups - 1 else 1)
        p_blocks = []
        for hh in range(HEADS_PER_STEP):
            for blk in range(Q_ROWS // BLOCK_ROWS):
                t_lo, t_hi = _visible_tiles(g, blk, rows)
                nb = BLOCK_ROWS * Q_COLS
                qsl = slice(blk * nb, (blk + 1) * nb)
                rsl = slice(hh * nq + qsl.start, hh * nq + qsl.stop)
                tiles = [s_loc[rsl, t * 128:(t + 1) * 128] + bias_ref[cls, hh, cb, qsl, t * 128:(t + 1) * 128]
                         for t in range(t_lo, t_hi)]
                tiles += [s_ctx[rsl, t * 128:(t + 1) * 128] for t in range(s_ctx.shape[1] // 128)]
                m = jnp.max(functools.reduce(jnp.maximum, tiles), axis=-1, keepdims=True)
                probs = [jnp.exp2(t - m).astype(BF16) for t in tiles]
                n_loc = t_hi - t_lo
                pieces = [jnp.zeros((nb, 128), BF16)] * t_lo + probs[:n_loc]
                pieces += [jnp.zeros((nb, 128), BF16)] * (nk // 128 - t_hi) + probs[n_loc:]
                p_blocks.append(jnp.concatenate(pieces, axis=1))
        p = jnp.concatenate(p_blocks, axis=0)
        vw = jnp.concatenate([window(v_ref, vf_ref, bi, g, cb), jnp.ones((nk, 128), BF16)], axis=1)
        vc = jnp.concatenate([vc_ref[bi], jnp.ones((vc_ref.shape[1], 128), BF16)], axis=1)
        o = (jnp.dot(p[:, :nk], vw, preferred_element_type=F32)
             + jnp.dot(p[:, nk:], vc, preferred_element_type=F32))
        o = o[:, :128] / o[:, 128:]
        out = o[(HEADS_PER_STEP - 1) * nq:]
        for hh in range(HEADS_PER_STEP - 2, -1, -1):
            out = jnp.where(lane < (hh + 1) * head_dim, o[hh * nq:(hh + 1) * nq], out)
        out = out.astype(BF16)
        for qr in range(Q_ROWS):
            tok = (Q_ROWS * g + qr) * GRID_W + Q_COLS * cb
            o_ref[bi, tok:tok + Q_COLS, :] = out[qr * Q_COLS:(qr + 1) * Q_COLS]

    units = [(bi, g, cb) for bi in range(n_batch) for g in range(n_groups) for cb in range(n_cb)]
    nxt = scores(*units[0])
    for u, unit in enumerate(units):
        cur = nxt
        if u + 1 < len(units):
            nxt = scores(*units[u + 1])
        attend(*unit, *cur)


def _attn_call(qkv, ckv, rpb_rows):
    b, s, d3 = qkv.shape
    d = d3 // 3
    ctx_len = ckv.shape[1]
    head_dim = d // N_HEADS
    w = HEADS_PER_STEP * head_dim
    assert w == 128
    n_hp = d // w
    rows = s // GRID_W
    n_dr = rpb_rows.shape[1]
    bb = ATTN_BATCH if b % ATTN_BATCH == 0 else 1
    kern = functools.partial(_attn_kernel, rows=rows, head_dim=head_dim)
    return pl.pallas_call(
        kern,
        grid=(n_hp, b // bb),
        in_specs=[
            pl.BlockSpec((bb, s, w), lambda h, i: (i, 0, h)),
            pl.BlockSpec((bb, s, w), lambda h, i: (i, 0, n_hp + h)),
            pl.BlockSpec((bb, s, w), lambda h, i: (i, 0, 2 * n_hp + h)),
            pl.BlockSpec((bb, ctx_len, w), lambda h, i: (i, 0, h)),
            pl.BlockSpec((bb, ctx_len, w), lambda h, i: (i, 0, n_hp + h)),
            pl.BlockSpec((HEADS_PER_STEP, n_dr, ROWS_PER_TILE, 128), lambda h, i: (h, 0, 0, 0)),
        ],
        out_specs=pl.BlockSpec((bb, s, w), lambda h, i: (i, 0, h)),
        out_shape=jax.ShapeDtypeStruct((b, s, d), BF16),
        scratch_shapes=[
            pltpu.VMEM((3, HEADS_PER_STEP, GRID_W // Q_COLS, Q_ROWS * Q_COLS, K_ROWS * K_COLS), F32),
            pltpu.VMEM((bb, s, w), F32),
            pltpu.VMEM((bb, s, w), F32),
        ],
        compiler_params=pltpu.CompilerParams(
            dimension_semantics=("arbitrary", "arbitrary"), vmem_limit_bytes=VMEM_LIMIT),
    )(qkv, qkv, qkv, ckv, ckv, rpb_rows)


def kernel(x, c, ctx, c_ctx, norm1_g, norm2_g, ada_w, ada_b, conv_in_w, conv_w, conv_out_w,
           attn_qkv_w, attn_rpb, attn_out_w, mlp_w1, mlp_w2, final_g):
    b, s, d = x.shape
    ctx_len = ctx.shape[1]
    depth = ada_w.shape[0]
    assert depth == 2 and s % (GRID_W * Q_ROWS) == 0 and s // GRID_W >= K_ROWS and d % N_HEADS == 0
    head_dim = d // N_HEADS
    tm_pre, tm_post, tmc = 1024, 512, ctx_len

    pad = (-(b + 1)) % 8
    cond = jnp.concatenate([c, c_ctx[None, :], jnp.zeros((pad, d), F32)], axis=0)
    mods = _ada_call(cond, ada_w, ada_b).reshape(depth, b + 1 + pad, 6, d)
    mod_lat = mods[:, :b]
    mod_ctx = mods[:, b:b + 1]

    n1 = norm1_g.reshape(depth, 1, d)
    n2 = norm2_g.reshape(depth, 1, d)
    fg = final_g.reshape(1, 1, d)
    w1 = mlp_w1.astype(BF16)
    w2 = mlp_w2.astype(BF16)
    w_in = conv_in_w.astype(BF16)
    w_co = conv_out_w.astype(BF16)
    w_qkv = attn_qkv_w.astype(BF16)
    w_ao = attn_out_w.astype(BF16)

    y = _conv_pre_call(x, mod_lat, n1, w_in, conv_w, 0, 0, tm_pre, 2)
    yc = _conv_pre_call(ctx, mod_ctx, n1, w_in, conv_w, 0, 0, tmc, 1)
    x = _post_call(x, y, mod_lat, n2, w_co, w1, w2, None, 0, 0, tm_post)
    ctx = _post_call(ctx, yc, mod_ctx, n2, w_co, w1, w2, None, 0, 0, tmc, n_sub=1)

    qkv = _proj_pre_call(x, mod_lat, n1, w_qkv, 1, 0, tm_pre, 0, d, head_dim ** -0.5 * LOG2E, 2)
    ckv = _proj_pre_call(ctx, mod_ctx, n1, w_qkv, 1, 0, tmc, d, 0, 1.0, 1)
    o = _attn_call(qkv, ckv, _rpb_lane_rows(attn_rpb[0]))
    return _post_call(x, o, mod_lat, n2, w_ao, w1, w2, fg, 1, 0, tm_post)
```

```python
import functools

import jax
import jax.numpy as jnp
from jax import lax
from jax.experimental import pallas as pl
from jax.experimental.pallas import tpu as pltpu

GRID_W = 64
N_HEADS = 16
WIN_ROWS = 8
WIN_COLS = 16
RMS_EPS = 1e-6
NEG_INF = -1e30
LOG2E = 1.4426950408889634

HALO = 8
HEADS_PER_STEP = 2
Q_ROWS, Q_COLS = 8, 16
K_ROWS, K_COLS = 16, 32
ROWS_PER_TILE = 128 // K_COLS
BLOCK_ROWS = 4
ATTN_BATCH = 2
VMEM_LIMIT = 56 * 1024 * 1024

BF16 = jnp.bfloat16
F32 = jnp.float32


def _norm_mod(v, g, shift, scale):
    inv = lax.rsqrt(jnp.mean(v * v, axis=-1, keepdims=True) + RMS_EPS)
    return ((v * inv) * g) * (1.0 + scale) + shift


def _ada_kernel(cond_ref, w_ref, b_ref, o_ref):
    cnd = cond_ref[...]
    s = (cnd / (1.0 + jnp.exp(-cnd))).astype(BF16)
    o_ref[...] = jnp.dot(s, w_ref[...].astype(BF16), preferred_element_type=F32) + b_ref[...]


def _ada_call(cond, ada_w, ada_b):
    depth, d, n = ada_w.shape
    rows = cond.shape[0]
    bn = d
    return pl.pallas_call(
        _ada_kernel,
        grid=(depth, n // bn),
        in_specs=[
            pl.BlockSpec((rows, d), lambda i, j: (0, 0)),
            pl.BlockSpec((None, d, bn), lambda i, j: (i, 0, j)),
            pl.BlockSpec((None, 1, bn), lambda i, j: (i, 0, j)),
        ],
        out_specs=pl.BlockSpec((None, rows, bn), lambda i, j: (i, 0, j)),
        out_shape=jax.ShapeDtypeStruct((depth, rows, n), F32),
        compiler_params=pltpu.CompilerParams(
            dimension_semantics=("arbitrary", "arbitrary"), vmem_limit_bytes=VMEM_LIMIT),
    )(cond, ada_w, ada_b.reshape(depth, 1, n))


def _mod_spec(mod, layer):
    if mod.shape[1] == 1:
        return pl.BlockSpec((None, None, 6, mod.shape[3]), lambda b, t: (layer, 0, 0, 0))
    return pl.BlockSpec((None, None, 6, mod.shape[3]), lambda b, t: (layer, b, 0, 0))


def _layer_spec(arr, layer):
    nd = arr.ndim - 1
    return pl.BlockSpec((None,) + arr.shape[1:], lambda b, t: (layer,) + (0,) * nd,
                        pipeline_mode=pl.Buffered(1))


def _conv_pre_kernel(x_ref, xp_ref, xn_ref, mod_ref, g_ref, w_ref, cw_ref, o_ref, *, tm, nt, d, n_sub):
    t = pl.program_id(1)
    shift = mod_ref[0:1, :]
    scale = mod_ref[1:2, :]
    sub = tm // n_sub
    n_ext = sub + 2 * HALO
    xe = jnp.concatenate([xp_ref[...], x_ref[...], xn_ref[...]], axis=0)
    row = lax.broadcasted_iota(jnp.int32, (n_ext, 1), 0)

    def project(i):
        he = _norm_mod(xe[i * sub:i * sub + n_ext], g_ref[...], shift, scale).astype(BF16)
        return he, jnp.dot(he, w_ref[:, d:], preferred_element_type=F32)

    def gate_conv(i, he, cv):
        u = cv[:, :d] * cv[:, d:]
        if i == 0:
            u = jnp.where((row >= HALO) | (t > 0), u, 0.0)
        if i == n_sub - 1:
            u = jnp.where((row < sub + HALO) | (t < nt - 1), u, 0.0)
        u_prev = pltpu.roll(u, 1, axis=0)[HALO:HALO + sub]
        u_next = pltpu.roll(u, n_ext - 1, axis=0)[HALO:HALO + sub]
        conv = u_prev * cw_ref[0:1, :] + u[HALO:HALO + sub] * cw_ref[1:2, :] + u_next * cw_ref[2:3, :]
        bg = jnp.dot(he[HALO:HALO + sub], w_ref[:, :d], preferred_element_type=F32)
        o_ref[i * sub:(i + 1) * sub, :] = (bg * conv).astype(BF16)

    projected = [project(i) for i in range(n_sub)]
    for i, (he, cv) in enumerate(projected):
        gate_conv(i, he, cv)


def _conv_pre_call(x, mod, g, w_in, conv_w, layer, mixer, tm, n_sub):
    b, l, d = x.shape
    nt = l // tm
    hb = tm // HALO
    last_hb = l // HALO - 1
    kern = functools.partial(_conv_pre_kernel, tm=tm, nt=nt, d=d, n_sub=n_sub)
    return pl.pallas_call(
        kern,
        grid=(b, nt),
        in_specs=[
            pl.BlockSpec((None, tm, d), lambda i, t: (i, t, 0)),
            pl.BlockSpec((None, HALO, d), lambda i, t: (i, jnp.maximum(t * hb - 1, 0), 0)),
            pl.BlockSpec((None, HALO, d), lambda i, t: (i, jnp.minimum((t + 1) * hb, last_hb), 0)),
            _mod_spec(mod, layer),
            _layer_spec(g, layer),
            _layer_spec(w_in, mixer),
            _layer_spec(conv_w, mixer),
        ],
        out_specs=pl.BlockSpec((None, tm, d), lambda i, t: (i, t, 0)),
        out_shape=jax.ShapeDtypeStruct((b, l, d), BF16),
        compiler_params=pltpu.CompilerParams(
            dimension_semantics=("arbitrary", "arbitrary"), vmem_limit_bytes=VMEM_LIMIT),
    )(x, x, x, mod, g, w_in, conv_w)


def _proj_pre_kernel(x_ref, mod_ref, g_ref, w_ref, o_ref, *, col0, n_scaled, q_scale, n_sub):
    sub = x_ref.shape[0] // n_sub
    for i in range(n_sub):
        rs = slice(i * sub, (i + 1) * sub)
        h = _norm_mod(x_ref[rs, :], g_ref[...], mod_ref[0:1, :], mod_ref[1:2, :]).astype(BF16)
        if n_scaled:
            o_ref[rs, :n_scaled] = (
                jnp.dot(h, w_ref[:, col0:col0 + n_scaled], preferred_element_type=F32) * q_scale).astype(BF16)
        o_ref[rs, n_scaled:] = jnp.dot(h, w_ref[:, col0 + n_scaled:], preferred_element_type=F32).astype(BF16)


def _proj_pre_call(x, mod, g, w, layer, mixer, tm, col0, n_scaled, q_scale, n_sub):
    b, l, d = x.shape
    n = w.shape[2] - col0
    kern = functools.partial(_proj_pre_kernel, col0=col0, n_scaled=n_scaled, q_scale=q_scale, n_sub=n_sub)
    return pl.pallas_call(
        kern,
        grid=(b, l // tm),
        in_specs=[
            pl.BlockSpec((None, tm, d), lambda i, t: (i, t, 0)),
            _mod_spec(mod, layer),
            _layer_spec(g, layer),
            _layer_spec(w, mixer),
        ],
        out_specs=pl.BlockSpec((None, tm, n), lambda i, t: (i, t, 0)),
        out_shape=jax.ShapeDtypeStruct((b, l, n), BF16),
        compiler_params=pltpu.CompilerParams(
            dimension_semantics=("arbitrary", "arbitrary"), vmem_limit_bytes=VMEM_LIMIT),
    )(x, mod, g, w)


def _post_kernel(x_ref, y_ref, mod_ref, g_ref, wo_ref, w1_ref, w2_ref, *rest, ff_chunk, final, n_sub):
    if final:
        fg_ref, o_ref = rest
    else:
        (o_ref,) = rest
    gate1 = mod_ref[2:3, :]
    shift2 = mod_ref[3:4, :]
    scale2 = mod_ref[4:5, :]
    gate2 = mod_ref[5:6, :]
    ff = w1_ref.shape[1]
    sub = x_ref.shape[0] // n_sub

    def out_proj(i):
        rs = slice(i * sub, (i + 1) * sub)
        return x_ref[rs, :] + gate1 * jnp.dot(y_ref[rs, :], wo_ref[...], preferred_element_type=F32)

    def mlp(x1):
        h = _norm_mod(x1, g_ref[...], shift2, scale2).astype(BF16)
        acc = None
        for f in range(0, ff, ff_chunk):
            a = jnp.dot(h, w1_ref[:, f:f + ff_chunk], preferred_element_type=F32)
            a = jnp.maximum(a, 0.0)
            a = (a * a).astype(BF16)
            part = jnp.dot(a, w2_ref[f:f + ff_chunk, :], preferred_element_type=F32)
            acc = part if acc is None else acc + part
        return x1 + gate2 * acc

    def finish(i, x2):
        if final:
            inv = lax.rsqrt(jnp.mean(x2 * x2, axis=-1, keepdims=True) + RMS_EPS)
            x2 = (x2 * inv) * fg_ref[...]
        o_ref[i * sub:(i + 1) * sub, :] = x2

    x1_next = out_proj(0)
    for i in range(n_sub):
        x1 = x1_next
        if i + 1 < n_sub:
            x1_next = out_proj(i + 1)
        finish(i, mlp(x1))


def _post_call(x, y, mod, g, w_out, w1, w2, final_g, layer, mixer, tm, ff_chunk=1024, n_sub=2):
    b, l, d = x.shape
    final = final_g is not None
    kern = functools.partial(_post_kernel, ff_chunk=ff_chunk, final=final, n_sub=n_sub)
    in_specs = [
        pl.BlockSpec((None, tm, d), lambda i, t: (i, t, 0)),
        pl.BlockSpec((None, tm, d), lambda i, t: (i, t, 0)),
        _mod_spec(mod, layer),
        _layer_spec(g, layer),
        _layer_spec(w_out, mixer),
        _layer_spec(w1, layer),
        _layer_spec(w2, layer),
    ]
    args = [x, y, mod, g, w_out, w1, w2]
    if final:
        in_specs.append(_layer_spec(final_g, 0))
        args.append(final_g)
    return pl.pallas_call(
        kern,
        grid=(b, l // tm),
        in_specs=in_specs,
        out_specs=pl.BlockSpec((None, tm, d), lambda i, t: (i, t, 0)),
        out_shape=jax.ShapeDtypeStruct((b, l, d), F32),
        compiler_params=pltpu.CompilerParams(
            dimension_semantics=("arbitrary", "arbitrary"), vmem_limit_bytes=VMEM_LIMIT),
    )(*args)


def _key_row_start(g, rows):
    return min(max(Q_ROWS * g - WIN_ROWS // 2, 0), rows - K_ROWS)


def _key_col_start(cb):
    lo = min(max(Q_COLS * cb - WIN_COLS // 2, 0), GRID_W - WIN_COLS)
    hi = min(max(Q_COLS * (cb + 1) - 1 - WIN_COLS // 2, 0), GRID_W - WIN_COLS) + WIN_COLS
    start = min(lo // HALO * HALO, GRID_W - K_COLS)
    assert start <= lo and hi <= start + K_COLS
    return start


def _window_row(r, rows):
    return min(max(r - WIN_ROWS // 2, 0), rows - WIN_ROWS)


def _visible_tiles(g, blk, rows):
    rel = [_window_row(Q_ROWS * g + BLOCK_ROWS * blk + i, rows) - _key_row_start(g, rows)
           for i in range(BLOCK_ROWS)]
    return min(rel) // ROWS_PER_TILE, -(-(max(rel) + WIN_ROWS) // ROWS_PER_TILE)


def _rpb_lane_rows(rpb):
    nc = rpb.shape[-1]
    placed = [jnp.pad(rpb, ((0, 0), (0, 0), (K_COLS * j, 128 - K_COLS * j - nc))) for j in range(ROWS_PER_TILE)]
    return jnp.stack(placed, axis=2)


def _fill_bias(rv_ref, bias_ref, *, rows):
    n_groups = rows // Q_ROWS
    shape = (Q_COLS, 128)
    qcl = lax.broadcasted_iota(jnp.int32, shape, 0)
    lane = lax.broadcasted_iota(jnp.int32, shape, 1)
    t = lane & (K_COLS - 1)
    neg = jnp.full(shape, NEG_INF, F32)
    for cls, g in enumerate((0, 1, n_groups - 1)):
        kr0 = _key_row_start(g, rows)
        for cb in range(GRID_W // Q_COLS):
            kstart = _key_col_start(cb)
            cs = jnp.clip(qcl + (Q_COLS * cb - WIN_COLS // 2), 0, GRID_W - WIN_COLS) - kstart
            col_ok = (t >= cs) & (t < cs + WIN_COLS)
            delta = kstart - Q_COLS * cb + WIN_COLS - 1
            for hh in range(HEADS_PER_STEP):
                for qr in range(Q_ROWS):
                    r = Q_ROWS * g + qr
                    rs = _window_row(r, rows)
                    t_lo, t_hi = _visible_tiles(g, qr // BLOCK_ROWS, rows)
                    for tile in range(t_lo, t_hi):
                        krow = [kr0 + ROWS_PER_TILE * tile + j for j in range(ROWS_PER_TILE)]
                        seen = [j for j in range(ROWS_PER_TILE) if rs <= krow[j] < rs + WIN_ROWS]
                        dst = (cls, hh, cb, slice(qr * Q_COLS, (qr + 1) * Q_COLS),
                               slice(tile * 128, (tile + 1) * 128))
                        if not seen:
                            bias_ref[dst] = neg
                            continue
                        v = None
                        for j in range(ROWS_PER_TILE):
                            dr = min(max(krow[j] - r + WIN_ROWS - 1, 0), 2 * WIN_ROWS - 2)
                            piece = rv_ref[hh, dr, j:j + 1, :]
                            v = piece if v is None else v + piece
                        rolled = pltpu.roll(jnp.broadcast_to(v * LOG2E, shape), (-delta) % 128, axis=1,
                                            stride=1, stride_axis=0)
                        ok = col_ok & (lane >= K_COLS * seen[0]) & (lane < K_COLS * (seen[-1] + 1))
                        bias_ref[dst] = jnp.where(ok, rolled, neg)


def _attn_kernel(q_ref, k_ref, v_ref, kc_ref, vc_ref, rv_ref, o_ref, bias_ref, kf_ref, vf_ref, *,
                 rows, head_dim):
    n_batch = q_ref.shape[0]
    n_groups = rows // Q_ROWS
    n_cb = GRID_W // Q_COLS
    nq = Q_ROWS * Q_COLS
    nk = K_ROWS * K_COLS

    @pl.when(pl.program_id(1) == 0)
    def _():
        _fill_bias(rv_ref, bias_ref, rows=rows)

    lane = lax.broadcasted_iota(jnp.int32, (1, HEADS_PER_STEP * head_dim), 1)
    nt = (((1,), (1,)), ((), ()))

    converted = {}

    def window(src_ref, dst_ref, bi, g, cb):
        r0 = _key_row_start(g, rows)
        c0 = _key_col_start(cb)
        done = converted.get((id(dst_ref), bi), 0)
        if done < r0 + K_ROWS:
            tok = slice(done * GRID_W, (r0 + K_ROWS) * GRID_W)
            dst_ref[bi, tok, :] = src_ref[bi, tok, :].astype(F32)
            converted[(id(dst_ref), bi)] = r0 + K_ROWS
        pieces = [dst_ref[bi, (r0 + j) * GRID_W + c0:(r0 + j) * GRID_W + c0 + K_COLS, :]
                  for j in range(K_ROWS)]
        return jnp.concatenate(pieces, axis=0).astype(BF16)

    def scores(bi, g, cb):
        pieces = [q_ref[bi, (Q_ROWS * g + qr) * GRID_W + Q_COLS * cb:
                        (Q_ROWS * g + qr) * GRID_W + Q_COLS * (cb + 1), :] for qr in range(Q_ROWS)]
        q = jnp.concatenate(pieces, axis=0)
        zero = jnp.zeros_like(q)
        qh = jnp.concatenate(
            [jnp.where((lane >= hh * head_dim) & (lane < (hh + 1) * head_dim), q, zero)
             for hh in range(HEADS_PER_STEP)], axis=0)
        s_loc = lax.dot_general(qh, window(k_ref, kf_ref, bi, g, cb), nt, preferred_element_type=F32)
        s_ctx = lax.dot_general(qh, kc_ref[bi], nt, preferred_element_type=F32)
        return s_loc, s_ctx

    def attend(bi, g, cb, s_loc, s_ctx):
        cls = 0 if g == 0 else (2 if g == n_groups - 1 else 1)
        p_blocks = []
        for hh in range(HEADS_PER_STEP):
            for blk in range(Q_ROWS // BLOCK_ROWS):
                t_lo, t_hi = _visible_tiles(g, blk, rows)
                nb = BLOCK_ROWS * Q_COLS
                qsl = slice(blk * nb, (blk + 1) * nb)
                rsl = slice(hh * nq + qsl.start, hh * nq + qsl.stop)
                tiles = [s_loc[rsl, t * 128:(t + 1) * 128] + bias_ref[cls, hh, cb, qsl, t * 128:(t + 1) * 128]
                         for t in range(t_lo, t_hi)]
                tiles += [s_ctx[rsl, t * 128:(t + 1) * 128] for t in range(s_ctx.shape[1] // 128)]
                m = jnp.max(functools.reduce(jnp.maximum, tiles), axis=-1, keepdims=True)
                probs = [jnp.exp2(t - m).astype(BF16) for t in tiles]
                n_loc = t_hi - t_lo
                pieces = [jnp.zeros((nb, 128), BF16)] * t_lo + probs[:n_loc]
                pieces += [jnp.zeros((nb, 128), BF16)] * (nk // 128 - t_hi) + probs[n_loc:]
                p_blocks.append(jnp.concatenate(pieces, axis=1))
        p = jnp.concatenate(p_blocks, axis=0)
        vw = jnp.concatenate([window(v_ref, vf_ref, bi, g, cb), jnp.ones((nk, 128), BF16)], axis=1)
        vc = jnp.concatenate([vc_ref[bi], jnp.ones((vc_ref.shape[1], 128), BF16)], axis=1)
        o = (jnp.dot(p[:, :nk], vw, preferred_element_type=F32)
             + jnp.dot(p[:, nk:], vc, preferred_element_type=F32))
        o = o[:, :128] / o[:, 128:]
        out = o[(HEADS_PER_STEP - 1) * nq:]
        for hh in range(HEADS_PER_STEP - 2, -1, -1):
            out = jnp.where(lane < (hh + 1) * head_dim, o[hh * nq:(hh + 1) * nq], out)
        out = out.astype(BF16)
        for qr in range(Q_ROWS):
            tok = (Q_ROWS * g + qr) * GRID_W + Q_COLS * cb
            o_ref[bi, tok:tok + Q_COLS, :] = out[qr * Q_COLS:(qr + 1) * Q_COLS]

    units = [(bi, g, cb) for bi in range(n_batch) for g in range(n_groups) for cb in range(n_cb)]
    nxt = scores(*units[0])
    for u, unit in enumerate(units):
        cur = nxt
        if u + 1 < len(units):
            nxt = scores(*units[u + 1])
        attend(*unit, *cur)


def _attn_call(qkv, ckv, rpb_rows):
    b, s, d3 = qkv.shape
    d = d3 // 3
    ctx_len = ckv.shape[1]
    head_dim = d // N_HEADS
    w = HEADS_PER_STEP * head_dim
    assert w == 128
    n_hp = d // w
    rows = s // GRID_W
    n_dr = rpb_rows.shape[1]
    bb = ATTN_BATCH if b % ATTN_BATCH == 0 else 1
    kern = functools.partial(_attn_kernel, rows=rows, head_dim=head_dim)
    return pl.pallas_call(
        kern,
        grid=(n_hp, b // bb),
        in_specs=[
            pl.BlockSpec((bb, s, w), lambda h, i: (i, 0, h)),
            pl.BlockSpec((bb, s, w), lambda h, i: (i, 0, n_hp + h)),
            pl.BlockSpec((bb, s, w), lambda h, i: (i, 0, 2 * n_hp + h)),
            pl.BlockSpec((bb, ctx_len, w), lambda h, i: (i, 0, h)),
            pl.BlockSpec((bb, ctx_len, w), lambda h, i: (i, 0, n_hp + h)),
            pl.BlockSpec((HEADS_PER_STEP, n_dr, ROWS_PER_TILE, 128), lambda h, i: (h, 0, 0, 0)),
        ],
        out_specs=pl.BlockSpec((bb, s, w), lambda h, i: (i, 0, h)),
        out_shape=jax.ShapeDtypeStruct((b, s, d), BF16),
        scratch_shapes=[
            pltpu.VMEM((3, HEADS_PER_STEP, GRID_W // Q_COLS, Q_ROWS * Q_COLS, K_ROWS * K_COLS), F32),
            pltpu.VMEM((bb, s, w), F32),
            pltpu.VMEM((bb, s, w), F32),
        ],
        compiler_params=pltpu.CompilerParams(
            dimension_semantics=("arbitrary", "arbitrary"), vmem_limit_bytes=VMEM_LIMIT),
    )(qkv, qkv, qkv, ckv, ckv, rpb_rows)


def kernel(x, c, ctx, c_ctx, norm1_g, norm2_g, ada_w, ada_b, conv_in_w, conv_w, conv_out_w,
           attn_qkv_w, attn_rpb, attn_out_w, mlp_w1, mlp_w2, final_g):
    b, s, d = x.shape
    ctx_len = ctx.shape[1]
    depth = ada_w.shape[0]
    assert depth == 2 and s % (GRID_W * Q_ROWS) == 0 and s // GRID_W >= K_ROWS and d % N_HEADS == 0
    head_dim = d // N_HEADS
    tm_pre, tm_post, tmc = 1024, 1024, ctx_len

    pad = (-(b + 1)) % 8
    cond = jnp.concatenate([c, c_ctx[None, :], jnp.zeros((pad, d), F32)], axis=0)
    mods = _ada_call(cond, ada_w, ada_b).reshape(depth, b + 1 + pad, 6, d)
    mod_lat = mods[:, :b]
    mod_ctx = mods[:, b:b + 1]

    n1 = norm1_g.reshape(depth, 1, d)
    n2 = norm2_g.reshape(depth, 1, d)
    fg = final_g.reshape(1, 1, d)
    w1 = mlp_w1.astype(BF16)
    w2 = mlp_w2.astype(BF16)
    w_in = conv_in_w.astype(BF16)
    w_co = conv_out_w.astype(BF16)
    w_qkv = attn_qkv_w.astype(BF16)
    w_ao = attn_out_w.astype(BF16)

    y = _conv_pre_call(x, mod_lat, n1, w_in, conv_w, 0, 0, tm_pre, 2)
    yc = _conv_pre_call(ctx, mod_ctx, n1, w_in, conv_w, 0, 0, tmc, 1)
    x = _post_call(x, y, mod_lat, n2, w_co, w1, w2, None, 0, 0, tm_post, n_sub=4)
    ctx = _post_call(ctx, yc, mod_ctx, n2, w_co, w1, w2, None, 0, 0, tmc, n_sub=1)

    qkv = _proj_pre_call(x, mod_lat, n1, w_qkv, 1, 0, tm_pre, 0, d, head_dim ** -0.5 * LOG2E, 2)
    ckv = _proj_pre_call(ctx, mod_ctx, n1, w_qkv, 1, 0, tmc, d, 0, 1.0, 1)
    o = _attn_call(qkv, ckv, _rpb_lane_rows(attn_rpb[0]))
    return _post_call(x, o, mod_lat, n2, w_ao, w1, w2, fg, 1, 0, tm_post, n_sub=4)
```

```python
import functools

import jax
import jax.numpy as jnp
from jax import lax
from jax.experimental import pallas as pl
from jax.experimental.pallas import tpu as pltpu

GRID_W = 64
N_HEADS = 16
WIN_ROWS = 8
WIN_COLS = 16
RMS_EPS = 1e-6
NEG_INF = -1e30
LOG2E = 1.4426950408889634

HALO = 8
HEADS_PER_STEP = 2
Q_ROWS, Q_COLS = 8, 16
K_ROWS, K_COLS = 16, 32
ROWS_PER_TILE = 128 // K_COLS
BLOCK_ROWS = 4
ATTN_BATCH = 2
VMEM_LIMIT = 56 * 1024 * 1024

BF16 = jnp.bfloat16
F32 = jnp.float32


def _norm_mod(v, g, shift, scale):
    inv = lax.rsqrt(jnp.mean(v * v, axis=-1, keepdims=True) + RMS_EPS)
    return ((v * inv) * g) * (1.0 + scale) + shift


def _ada_kernel(cond_ref, w_ref, b_ref, o_ref):
    cnd = cond_ref[...]
    s = (cnd / (1.0 + jnp.exp(-cnd))).astype(BF16)
    o_ref[...] = jnp.dot(s, w_ref[...].astype(BF16), preferred_element_type=F32) + b_ref[...]


def _ada_call(cond, ada_w, ada_b):
    depth, d, n = ada_w.shape
    rows = cond.shape[0]
    bn = d
    return pl.pallas_call(
        _ada_kernel,
        grid=(depth, n // bn),
        in_specs=[
            pl.BlockSpec((rows, d), lambda i, j: (0, 0)),
            pl.BlockSpec((None, d, bn), lambda i, j: (i, 0, j)),
            pl.BlockSpec((None, 1, bn), lambda i, j: (i, 0, j)),
        ],
        out_specs=pl.BlockSpec((None, rows, bn), lambda i, j: (i, 0, j)),
        out_shape=jax.ShapeDtypeStruct((depth, rows, n), F32),
        compiler_params=pltpu.CompilerParams(
            dimension_semantics=("arbitrary", "arbitrary"), vmem_limit_bytes=VMEM_LIMIT),
    )(cond, ada_w, ada_b.reshape(depth, 1, n))


def _mod_spec(mod, layer):
    if mod.shape[1] == 1:
        return pl.BlockSpec((None, None, 6, mod.shape[3]), lambda b, t: (layer, 0, 0, 0))
    return pl.BlockSpec((None, None, 6, mod.shape[3]), lambda b, t: (layer, b, 0, 0))


def _layer_spec(arr, layer):
    nd = arr.ndim - 1
    return pl.BlockSpec((None,) + arr.shape[1:], lambda b, t: (layer,) + (0,) * nd,
                        pipeline_mode=pl.Buffered(1))


def _tile_spec(tm, d, nt):
    return pl.BlockSpec((None, tm, d), lambda i, t: (i, jnp.minimum(t, nt - 1), 0))


def _ctx_spec(rows, d):
    return pl.BlockSpec((None, rows, d), lambda i, t: (i, 0, 0))


def _conv_tile(xe, mod_ref, g_ref, w_ref, cw_ref, o_ref, *, n_sub, pad_first, pad_last):
    d = xe.shape[1]
    shift = mod_ref[0:1, :]
    scale = mod_ref[1:2, :]
    sub = (xe.shape[0] - 2 * HALO) // n_sub
    n_ext = sub + 2 * HALO
    row = lax.broadcasted_iota(jnp.int32, (n_ext, 1), 0)

    def project(i):
        he = _norm_mod(xe[i * sub:i * sub + n_ext], g_ref[...], shift, scale).astype(BF16)
        return he, jnp.dot(he, w_ref[:, d:], preferred_element_type=F32)

    def gate_conv(i, he, cv):
        u = cv[:, :d] * cv[:, d:]
        if i == 0:
            u = jnp.where((row >= HALO) | jnp.logical_not(pad_first), u, 0.0)
        if i == n_sub - 1:
            u = jnp.where((row < sub + HALO) | jnp.logical_not(pad_last), u, 0.0)
        u_prev = pltpu.roll(u, 1, axis=0)[HALO:HALO + sub]
        u_next = pltpu.roll(u, n_ext - 1, axis=0)[HALO:HALO + sub]
        conv = u_prev * cw_ref[0:1, :] + u[HALO:HALO + sub] * cw_ref[1:2, :] + u_next * cw_ref[2:3, :]
        bg = jnp.dot(he[HALO:HALO + sub], w_ref[:, :d], preferred_element_type=F32)
        o_ref[i * sub:(i + 1) * sub, :] = (bg * conv).astype(BF16)

    projected = [project(i) for i in range(n_sub)]
    for i, (he, cv) in enumerate(projected):
        gate_conv(i, he, cv)


def _conv_pre_kernel(x_ref, xp_ref, xn_ref, c_ref, modx_ref, modc_ref, g_ref, w_ref, cw_ref, y_ref, yc_ref, *,
                     nt, n_sub):
    t = pl.program_id(1)

    @pl.when(t < nt)
    def _():
        xe = jnp.concatenate([xp_ref[...], x_ref[...], xn_ref[...]], axis=0)
        _conv_tile(xe, modx_ref, g_ref, w_ref, cw_ref, y_ref, n_sub=n_sub,
                   pad_first=t == 0, pad_last=t == nt - 1)

    @pl.when(t == nt)
    def _():
        z = jnp.zeros((HALO, c_ref.shape[1]), F32)
        xe = jnp.concatenate([z, c_ref[...], z], axis=0)
        _conv_tile(xe, modc_ref, g_ref, w_ref, cw_ref, yc_ref, n_sub=1, pad_first=True, pad_last=True)


def _conv_pre_call(x, ctx, mod_x, mod_c, g, w_in, conv_w, layer, mixer, tm, n_sub):
    b, l, d = x.shape
    lc = ctx.shape[1]
    nt = l // tm
    hb = tm // HALO
    last_hb = l // HALO - 1
    kern = functools.partial(_conv_pre_kernel, nt=nt, n_sub=n_sub)

    def prev_halo(i, t):
        return (i, jnp.maximum(jnp.minimum(t, nt - 1) * hb - 1, 0), 0)

    def next_halo(i, t):
        return (i, jnp.minimum((jnp.minimum(t, nt - 1) + 1) * hb, last_hb), 0)

    return pl.pallas_call(
        kern,
        grid=(b, nt + 1),
        in_specs=[
            _tile_spec(tm, d, nt),
            pl.BlockSpec((None, HALO, d), prev_halo),
            pl.BlockSpec((None, HALO, d), next_halo),
            _ctx_spec(lc, d),
            _mod_spec(mod_x, layer),
            _mod_spec(mod_c, layer),
            _layer_spec(g, layer),
            _layer_spec(w_in, mixer),
            _layer_spec(conv_w, mixer),
        ],
        out_specs=[_tile_spec(tm, d, nt), _ctx_spec(lc, d)],
        out_shape=[jax.ShapeDtypeStruct((b, l, d), BF16), jax.ShapeDtypeStruct((b, lc, d), BF16)],
        compiler_params=pltpu.CompilerParams(
            dimension_semantics=("arbitrary", "arbitrary"), vmem_limit_bytes=VMEM_LIMIT),
    )(x, x, x, ctx, mod_x, mod_c, g, w_in, conv_w)


def _proj_tile(x_ref, mod_ref, g_ref, w_ref, o_ref, *, col0, n_scaled, q_scale, n_sub):
    sub = x_ref.shape[0] // n_sub
    for i in range(n_sub):
        rs = slice(i * sub, (i + 1) * sub)
        h = _norm_mod(x_ref[rs, :], g_ref[...], mod_ref[0:1, :], mod_ref[1:2, :]).astype(BF16)
        if n_scaled:
            o_ref[rs, :n_scaled] = (
                jnp.dot(h, w_ref[:, col0:col0 + n_scaled], preferred_element_type=F32) * q_scale).astype(BF16)
        o_ref[rs, n_scaled:] = jnp.dot(h, w_ref[:, col0 + n_scaled:], preferred_element_type=F32).astype(BF16)


def _proj_pre_kernel(x_ref, c_ref, modx_ref, modc_ref, g_ref, w_ref, qkv_ref, ckv_ref, *, nt, d, q_scale, n_sub):
    t = pl.program_id(1)

    @pl.when(t < nt)
    def _():
        _proj_tile(x_ref, modx_ref, g_ref, w_ref, qkv_ref, col0=0, n_scaled=d, q_scale=q_scale, n_sub=n_sub)

    @pl.when(t == nt)
    def _():
        _proj_tile(c_ref, modc_ref, g_ref, w_ref, ckv_ref, col0=d, n_scaled=0, q_scale=1.0, n_sub=1)


def _proj_pre_call(x, ctx, mod_x, mod_c, g, w, layer, mixer, tm, q_scale, n_sub):
    b, l, d = x.shape
    lc = ctx.shape[1]
    n = w.shape[2]
    nt = l // tm
    kern = functools.partial(_proj_pre_kernel, nt=nt, d=d, q_scale=q_scale, n_sub=n_sub)
    return pl.pallas_call(
        kern,
        grid=(b, nt + 1),
        in_specs=[
            _tile_spec(tm, d, nt),
            _ctx_spec(lc, d),
            _mod_spec(mod_x, layer),
            _mod_spec(mod_c, layer),
            _layer_spec(g, layer),
            _layer_spec(w, mixer),
        ],
        out_specs=[_tile_spec(tm, n, nt), _ctx_spec(lc, n - d)],
        out_shape=[jax.ShapeDtypeStruct((b, l, n), BF16), jax.ShapeDtypeStruct((b, lc, n - d), BF16)],
        compiler_params=pltpu.CompilerParams(
            dimension_semantics=("arbitrary", "arbitrary"), vmem_limit_bytes=VMEM_LIMIT),
    )(x, ctx, mod_x, mod_c, g, w)


def _post_tile(x_ref, y_ref, mod_ref, g_ref, wo_ref, w1_ref, w2_ref, fg_ref, o_ref, *, ff_chunk, n_sub):
    gate1 = mod_ref[2:3, :]
    shift2 = mod_ref[3:4, :]
    scale2 = mod_ref[4:5, :]
    gate2 = mod_ref[5:6, :]
    ff = w1_ref.shape[1]
    sub = x_ref.shape[0] // n_sub

    def out_proj(i):
        rs = slice(i * sub, (i + 1) * sub)
        return x_ref[rs, :] + gate1 * jnp.dot(y_ref[rs, :], wo_ref[...], preferred_element_type=F32)

    def mlp(x1):
        h = _norm_mod(x1, g_ref[...], shift2, scale2).astype(BF16)
        acc = None
        for f in range(0, ff, ff_chunk):
            a = jnp.dot(h, w1_ref[:, f:f + ff_chunk], preferred_element_type=F32)
            a = jnp.maximum(a, 0.0)
            a = (a * a).astype(BF16)
            part = jnp.dot(a, w2_ref[f:f + ff_chunk, :], preferred_element_type=F32)
            acc = part if acc is None else acc + part
        return x1 + gate2 * acc

    def finish(i, x2):
        if fg_ref is not None:
            inv = lax.rsqrt(jnp.mean(x2 * x2, axis=-1, keepdims=True) + RMS_EPS)
            x2 = (x2 * inv) * fg_ref[...]
        o_ref[i * sub:(i + 1) * sub, :] = x2

    x1_next = out_proj(0)
    for i in range(n_sub):
        x1 = x1_next
        if i + 1 < n_sub:
            x1_next = out_proj(i + 1)
        finish(i, mlp(x1))


def _post_kernel(*refs, nt, ff_chunk, n_sub, with_ctx, final):
    refs = list(refs)
    x_ref, y_ref = refs[:2]
    del refs[:2]
    if with_ctx:
        c_ref, yc_ref = refs[:2]
        del refs[:2]
    modx_ref = refs.pop(0)
    modc_ref = refs.pop(0) if with_ctx else None
    g_ref, wo_ref, w1_ref, w2_ref = refs[:4]
    del refs[:4]
    fg_ref = refs.pop(0) if final else None
    o_ref = refs.pop(0)
    oc_ref = refs.pop(0) if with_ctx else None
    t = pl.program_id(1)

    @pl.when(t < nt)
    def _():
        _post_tile(x_ref, y_ref, modx_ref, g_ref, wo_ref, w1_ref, w2_ref, fg_ref, o_ref,
                   ff_chunk=ff_chunk, n_sub=n_sub)

    if with_ctx:
        @pl.when(t == nt)
        def _():
            _post_tile(c_ref, yc_ref, modc_ref, g_ref, wo_ref, w1_ref, w2_ref, fg_ref, oc_ref,
                       ff_chunk=ff_chunk, n_sub=1)


def _post_call(x, y, ctx, yc, mod_x, mod_c, g, w_out, w1, w2, final_g, layer, mixer, tm, n_sub, ff_chunk=1024):
    b, l, d = x.shape
    nt = l // tm
    with_ctx = ctx is not None
    final = final_g is not None
    kern = functools.partial(_post_kernel, nt=nt, ff_chunk=ff_chunk, n_sub=n_sub, with_ctx=with_ctx, final=final)
    in_specs = [_tile_spec(tm, d, nt), _tile_spec(tm, d, nt)]
    args = [x, y]
    out_specs = [_tile_spec(tm, d, nt)]
    out_shape = [jax.ShapeDtypeStruct((b, l, d), F32)]
    if with_ctx:
        lc = ctx.shape[1]
        in_specs += [_ctx_spec(lc, d), _ctx_spec(lc, d)]
        args += [ctx, yc]
        out_specs.append(_ctx_spec(lc, d))
        out_shape.append(jax.ShapeDtypeStruct((b, lc, d), F32))
    in_specs.append(_mod_spec(mod_x, layer))
    args.append(mod_x)
    if with_ctx:
        in_specs.append(_mod_spec(mod_c, layer))
        args.append(mod_c)
    in_specs += [_layer_spec(g, layer), _layer_spec(w_out, mixer), _layer_spec(w1, layer), _layer_spec(w2, layer)]
    args += [g, w_out, w1, w2]
    if final:
        in_specs.append(_layer_spec(final_g, 0))
        args.append(final_g)
    return pl.pallas_call(
        kern,
        grid=(b, nt + (1 if with_ctx else 0)),
        in_specs=in_specs,
        out_specs=out_specs,
        out_shape=out_shape,
        compiler_params=pltpu.CompilerParams(
            dimension_semantics=("arbitrary", "arbitrary"), vmem_limit_bytes=VMEM_LIMIT),
    )(*args)


def _key_row_start(g, rows):
    return min(max(Q_ROWS * g - WIN_ROWS // 2, 0), rows - K_ROWS)


def _key_col_start(cb):
    lo = min(max(Q_COLS * cb - WIN_COLS // 2, 0), GRID_W - WIN_COLS)
    hi = min(max(Q_COLS * (cb + 1) - 1 - WIN_COLS // 2, 0), GRID_W - WIN_COLS) + WIN_COLS
    start = min(lo // HALO * HALO, GRID_W - K_COLS)
    assert start <= lo and hi <= start + K_COLS
    return start


def _window_row(r, rows):
    return min(max(r - WIN_ROWS // 2, 0), rows - WIN_ROWS)


def _visible_tiles(g, blk, rows):
    rel = [_window_row(Q_ROWS * g + BLOCK_ROWS * blk + i, rows) - _key_row_start(g, rows)
           for i in range(BLOCK_ROWS)]
    return min(rel) // ROWS_PER_TILE, -(-(max(rel) + WIN_ROWS) // ROWS_PER_TILE)


def _rpb_lane_rows(rpb):
    nc = rpb.shape[-1]
    placed = [jnp.pad(rpb, ((0, 0), (0, 0), (K_COLS * j, 128 - K_COLS * j - nc))) for j in range(ROWS_PER_TILE)]
    return jnp.stack(placed, axis=2)


def _fill_bias(rv_ref, bias_ref, *, rows):
    n_groups = rows // Q_ROWS
    shape = (Q_COLS, 128)
    qcl = lax.broadcasted_iota(jnp.int32, shape, 0)
    lane = lax.broadcasted_iota(jnp.int32, shape, 1)
    t = lane & (K_COLS - 1)
    neg = jnp.full(shape, NEG_INF, F32)
    for cls, g in enumerate((0, 1, n_groups - 1)):
        kr0 = _key_row_start(g, rows)
        for cb in range(GRID_W // Q_COLS):
            kstart = _key_col_start(cb)
            cs = jnp.clip(qcl + (Q_COLS * cb - WIN_COLS // 2), 0, GRID_W - WIN_COLS) - kstart
            col_ok = (t >= cs) & (t < cs + WIN_COLS)
            delta = kstart - Q_COLS * cb + WIN_COLS - 1
            for hh in range(HEADS_PER_STEP):
                for qr in range(Q_ROWS):
                    r = Q_ROWS * g + qr
                    rs = _window_row(r, rows)
                    t_lo, t_hi = _visible_tiles(g, qr // BLOCK_ROWS, rows)
                    for tile in range(t_lo, t_hi):
                        krow = [kr0 + ROWS_PER_TILE * tile + j for j in range(ROWS_PER_TILE)]
                        seen = [j for j in range(ROWS_PER_TILE) if rs <= krow[j] < rs + WIN_ROWS]
                        dst = (cls, hh, cb, slice(qr * Q_COLS, (qr + 1) * Q_COLS),
                               slice(tile * 128, (tile + 1) * 128))
                        if not seen:
                            bias_ref[dst] = neg
                            continue
                        v = None
                        for j in range(ROWS_PER_TILE):
                            dr = min(max(krow[j] - r + WIN_ROWS - 1, 0), 2 * WIN_ROWS - 2)
                            piece = rv_ref[hh, dr, j:j + 1, :]
                            v = piece if v is None else v + piece
                        rolled = pltpu.roll(jnp.broadcast_to(v * LOG2E, shape), (-delta) % 128, axis=1,
                                            stride=1, stride_axis=0)
                        ok = col_ok & (lane >= K_COLS * seen[0]) & (lane < K_COLS * (seen[-1] + 1))
                        bias_ref[dst] = jnp.where(ok, rolled, neg)


def _attn_kernel(q_ref, k_ref, v_ref, kc_ref, vc_ref, rv_ref, o_ref, bias_ref, kf_ref, vf_ref, *,
                 rows, head_dim):
    n_batch = q_ref.shape[0]
    n_groups = rows // Q_ROWS
    n_cb = GRID_W // Q_COLS
    nq = Q_ROWS * Q_COLS
    nk = K_ROWS * K_COLS

    @pl.when(pl.program_id(1) == 0)
    def _():
        _fill_bias(rv_ref, bias_ref, rows=rows)

    lane = lax.broadcasted_iota(jnp.int32, (1, HEADS_PER_STEP * head_dim), 1)
    nt = (((1,), (1,)), ((), ()))

    converted = {}

    def window(src_ref, dst_ref, bi, g, cb):
        r0 = _key_row_start(g, rows)
        c0 = _key_col_start(cb)
        done = converted.get((id(dst_ref), bi), 0)
        if done < r0 + K_ROWS:
            tok = slice(done * GRID_W, (r0 + K_ROWS) * GRID_W)
            dst_ref[bi, tok, :] = src_ref[bi, tok, :].astype(F32)
            converted[(id(dst_ref), bi)] = r0 + K_ROWS
        pieces = [dst_ref[bi, (r0 + j) * GRID_W + c0:(r0 + j) * GRID_W + c0 + K_COLS, :]
                  for j in range(K_ROWS)]
        return jnp.concatenate(pieces, axis=0).astype(BF16)

    def scores(bi, g, cb):
        pieces = [q_ref[bi, (Q_ROWS * g + qr) * GRID_W + Q_COLS * cb:
                        (Q_ROWS * g + qr) * GRID_W + Q_COLS * (cb + 1), :] for qr in range(Q_ROWS)]
        q = jnp.concatenate(pieces, axis=0)
        zero = jnp.zeros_like(q)
        qh = jnp.concatenate(
            [jnp.where((lane >= hh * head_dim) & (lane < (hh + 1) * head_dim), q, zero)
             for hh in range(HEADS_PER_STEP)], axis=0)
        s_loc = lax.dot_general(qh, window(k_ref, kf_ref, bi, g, cb), nt, preferred_element_type=F32)
        s_ctx = lax.dot_general(qh, kc_ref[bi], nt, preferred_element_type=F32)
        return s_loc, s_ctx

    def attend(bi, g, cb, s_loc, s_ctx):
        cls = 0 if g == 0 else (2 if g == n_groups - 1 else 1)
        p_blocks = []
        for hh in range(HEADS_PER_STEP):
            for blk in range(Q_ROWS // BLOCK_ROWS):
                t_lo, t_hi = _visible_tiles(g, blk, rows)
                nb = BLOCK_ROWS * Q_COLS
                qsl = slice(blk * nb, (blk + 1) * nb)
                rsl = slice(hh * nq + qsl.start, hh * nq + qsl.stop)
                tiles = [s_loc[rsl, t * 128:(t + 1) * 128] + bias_ref[cls, hh, cb, qsl, t * 128:(t + 1) * 128]
                         for t in range(t_lo, t_hi)]
                tiles += [s_ctx[rsl, t * 128:(t + 1) * 128] for t in range(s_ctx.shape[1] // 128)]
                m = jnp.max(functools.reduce(jnp.maximum, tiles), axis=-1, keepdims=True)
                probs = [jnp.exp2(t - m).astype(BF16) for t in tiles]
                n_loc = t_hi - t_lo
                pieces = [jnp.zeros((nb, 128), BF16)] * t_lo + probs[:n_loc]
                pieces += [jnp.zeros((nb, 128), BF16)] * (nk // 128 - t_hi) + probs[n_loc:]
                p_blocks.append(jnp.concatenate(pieces, axis=1))
        p = jnp.concatenate(p_blocks, axis=0)
        vw = jnp.concatenate([window(v_ref, vf_ref, bi, g, cb), jnp.ones((nk, 128), BF16)], axis=1)
        vc = jnp.concatenate([vc_ref[bi], jnp.ones((vc_ref.shape[1], 128), BF16)], axis=1)
        o = (jnp.dot(p[:, :nk], vw, preferred_element_type=F32)
             + jnp.dot(p[:, nk:], vc, preferred_element_type=F32))
        o = o[:, :128] / o[:, 128:]
        out = o[(HEADS_PER_STEP - 1) * nq:]
        for hh in range(HEADS_PER_STEP - 2, -1, -1):
            out = jnp.where(lane < (hh + 1) * head_dim, o[hh * nq:(hh + 1) * nq], out)
        out = out.astype(BF16)
        for qr in range(Q_ROWS):
            tok = (Q_ROWS * g + qr) * GRID_W + Q_COLS * cb
            o_ref[bi, tok:tok + Q_COLS, :] = out[qr * Q_COLS:(qr + 1) * Q_COLS]

    units = [(bi, g, cb) for bi in range(n_batch) for g in range(n_groups) for cb in range(n_cb)]
    nxt = scores(*units[0])
    for u, unit in enumerate(units):
        cur = nxt
        if u + 1 < len(units):
            nxt = scores(*units[u + 1])
        attend(*unit, *cur)


def _attn_call(qkv, ckv, rpb_rows):
    b, s, d3 = qkv.shape
    d = d3 // 3
    ctx_len = ckv.shape[1]
    head_dim = d // N_HEADS
    w = HEADS_PER_STEP * head_dim
    assert w == 128
    n_hp = d // w
    rows = s // GRID_W
    n_dr = rpb_rows.shape[1]
    bb = ATTN_BATCH if b % ATTN_BATCH == 0 else 1
    kern = functools.partial(_attn_kernel, rows=rows, head_dim=head_dim)
    return pl.pallas_call(
        kern,
        grid=(n_hp, b // bb),
        in_specs=[
            pl.BlockSpec((bb, s, w), lambda h, i: (i, 0, h)),
            pl.BlockSpec((bb, s, w), lambda h, i: (i, 0, n_hp + h)),
            pl.BlockSpec((bb, s, w), lambda h, i: (i, 0, 2 * n_hp + h)),
            pl.BlockSpec((bb, ctx_len, w), lambda h, i: (i, 0, h)),
            pl.BlockSpec((bb, ctx_len, w), lambda h, i: (i, 0, n_hp + h)),
            pl.BlockSpec((HEADS_PER_STEP, n_dr, ROWS_PER_TILE, 128), lambda h, i: (h, 0, 0, 0)),
        ],
        out_specs=pl.BlockSpec((bb, s, w), lambda h, i: (i, 0, h)),
        out_shape=jax.ShapeDtypeStruct((b, s, d), BF16),
        scratch_shapes=[
            pltpu.VMEM((3, HEADS_PER_STEP, GRID_W // Q_COLS, Q_ROWS * Q_COLS, K_ROWS * K_COLS), F32),
            pltpu.VMEM((bb, s, w), F32),
            pltpu.VMEM((bb, s, w), F32),
        ],
        compiler_params=pltpu.CompilerParams(
            dimension_semantics=("arbitrary", "arbitrary"), vmem_limit_bytes=VMEM_LIMIT),
    )(qkv, qkv, qkv, ckv, ckv, rpb_rows)


def kernel(x, c, ctx, c_ctx, norm1_g, norm2_g, ada_w, ada_b, conv_in_w, conv_w, conv_out_w,
           attn_qkv_w, attn_rpb, attn_out_w, mlp_w1, mlp_w2, final_g):
    b, s, d = x.shape
    depth = ada_w.shape[0]
    assert depth == 2 and s % (GRID_W * Q_ROWS) == 0 and s // GRID_W >= K_ROWS and d % N_HEADS == 0
    head_dim = d // N_HEADS
    tm_pre, tm_post = 1024, 1024

    pad = (-(b + 1)) % 8
    cond = jnp.concatenate([c, c_ctx[None, :], jnp.zeros((pad, d), F32)], axis=0)
    mods = _ada_call(cond, ada_w, ada_b).reshape(depth, b + 1 + pad, 6, d)
    mod_lat = mods[:, :b]
    mod_ctx = mods[:, b:b + 1]

    n1 = norm1_g.reshape(depth, 1, d)
    n2 = norm2_g.reshape(depth, 1, d)
    fg = final_g.reshape(1, 1, d)
    w1 = mlp_w1.astype(BF16)
    w2 = mlp_w2.astype(BF16)
    w_in = conv_in_w.astype(BF16)
    w_co = conv_out_w.astype(BF16)
    w_qkv = attn_qkv_w.astype(BF16)
    w_ao = attn_out_w.astype(BF16)

    y, yc = _conv_pre_call(x, ctx, mod_lat, mod_ctx, n1, w_in, conv_w, 0, 0, tm_pre, 2)
    x, ctx = _post_call(x, y, ctx, yc, mod_lat, mod_ctx, n2, w_co, w1, w2, None, 0, 0, tm_post, 4)

    qkv, ckv = _proj_pre_call(x, ctx, mod_lat, mod_ctx, n1, w_qkv, 1, 0, tm_pre, head_dim ** -0.5 * LOG2E, 2)
    o = _attn_call(qkv, ckv, _rpb_lane_rows(attn_rpb[0]))
    (out,) = _post_call(x, o, None, None, mod_lat, None, n2, w_ao, w1, w2, fg, 1, 0, tm_post, 4)
    return out
```

```python
import functools

import jax
import jax.numpy as jnp
from jax import lax
from jax.experimental import pallas as pl
from jax.experimental.pallas import tpu as pltpu

GRID_W = 64
N_HEADS = 16
WIN_ROWS = 8
WIN_COLS = 16
RMS_EPS = 1e-6
NEG_INF = -1e30
LOG2E = 1.4426950408889634

HALO = 8
HEADS_PER_STEP = 2
Q_ROWS, Q_COLS = 8, 16
K_ROWS, K_COLS = 16, 32
ROWS_PER_TILE = 128 // K_COLS
BLOCK_ROWS = 4
ATTN_BATCH = 2
VMEM_LIMIT = 56 * 1024 * 1024

BF16 = jnp.bfloat16
F32 = jnp.float32


def _norm_mod(v, g, shift, scale):
    inv = lax.rsqrt(jnp.mean(v * v, axis=-1, keepdims=True) + RMS_EPS)
    return ((v * inv) * g) * (1.0 + scale) + shift


def _ada_kernel(cond_ref, w_ref, b_ref, o_ref):
    cnd = cond_ref[...]
    s = (cnd / (1.0 + jnp.exp(-cnd))).astype(BF16)
    o_ref[...] = jnp.dot(s, w_ref[...].astype(BF16), preferred_element_type=F32) + b_ref[...]


def _ada_call(cond, ada_w, ada_b):
    depth, d, n = ada_w.shape
    rows = cond.shape[0]
    bn = d
    return pl.pallas_call(
        _ada_kernel,
        grid=(depth, n // bn),
        in_specs=[
            pl.BlockSpec((rows, d), lambda i, j: (0, 0)),
            pl.BlockSpec((None, d, bn), lambda i, j: (i, 0, j)),
            pl.BlockSpec((None, 1, bn), lambda i, j: (i, 0, j)),
        ],
        out_specs=pl.BlockSpec((None, rows, bn), lambda i, j: (i, 0, j)),
        out_shape=jax.ShapeDtypeStruct((depth, rows, n), F32),
        compiler_params=pltpu.CompilerParams(
            dimension_semantics=("arbitrary", "arbitrary"), vmem_limit_bytes=VMEM_LIMIT),
    )(cond, ada_w, ada_b.reshape(depth, 1, n))


def _mod_spec(mod, layer):
    if mod.shape[1] == 1:
        return pl.BlockSpec((None, None, 6, mod.shape[3]), lambda t, b: (layer, 0, 0, 0))
    return pl.BlockSpec((None, None, 6, mod.shape[3]), lambda t, b: (layer, b, 0, 0))


def _layer_spec(arr, layer):
    nd = arr.ndim - 1
    return pl.BlockSpec((None,) + arr.shape[1:], lambda t, b: (layer,) + (0,) * nd,
                        pipeline_mode=pl.Buffered(1))


def _latent_batch(t, b, nt, n_batch):
    return jnp.where(t < nt, b, n_batch - 1)


def _tile_spec(tm, d, nt, n_batch):
    return pl.BlockSpec((None, tm, d), lambda t, b: (_latent_batch(t, b, nt, n_batch), jnp.minimum(t, nt - 1), 0))


def _ctx_spec(rows, d, nt):
    return pl.BlockSpec((None, rows, d), lambda t, b: (jnp.where(t == nt, b, 0), 0, 0))


def _conv_tile(xe, mod_ref, g_ref, w_ref, cw_ref, o_ref, *, n_sub, pad_first, pad_last):
    d = xe.shape[1]
    shift = mod_ref[0:1, :]
    scale = mod_ref[1:2, :]
    sub = (xe.shape[0] - 2 * HALO) // n_sub
    n_ext = sub + 2 * HALO
    row = lax.broadcasted_iota(jnp.int32, (n_ext, 1), 0)

    def project(i):
        he = _norm_mod(xe[i * sub:i * sub + n_ext], g_ref[...], shift, scale).astype(BF16)
        return he, jnp.dot(he, w_ref[:, d:], preferred_element_type=F32)

    def gate_conv(i, he, cv):
        u = cv[:, :d] * cv[:, d:]
        if i == 0:
            u = jnp.where((row >= HALO) | jnp.logical_not(pad_first), u, 0.0)
        if i == n_sub - 1:
            u = jnp.where((row < sub + HALO) | jnp.logical_not(pad_last), u, 0.0)
        u_prev = pltpu.roll(u, 1, axis=0)[HALO:HALO + sub]
        u_next = pltpu.roll(u, n_ext - 1, axis=0)[HALO:HALO + sub]
        conv = u_prev * cw_ref[0:1, :] + u[HALO:HALO + sub] * cw_ref[1:2, :] + u_next * cw_ref[2:3, :]
        bg = jnp.dot(he[HALO:HALO + sub], w_ref[:, :d], preferred_element_type=F32)
        o_ref[i * sub:(i + 1) * sub, :] = (bg * conv).astype(BF16)

    projected = [project(i) for i in range(n_sub)]
    for i, (he, cv) in enumerate(projected):
        gate_conv(i, he, cv)


def _conv_pre_kernel(x_ref, xp_ref, xn_ref, c_ref, modx_ref, modc_ref, g_ref, w_ref, cw_ref, y_ref, yc_ref, *,
                     nt, n_sub):
    t = pl.program_id(0)

    @pl.when(t < nt)
    def _():
        xe = jnp.concatenate([xp_ref[...], x_ref[...], xn_ref[...]], axis=0)
        _conv_tile(xe, modx_ref, g_ref, w_ref, cw_ref, y_ref, n_sub=n_sub,
                   pad_first=t == 0, pad_last=t == nt - 1)

    @pl.when(t == nt)
    def _():
        z = jnp.zeros((HALO, c_ref.shape[1]), F32)
        xe = jnp.concatenate([z, c_ref[...], z], axis=0)
        _conv_tile(xe, modc_ref, g_ref, w_ref, cw_ref, yc_ref, n_sub=1, pad_first=True, pad_last=True)


def _conv_pre_call(x, ctx, mod_x, mod_c, g, w_in, conv_w, layer, mixer, tm, n_sub):
    b, l, d = x.shape
    lc = ctx.shape[1]
    nt = l // tm
    hb = tm // HALO
    last_hb = l // HALO - 1
    kern = functools.partial(_conv_pre_kernel, nt=nt, n_sub=n_sub)

    def prev_halo(t, i):
        return (_latent_batch(t, i, nt, b), jnp.maximum(jnp.minimum(t, nt - 1) * hb - 1, 0), 0)

    def next_halo(t, i):
        return (_latent_batch(t, i, nt, b), jnp.minimum((jnp.minimum(t, nt - 1) + 1) * hb, last_hb), 0)

    return pl.pallas_call(
        kern,
        grid=(nt + 1, b),
        in_specs=[
            _tile_spec(tm, d, nt, b),
            pl.BlockSpec((None, HALO, d), prev_halo),
            pl.BlockSpec((None, HALO, d), next_halo),
            _ctx_spec(lc, d, nt),
            _mod_spec(mod_x, layer),
            _mod_spec(mod_c, layer),
            _layer_spec(g, layer),
            _layer_spec(w_in, mixer),
            _layer_spec(conv_w, mixer),
        ],
        out_specs=[_tile_spec(tm, d, nt, b), _ctx_spec(lc, d, nt)],
        out_shape=[jax.ShapeDtypeStruct((b, l, d), BF16), jax.ShapeDtypeStruct((b, lc, d), BF16)],
        compiler_params=pltpu.CompilerParams(
            dimension_semantics=("arbitrary", "arbitrary"), vmem_limit_bytes=VMEM_LIMIT),
    )(x, x, x, ctx, mod_x, mod_c, g, w_in, conv_w)


def _proj_tile(x_ref, mod_ref, g_ref, w_ref, o_ref, *, col0, n_scaled, q_scale, n_sub):
    sub = x_ref.shape[0] // n_sub
    for i in range(n_sub):
        rs = slice(i * sub, (i + 1) * sub)
        h = _norm_mod(x_ref[rs, :], g_ref[...], mod_ref[0:1, :], mod_ref[1:2, :]).astype(BF16)
        if n_scaled:
            o_ref[rs, :n_scaled] = (
                jnp.dot(h, w_ref[:, col0:col0 + n_scaled], preferred_element_type=F32) * q_scale).astype(BF16)
        o_ref[rs, n_scaled:] = jnp.dot(h, w_ref[:, col0 + n_scaled:], preferred_element_type=F32).astype(BF16)


def _proj_pre_kernel(x_ref, c_ref, modx_ref, modc_ref, g_ref, w_ref, qkv_ref, ckv_ref, *, nt, d, q_scale, n_sub):
    t = pl.program_id(0)

    @pl.when(t < nt)
    def _():
        _proj_tile(x_ref, modx_ref, g_ref, w_ref, qkv_ref, col0=0, n_scaled=d, q_scale=q_scale, n_sub=n_sub)

    @pl.when(t == nt)
    def _():
        _proj_tile(c_ref, modc_ref, g_ref, w_ref, ckv_ref, col0=d, n_scaled=0, q_scale=1.0, n_sub=1)


def _proj_pre_call(x, ctx, mod_x, mod_c, g, w, layer, mixer, tm, q_scale, n_sub):
    b, l, d = x.shape
    lc = ctx.shape[1]
    n = w.shape[2]
    nt = l // tm
    kern = functools.partial(_proj_pre_kernel, nt=nt, d=d, q_scale=q_scale, n_sub=n_sub)
    return pl.pallas_call(
        kern,
        grid=(nt + 1, b),
        in_specs=[
            _tile_spec(tm, d, nt, b),
            _ctx_spec(lc, d, nt),
            _mod_spec(mod_x, layer),
            _mod_spec(mod_c, layer),
            _layer_spec(g, layer),
            _layer_spec(w, mixer),
        ],
        out_specs=[_tile_spec(tm, n, nt, b), _ctx_spec(lc, n - d, nt)],
        out_shape=[jax.ShapeDtypeStruct((b, l, n), BF16), jax.ShapeDtypeStruct((b, lc, n - d), BF16)],
        compiler_params=pltpu.CompilerParams(
            dimension_semantics=("arbitrary", "arbitrary"), vmem_limit_bytes=VMEM_LIMIT),
    )(x, ctx, mod_x, mod_c, g, w)


def _post_tile(x_ref, y_ref, mod_ref, g_ref, wo_ref, w1_ref, w2_ref, fg_ref, o_ref, *, ff_chunk, n_sub):
    gate1 = mod_ref[2:3, :]
    shift2 = mod_ref[3:4, :]
    scale2 = mod_ref[4:5, :]
    gate2 = mod_ref[5:6, :]
    ff = w1_ref.shape[1]
    sub = x_ref.shape[0] // n_sub

    def out_proj(i):
        rs = slice(i * sub, (i + 1) * sub)
        return x_ref[rs, :] + gate1 * jnp.dot(y_ref[rs, :], wo_ref[...], preferred_element_type=F32)

    def mlp(x1):
        h = _norm_mod(x1, g_ref[...], shift2, scale2).astype(BF16)
        acc = None
        for f in range(0, ff, ff_chunk):
            a = jnp.dot(h, w1_ref[:, f:f + ff_chunk], preferred_element_type=F32)
            a = jnp.maximum(a, 0.0)
            a = (a * a).astype(BF16)
            part = jnp.dot(a, w2_ref[f:f + ff_chunk, :], preferred_element_type=F32)
            acc = part if acc is None else acc + part
        return x1 + gate2 * acc

    def finish(i, x2):
        if fg_ref is not None:
            inv = lax.rsqrt(jnp.mean(x2 * x2, axis=-1, keepdims=True) + RMS_EPS)
            x2 = (x2 * inv) * fg_ref[...]
        o_ref[i * sub:(i + 1) * sub, :] = x2

    x1_next = out_proj(0)
    for i in range(n_sub):
        x1 = x1_next
        if i + 1 < n_sub:
            x1_next = out_proj(i + 1)
        finish(i, mlp(x1))


def _post_kernel(*refs, nt, ff_chunk, n_sub, with_ctx, final):
    refs = list(refs)
    x_ref, y_ref = refs[:2]
    del refs[:2]
    if with_ctx:
        c_ref, yc_ref = refs[:2]
        del refs[:2]
    modx_ref = refs.pop(0)
    modc_ref = refs.pop(0) if with_ctx else None
    g_ref, wo_ref, w1_ref, w2_ref = refs[:4]
    del refs[:4]
    fg_ref = refs.pop(0) if final else None
    o_ref = refs.pop(0)
    oc_ref = refs.pop(0) if with_ctx else None
    t = pl.program_id(0)

    @pl.when(t < nt)
    def _():
        _post_tile(x_ref, y_ref, modx_ref, g_ref, wo_ref, w1_ref, w2_ref, fg_ref, o_ref,
                   ff_chunk=ff_chunk, n_sub=n_sub)

    if with_ctx:
        @pl.when(t == nt)
        def _():
            _post_tile(c_ref, yc_ref, modc_ref, g_ref, wo_ref, w1_ref, w2_ref, fg_ref, oc_ref,
                       ff_chunk=ff_chunk, n_sub=1)


def _post_call(x, y, ctx, yc, mod_x, mod_c, g, w_out, w1, w2, final_g, layer, mixer, tm, n_sub, ff_chunk=1024):
    b, l, d = x.shape
    nt = l // tm
    with_ctx = ctx is not None
    final = final_g is not None
    kern = functools.partial(_post_kernel, nt=nt, ff_chunk=ff_chunk, n_sub=n_sub, with_ctx=with_ctx, final=final)
    in_specs = [_tile_spec(tm, d, nt, b), _tile_spec(tm, d, nt, b)]
    args = [x, y]
    out_specs = [_tile_spec(tm, d, nt, b)]
    out_shape = [jax.ShapeDtypeStruct((b, l, d), F32)]
    if with_ctx:
        lc = ctx.shape[1]
        in_specs += [_ctx_spec(lc, d, nt), _ctx_spec(lc, d, nt)]
        args += [ctx, yc]
        out_specs.append(_ctx_spec(lc, d, nt))
        out_shape.append(jax.ShapeDtypeStruct((b, lc, d), F32))
    in_specs.append(_mod_spec(mod_x, layer))
    args.append(mod_x)
    if with_ctx:
        in_specs.append(_mod_spec(mod_c, layer))
        args.append(mod_c)
    in_specs += [_layer_spec(g, layer), _layer_spec(w_out, mixer), _layer_spec(w1, layer), _layer_spec(w2, layer)]
    args += [g, w_out, w1, w2]
    if final:
        in_specs.append(_layer_spec(final_g, 0))
        args.append(final_g)
    return pl.pallas_call(
        kern,
        grid=(nt + (1 if with_ctx else 0), b),
        in_specs=in_specs,
        out_specs=out_specs,
        out_shape=out_shape,
        compiler_params=pltpu.CompilerParams(
            dimension_semantics=("arbitrary", "arbitrary"), vmem_limit_bytes=VMEM_LIMIT),
    )(*args)


def _key_row_start(g, rows):
    return min(max(Q_ROWS * g - WIN_ROWS // 2, 0), rows - K_ROWS)


def _key_col_start(cb):
    lo = min(max(Q_COLS * cb - WIN_COLS // 2, 0), GRID_W - WIN_COLS)
    hi = min(max(Q_COLS * (cb + 1) - 1 - WIN_COLS // 2, 0), GRID_W - WIN_COLS) + WIN_COLS
    start = min(lo // HALO * HALO, GRID_W - K_COLS)
    assert start <= lo and hi <= start + K_COLS
    return start


def _window_row(r, rows):
    return min(max(r - WIN_ROWS // 2, 0), rows - WIN_ROWS)


def _visible_tiles(g, blk, rows):
    rel = [_window_row(Q_ROWS * g + BLOCK_ROWS * blk + i, rows) - _key_row_start(g, rows)
           for i in range(BLOCK_ROWS)]
    return min(rel) // ROWS_PER_TILE, -(-(max(rel) + WIN_ROWS) // ROWS_PER_TILE)


def _rpb_lane_rows(rpb):
    nc = rpb.shape[-1]
    placed = [jnp.pad(rpb, ((0, 0), (0, 0), (K_COLS * j, 128 - K_COLS * j - nc))) for j in range(ROWS_PER_TILE)]
    return jnp.stack(placed, axis=2)


def _fill_bias(rv_ref, bias_ref, *, rows):
    n_groups = rows // Q_ROWS
    shape = (Q_COLS, 128)
    qcl = lax.broadcasted_iota(jnp.int32, shape, 0)
    lane = lax.broadcasted_iota(jnp.int32, shape, 1)
    t = lane & (K_COLS - 1)
    neg = jnp.full(shape, NEG_INF, F32)
    for cls, g in enumerate((0, 1, n_groups - 1)):
        kr0 = _key_row_start(g, rows)
        for cb in range(GRID_W // Q_COLS):
            kstart = _key_col_start(cb)
            cs = jnp.clip(qcl + (Q_COLS * cb - WIN_COLS // 2), 0, GRID_W - WIN_COLS) - kstart
            col_ok = (t >= cs) & (t < cs + WIN_COLS)
            delta = kstart - Q_COLS * cb + WIN_COLS - 1
            for hh in range(HEADS_PER_STEP):
                for qr in range(Q_ROWS):
                    r = Q_ROWS * g + qr
                    rs = _window_row(r, rows)
                    t_lo, t_hi = _visible_tiles(g, qr // BLOCK_ROWS, rows)
                    for tile in range(t_lo, t_hi):
                        krow = [kr0 + ROWS_PER_TILE * tile + j for j in range(ROWS_PER_TILE)]
                        seen = [j for j in range(ROWS_PER_TILE) if rs <= krow[j] < rs + WIN_ROWS]
                        dst = (cls, hh, cb, slice(qr * Q_COLS, (qr + 1) * Q_COLS),
                               slice(tile * 128, (tile + 1) * 128))
                        if not seen:
                            bias_ref[dst] = neg
                            continue
                        v = None
                        for j in range(ROWS_PER_TILE):
                            dr = min(max(krow[j] - r + WIN_ROWS - 1, 0), 2 * WIN_ROWS - 2)
                            piece = rv_ref[hh, dr, j:j + 1, :]
                            v = piece if v is None else v + piece
                        rolled = pltpu.roll(jnp.broadcast_to(v * LOG2E, shape), (-delta) % 128, axis=1,
                                            stride=1, stride_axis=0)
                        ok = col_ok & (lane >= K_COLS * seen[0]) & (lane < K_COLS * (seen[-1] + 1))
                        bias_ref[dst] = jnp.where(ok, rolled, neg)


def _attn_kernel(q_ref, k_ref, v_ref, kc_ref, vc_ref, rv_ref, o_ref, bias_ref, kf_ref, vf_ref, *,
                 rows, head_dim):
    n_batch = q_ref.shape[0]
    n_groups = rows // Q_ROWS
    n_cb = GRID_W // Q_COLS
    nq = Q_ROWS * Q_COLS
    nk = K_ROWS * K_COLS

    @pl.when(pl.program_id(1) == 0)
    def _():
        _fill_bias(rv_ref, bias_ref, rows=rows)

    lane = lax.broadcasted_iota(jnp.int32, (1, HEADS_PER_STEP * head_dim), 1)
    nt = (((1,), (1,)), ((), ()))

    converted = {}

    def window(src_ref, dst_ref, bi, g, cb):
        r0 = _key_row_start(g, rows)
        c0 = _key_col_start(cb)
        done = converted.get((id(dst_ref), bi), 0)
        if done < r0 + K_ROWS:
            tok = slice(done * GRID_W, (r0 + K_ROWS) * GRID_W)
            dst_ref[bi, tok, :] = src_ref[bi, tok, :].astype(F32)
            converted[(id(dst_ref), bi)] = r0 + K_ROWS
        pieces = [dst_ref[bi, (r0 + j) * GRID_W + c0:(r0 + j) * GRID_W + c0 + K_COLS, :]
                  for j in range(K_ROWS)]
        return jnp.concatenate(pieces, axis=0).astype(BF16)

    def scores(bi, g, cb):
        pieces = [q_ref[bi, (Q_ROWS * g + qr) * GRID_W + Q_COLS * cb:
                        (Q_ROWS * g + qr) * GRID_W + Q_COLS * (cb + 1), :] for qr in range(Q_ROWS)]
        q = jnp.concatenate(pieces, axis=0)
        zero = jnp.zeros_like(q)
        qh = jnp.concatenate(
            [jnp.where((lane >= hh * head_dim) & (lane < (hh + 1) * head_dim), q, zero)
             for hh in range(HEADS_PER_STEP)], axis=0)
        s_loc = lax.dot_general(qh, window(k_ref, kf_ref, bi, g, cb), nt, preferred_element_type=F32)
        s_ctx = lax.dot_general(qh, kc_ref[bi], nt, preferred_element_type=F32)
        return s_loc, s_ctx

    def attend(bi, g, cb, s_loc, s_ctx):
        cls = 0 if g == 0 else (2 if g == n_groups - 1 else 1)
        p_blocks = []
        for hh in range(HEADS_PER_STEP):
            for blk in range(Q_ROWS // BLOCK_ROWS):
                t_lo, t_hi = _visible_tiles(g, blk, rows)
                nb = BLOCK_ROWS * Q_COLS
                qsl = slice(blk * nb, (blk + 1) * nb)
                rsl = slice(hh * nq + qsl.start, hh * nq + qsl.stop)
                tiles = [s_loc[rsl, t * 128:(t + 1) * 128] + bias_ref[cls, hh, cb, qsl, t * 128:(t + 1) * 128]
                         for t in range(t_lo, t_hi)]
                tiles += [s_ctx[rsl, t * 128:(t + 1) * 128] for t in range(s_ctx.shape[1] // 128)]
                m = jnp.max(functools.reduce(jnp.maximum, tiles), axis=-1, keepdims=True)
                probs = [jnp.exp2(t - m).astype(BF16) for t in tiles]
                n_loc = t_hi - t_lo
                pieces = [jnp.zeros((nb, 128), BF16)] * t_lo + probs[:n_loc]
                pieces += [jnp.zeros((nb, 128), BF16)] * (nk // 128 - t_hi) + probs[n_loc:]
                p_blocks.append(jnp.concatenate(pieces, axis=1))
        p = jnp.concatenate(p_blocks, axis=0)
        vw = jnp.concatenate([window(v_ref, vf_ref, bi, g, cb), jnp.ones((nk, 128), BF16)], axis=1)
        vc = jnp.concatenate([vc_ref[bi], jnp.ones((vc_ref.shape[1], 128), BF16)], axis=1)
        o = (jnp.dot(p[:, :nk], vw, preferred_element_type=F32)
             + jnp.dot(p[:, nk:], vc, preferred_element_type=F32))
        o = o[:, :128] / o[:, 128:]
        out = o[(HEADS_PER_STEP - 1) * nq:]
        for hh in range(HEADS_PER_STEP - 2, -1, -1):
            out = jnp.where(lane < (hh + 1) * head_dim, o[hh * nq:(hh + 1) * nq], out)
        out = out.astype(BF16)
        for qr in range(Q_ROWS):
            tok = (Q_ROWS * g + qr) * GRID_W + Q_COLS * cb
            o_ref[bi, tok:tok + Q_COLS, :] = out[qr * Q_COLS:(qr + 1) * Q_COLS]

    units = [(bi, g, cb) for bi in range(n_batch) for g in range(n_groups) for cb in range(n_cb)]
    nxt = scores(*units[0])
    for u, unit in enumerate(units):
        cur = nxt
        if u + 1 < len(units):
            nxt = scores(*units[u + 1])
        attend(*unit, *cur)


def _attn_call(qkv, ckv, rpb_rows):
    b, s, d3 = qkv.shape
    d = d3 // 3
    ctx_len = ckv.shape[1]
    head_dim = d // N_HEADS
    w = HEADS_PER_STEP * head_dim
    assert w == 128
    n_hp = d // w
    rows = s // GRID_W
    n_dr = rpb_rows.shape[1]
    bb = ATTN_BATCH if b % ATTN_BATCH == 0 else 1
    kern = functools.partial(_attn_kernel, rows=rows, head_dim=head_dim)
    return pl.pallas_call(
        kern,
        grid=(n_hp, b // bb),
        in_specs=[
            pl.BlockSpec((bb, s, w), lambda h, i: (i, 0, h)),
            pl.BlockSpec((bb, s, w), lambda h, i: (i, 0, n_hp + h)),
            pl.BlockSpec((bb, s, w), lambda h, i: (i, 0, 2 * n_hp + h)),
            pl.BlockSpec((bb, ctx_len, w), lambda h, i: (i, 0, h)),
            pl.BlockSpec((bb, ctx_len, w), lambda h, i: (i, 0, n_hp + h)),
            pl.BlockSpec((HEADS_PER_STEP, n_dr, ROWS_PER_TILE, 128), lambda h, i: (h, 0, 0, 0)),
        ],
        out_specs=pl.BlockSpec((bb, s, w), lambda h, i: (i, 0, h)),
        out_shape=jax.ShapeDtypeStruct((b, s, d), BF16),
        scratch_shapes=[
            pltpu.VMEM((3, HEADS_PER_STEP, GRID_W // Q_COLS, Q_ROWS * Q_COLS, K_ROWS * K_COLS), F32),
            pltpu.VMEM((bb, s, w), F32),
            pltpu.VMEM((bb, s, w), F32),
        ],
        compiler_params=pltpu.CompilerParams(
            dimension_semantics=("arbitrary", "arbitrary"), vmem_limit_bytes=VMEM_LIMIT),
    )(qkv, qkv, qkv, ckv, ckv, rpb_rows)


def kernel(x, c, ctx, c_ctx, norm1_g, norm2_g, ada_w, ada_b, conv_in_w, conv_w, conv_out_w,
           attn_qkv_w, attn_rpb, attn_out_w, mlp_w1, mlp_w2, final_g):
    b, s, d = x.shape
    depth = ada_w.shape[0]
    assert depth == 2 and s % (GRID_W * Q_ROWS) == 0 and s // GRID_W >= K_ROWS and d % N_HEADS == 0
    head_dim = d // N_HEADS
    tm_pre, tm_post = 1024, 1024

    pad = (-(b + 1)) % 8
    cond = jnp.concatenate([c, c_ctx[None, :], jnp.zeros((pad, d), F32)], axis=0)
    mods = _ada_call(cond, ada_w, ada_b).reshape(depth, b + 1 + pad, 6, d)
    mod_lat = mods[:, :b]
    mod_ctx = mods[:, b:b + 1]

    n1 = norm1_g.reshape(depth, 1, d)
    n2 = norm2_g.reshape(depth, 1, d)
    fg = final_g.reshape(1, 1, d)
    w1 = mlp_w1.astype(BF16)
    w2 = mlp_w2.astype(BF16)
    w_in = conv_in_w.astype(BF16)
    w_co = conv_out_w.astype(BF16)
    w_qkv = attn_qkv_w.astype(BF16)
    w_ao = attn_out_w.astype(BF16)

    y, yc = _conv_pre_call(x, ctx, mod_lat, mod_ctx, n1, w_in, conv_w, 0, 0, tm_pre, 2)
    x, ctx = _post_call(x, y, ctx, yc, mod_lat, mod_ctx, n2, w_co, w1, w2, None, 0, 0, tm_post, 4)

    qkv, ckv = _proj_pre_call(x, ctx, mod_lat, mod_ctx, n1, w_qkv, 1, 0, tm_pre, head_dim ** -0.5 * LOG2E, 2)
    o = _attn_call(qkv, ckv, _rpb_lane_rows(attn_rpb[0]))
    (out,) = _post_call(x, o, None, None, mod_lat, None, n2, w_ao, w1, w2, fg, 1, 0, tm_post, 4)
    return out
```

```python
import functools

import jax
import jax.numpy as jnp
from jax import lax
from jax.experimental import pallas as pl
from jax.experimental.pallas import tpu as pltpu

GRID_W = 64
N_HEADS = 16
WIN_ROWS = 8
WIN_COLS = 16
RMS_EPS = 1e-6
NEG_INF = -1e30
LOG2E = 1.4426950408889634

HALO = 8
HEADS_PER_STEP = 2
Q_ROWS, Q_COLS = 8, 16
K_ROWS, K_COLS = 16, 32
ROWS_PER_TILE = 128 // K_COLS
BLOCK_ROWS = 4
ATTN_BATCH = 2
VMEM_LIMIT = 56 * 1024 * 1024

BF16 = jnp.bfloat16
F32 = jnp.float32


def _norm_mod(v, g, shift, scale):
    inv = lax.rsqrt(jnp.mean(v * v, axis=-1, keepdims=True) + RMS_EPS)
    return ((v * inv) * g) * (1.0 + scale) + shift


def _ada_kernel(cond_ref, w_ref, b_ref, o_ref):
    cnd = cond_ref[...]
    s = (cnd / (1.0 + jnp.exp(-cnd))).astype(BF16)
    o_ref[...] = jnp.dot(s, w_ref[...].astype(BF16), preferred_element_type=F32) + b_ref[...]


def _ada_call(cond, ada_w, ada_b):
    depth, d, n = ada_w.shape
    rows = cond.shape[0]
    bn = d
    return pl.pallas_call(
        _ada_kernel,
        grid=(depth, n // bn),
        in_specs=[
            pl.BlockSpec((rows, d), lambda i, j: (0, 0)),
            pl.BlockSpec((None, d, bn), lambda i, j: (i, 0, j)),
            pl.BlockSpec((None, 1, bn), lambda i, j: (i, 0, j)),
        ],
        out_specs=pl.BlockSpec((None, rows, bn), lambda i, j: (i, 0, j)),
        out_shape=jax.ShapeDtypeStruct((depth, rows, n), F32),
        compiler_params=pltpu.CompilerParams(
            dimension_semantics=("arbitrary", "arbitrary"), vmem_limit_bytes=VMEM_LIMIT),
    )(cond, ada_w, ada_b.reshape(depth, 1, n))


def _mod_spec(mod, layer):
    if mod.shape[1] == 1:
        return pl.BlockSpec((None, None, 6, mod.shape[3]), lambda t, b: (layer, 0, 0, 0))
    return pl.BlockSpec((None, None, 6, mod.shape[3]), lambda t, b: (layer, b, 0, 0))


def _layer_spec(arr, layer):
    nd = arr.ndim - 1
    return pl.BlockSpec((None,) + arr.shape[1:], lambda t, b: (layer,) + (0,) * nd,
                        pipeline_mode=pl.Buffered(1))


def _latent_batch(t, b, nt, n_batch):
    return jnp.where(t < nt, b, n_batch - 1)


def _tile_spec(tm, d, nt, n_batch):
    return pl.BlockSpec((None, tm, d), lambda t, b: (_latent_batch(t, b, nt, n_batch), jnp.minimum(t, nt - 1), 0))


def _ctx_spec(rows, d, nt):
    return pl.BlockSpec((None, rows, d), lambda t, b: (jnp.where(t == nt, b, 0), 0, 0))


def _conv_tile(xe, mod_ref, g_ref, w_ref, cw_ref, o_ref, *, n_sub, pad_first, pad_last):
    d = xe.shape[1]
    shift = mod_ref[0:1, :]
    scale = mod_ref[1:2, :]
    sub = (xe.shape[0] - 2 * HALO) // n_sub
    n_ext = sub + 2 * HALO
    row = lax.broadcasted_iota(jnp.int32, (n_ext, 1), 0)

    def project(i):
        he = _norm_mod(xe[i * sub:i * sub + n_ext], g_ref[...], shift, scale).astype(BF16)
        return he, jnp.dot(he, w_ref[:, d:], preferred_element_type=F32)

    def gate_conv(i, he, cv):
        u = cv[:, :d] * cv[:, d:]
        if i == 0:
            u = jnp.where((row >= HALO) | jnp.logical_not(pad_first), u, 0.0)
        if i == n_sub - 1:
            u = jnp.where((row < sub + HALO) | jnp.logical_not(pad_last), u, 0.0)
        u_prev = pltpu.roll(u, 1, axis=0)[HALO:HALO + sub]
        u_next = pltpu.roll(u, n_ext - 1, axis=0)[HALO:HALO + sub]
        conv = u_prev * cw_ref[0:1, :] + u[HALO:HALO + sub] * cw_ref[1:2, :] + u_next * cw_ref[2:3, :]
        bg = jnp.dot(he[HALO:HALO + sub], w_ref[:, :d], preferred_element_type=F32)
        o_ref[i * sub:(i + 1) * sub, :] = (bg * conv).astype(BF16)

    projected = [project(i) for i in range(n_sub)]
    for i, (he, cv) in enumerate(projected):
        gate_conv(i, he, cv)


def _conv_pre_kernel(x_ref, xp_ref, xn_ref, c_ref, modx_ref, modc_ref, g_ref, w_ref, cw_ref, *rest, nt, n_sub):
    n_cast = (len(rest) - 2) // 2
    cast_in = rest[:n_cast]
    y_ref, yc_ref = rest[n_cast:n_cast + 2]
    cast_out = rest[n_cast + 2:]
    t = pl.program_id(0)

    @pl.when(t < nt)
    def _():
        xe = jnp.concatenate([xp_ref[...], x_ref[...], xn_ref[...]], axis=0)
        _conv_tile(xe, modx_ref, g_ref, w_ref, cw_ref, y_ref, n_sub=n_sub,
                   pad_first=t == 0, pad_last=t == nt - 1)
        for src, dst in zip(cast_in, cast_out):
            dst[...] = src[...].astype(BF16)

    @pl.when(t == nt)
    def _():
        z = jnp.zeros((HALO, c_ref.shape[1]), F32)
        xe = jnp.concatenate([z, c_ref[...], z], axis=0)
        _conv_tile(xe, modc_ref, g_ref, w_ref, cw_ref, yc_ref, n_sub=1, pad_first=True, pad_last=True)


def _conv_pre_call(x, ctx, mod_x, mod_c, g, w_in, conv_w, layer, mixer, tm, n_sub, to_cast):
    b, l, d = x.shape
    lc = ctx.shape[1]
    nt = l // tm
    hb = tm // HALO
    last_hb = l // HALO - 1
    n_steps = nt * b
    kern = functools.partial(_conv_pre_kernel, nt=nt, n_sub=n_sub)

    def prev_halo(t, i):
        return (_latent_batch(t, i, nt, b), jnp.maximum(jnp.minimum(t, nt - 1) * hb - 1, 0), 0)

    def next_halo(t, i):
        return (_latent_batch(t, i, nt, b), jnp.minimum((jnp.minimum(t, nt - 1) + 1) * hb, last_hb), 0)

    def cast_block(t, i):
        return (jnp.minimum(t * b + i, n_steps - 1), 0)

    flat = [w.reshape(-1, w.shape[-1]) for w in to_cast]
    cast_specs = [pl.BlockSpec((w.shape[0] // n_steps, w.shape[1]), cast_block) for w in flat]
    assert all(w.shape[0] % (n_steps * 16) == 0 for w in flat)
    outs = pl.pallas_call(
        kern,
        grid=(nt + 1, b),
        in_specs=[
            _tile_spec(tm, d, nt, b),
            pl.BlockSpec((None, HALO, d), prev_halo),
            pl.BlockSpec((None, HALO, d), next_halo),
            _ctx_spec(lc, d, nt),
            _mod_spec(mod_x, layer),
            _mod_spec(mod_c, layer),
            _layer_spec(g, layer),
            _layer_spec(w_in, mixer),
            _layer_spec(conv_w, mixer),
        ] + cast_specs,
        out_specs=[_tile_spec(tm, d, nt, b), _ctx_spec(lc, d, nt)] + cast_specs,
        out_shape=[jax.ShapeDtypeStruct((b, l, d), BF16), jax.ShapeDtypeStruct((b, lc, d), BF16)]
        + [jax.ShapeDtypeStruct(w.shape, BF16) for w in flat],
        compiler_params=pltpu.CompilerParams(
            dimension_semantics=("arbitrary", "arbitrary"), vmem_limit_bytes=VMEM_LIMIT),
    )(x, x, x, ctx, mod_x, mod_c, g, w_in, conv_w, *flat)
    return outs[0], outs[1], [o.reshape(w.shape) for o, w in zip(outs[2:], to_cast)]


def _proj_tile(x_ref, mod_ref, g_ref, w_ref, o_ref, *, col0, n_scaled, q_scale, n_sub):
    sub = x_ref.shape[0] // n_sub
    for i in range(n_sub):
        rs = slice(i * sub, (i + 1) * sub)
        h = _norm_mod(x_ref[rs, :], g_ref[...], mod_ref[0:1, :], mod_ref[1:2, :]).astype(BF16)
        if n_scaled:
            o_ref[rs, :n_scaled] = (
                jnp.dot(h, w_ref[:, col0:col0 + n_scaled], preferred_element_type=F32) * q_scale).astype(BF16)
        o_ref[rs, n_scaled:] = jnp.dot(h, w_ref[:, col0 + n_scaled:], preferred_element_type=F32).astype(BF16)


def _proj_pre_kernel(x_ref, c_ref, modx_ref, modc_ref, g_ref, w_ref, qkv_ref, ckv_ref, *, nt, d, q_scale, n_sub):
    t = pl.program_id(0)

    @pl.when(t < nt)
    def _():
        _proj_tile(x_ref, modx_ref, g_ref, w_ref, qkv_ref, col0=0, n_scaled=d, q_scale=q_scale, n_sub=n_sub)

    @pl.when(t == nt)
    def _():
        _proj_tile(c_ref, modc_ref, g_ref, w_ref, ckv_ref, col0=d, n_scaled=0, q_scale=1.0, n_sub=1)


def _proj_pre_call(x, ctx, mod_x, mod_c, g, w, layer, mixer, tm, q_scale, n_sub):
    b, l, d = x.shape
    lc = ctx.shape[1]
    n = w.shape[2]
    nt = l // tm
    kern = functools.partial(_proj_pre_kernel, nt=nt, d=d, q_scale=q_scale, n_sub=n_sub)
    return pl.pallas_call(
        kern,
        grid=(nt + 1, b),
        in_specs=[
            _tile_spec(tm, d, nt, b),
            _ctx_spec(lc, d, nt),
            _mod_spec(mod_x, layer),
            _mod_spec(mod_c, layer),
            _layer_spec(g, layer),
            _layer_spec(w, mixer),
        ],
        out_specs=[_tile_spec(tm, n, nt, b), _ctx_spec(lc, n - d, nt)],
        out_shape=[jax.ShapeDtypeStruct((b, l, n), BF16), jax.ShapeDtypeStruct((b, lc, n - d), BF16)],
        compiler_params=pltpu.CompilerParams(
            dimension_semantics=("arbitrary", "arbitrary"), vmem_limit_bytes=VMEM_LIMIT),
    )(x, ctx, mod_x, mod_c, g, w)


def _post_tile(x_ref, y_ref, mod_ref, g_ref, wo_ref, w1_ref, w2_ref, fg_ref, o_ref, *, ff_chunk, n_sub):
    gate1 = mod_ref[2:3, :]
    shift2 = mod_ref[3:4, :]
    scale2 = mod_ref[4:5, :]
    gate2 = mod_ref[5:6, :]
    ff = w1_ref.shape[1]
    sub = x_ref.shape[0] // n_sub

    def out_proj(i):
        rs = slice(i * sub, (i + 1) * sub)
        return x_ref[rs, :] + gate1 * jnp.dot(y_ref[rs, :], wo_ref[...], preferred_element_type=F32)

    def mlp(x1):
        h = _norm_mod(x1, g_ref[...], shift2, scale2).astype(BF16)
        acc = None
        for f in range(0, ff, ff_chunk):
            a = jnp.dot(h, w1_ref[:, f:f + ff_chunk], preferred_element_type=F32)
            a = jnp.maximum(a, 0.0)
            a = (a * a).astype(BF16)
            part = jnp.dot(a, w2_ref[f:f + ff_chunk, :], preferred_element_type=F32)
            acc = part if acc is None else acc + part
        return x1 + gate2 * acc

    def finish(i, x2):
        if fg_ref is not None:
            inv = lax.rsqrt(jnp.mean(x2 * x2, axis=-1, keepdims=True) + RMS_EPS)
            x2 = (x2 * inv) * fg_ref[...]
        o_ref[i * sub:(i + 1) * sub, :] = x2

    x1_next = out_proj(0)
    for i in range(n_sub):
        x1 = x1_next
        if i + 1 < n_sub:
            x1_next = out_proj(i + 1)
        finish(i, mlp(x1))


def _post_kernel(*refs, nt, ff_chunk, n_sub, with_ctx, final):
    refs = list(refs)
    x_ref, y_ref = refs[:2]
    del refs[:2]
    if with_ctx:
        c_ref, yc_ref = refs[:2]
        del refs[:2]
    modx_ref = refs.pop(0)
    modc_ref = refs.pop(0) if with_ctx else None
    g_ref, wo_ref, w1_ref, w2_ref = refs[:4]
    del refs[:4]
    fg_ref = refs.pop(0) if final else None
    o_ref = refs.pop(0)
    oc_ref = refs.pop(0) if with_ctx else None
    t = pl.program_id(0)

    @pl.when(t < nt)
    def _():
        _post_tile(x_ref, y_ref, modx_ref, g_ref, wo_ref, w1_ref, w2_ref, fg_ref, o_ref,
                   ff_chunk=ff_chunk, n_sub=n_sub)

    if with_ctx:
        @pl.when(t == nt)
        def _():
            _post_tile(c_ref, yc_ref, modc_ref, g_ref, wo_ref, w1_ref, w2_ref, fg_ref, oc_ref,
                       ff_chunk=ff_chunk, n_sub=1)


def _post_call(x, y, ctx, yc, mod_x, mod_c, g, w_out, w1, w2, final_g, layer, mixer, tm, n_sub, ff_chunk=1024):
    b, l, d = x.shape
    nt = l // tm
    with_ctx = ctx is not None
    final = final_g is not None
    kern = functools.partial(_post_kernel, nt=nt, ff_chunk=ff_chunk, n_sub=n_sub, with_ctx=with_ctx, final=final)
    in_specs = [_tile_spec(tm, d, nt, b), _tile_spec(tm, d, nt, b)]
    args = [x, y]
    out_specs = [_tile_spec(tm, d, nt, b)]
    out_shape = [jax.ShapeDtypeStruct((b, l, d), F32)]
    if with_ctx:
        lc = ctx.shape[1]
        in_specs += [_ctx_spec(lc, d, nt), _ctx_spec(lc, d, nt)]
        args += [ctx, yc]
        out_specs.append(_ctx_spec(lc, d, nt))
        out_shape.append(jax.ShapeDtypeStruct((b, lc, d), F32))
    in_specs.append(_mod_spec(mod_x, layer))
    args.append(mod_x)
    if with_ctx:
        in_specs.append(_mod_spec(mod_c, layer))
        args.append(mod_c)
    in_specs += [_layer_spec(g, layer), _layer_spec(w_out, mixer), _layer_spec(w1, layer), _layer_spec(w2, layer)]
    args += [g, w_out, w1, w2]
    if final:
        in_specs.append(_layer_spec(final_g, 0))
        args.append(final_g)
    return pl.pallas_call(
        kern,
        grid=(nt + (1 if with_ctx else 0), b),
        in_specs=in_specs,
        out_specs=out_specs,
        out_shape=out_shape,
        compiler_params=pltpu.CompilerParams(
            dimension_semantics=("arbitrary", "arbitrary"), vmem_limit_bytes=VMEM_LIMIT),
    )(*args)


def _key_row_start(g, rows):
    return min(max(Q_ROWS * g - WIN_ROWS // 2, 0), rows - K_ROWS)


def _key_col_start(cb):
    lo = min(max(Q_COLS * cb - WIN_COLS // 2, 0), GRID_W - WIN_COLS)
    hi = min(max(Q_COLS * (cb + 1) - 1 - WIN_COLS // 2, 0), GRID_W - WIN_COLS) + WIN_COLS
    start = min(lo // HALO * HALO, GRID_W - K_COLS)
    assert start <= lo and hi <= start + K_COLS
    return start


def _window_row(r, rows):
    return min(max(r - WIN_ROWS // 2, 0), rows - WIN_ROWS)


def _visible_tiles(g, blk, rows):
    rel = [_window_row(Q_ROWS * g + BLOCK_ROWS * blk + i, rows) - _key_row_start(g, rows)
           for i in range(BLOCK_ROWS)]
    return min(rel) // ROWS_PER_TILE, -(-(max(rel) + WIN_ROWS) // ROWS_PER_TILE)


def _rpb_lane_rows(rpb):
    nc = rpb.shape[-1]
    placed = [jnp.pad(rpb, ((0, 0), (0, 0), (K_COLS * j, 128 - K_COLS * j - nc))) for j in range(ROWS_PER_TILE)]
    return jnp.stack(placed, axis=2)


def _fill_bias(rv_ref, bias_ref, *, rows):
    n_groups = rows // Q_ROWS
    shape = (Q_COLS, 128)
    qcl = lax.broadcasted_iota(jnp.int32, shape, 0)
    lane = lax.broadcasted_iota(jnp.int32, shape, 1)
    t = lane & (K_COLS - 1)
    neg = jnp.full(shape, NEG_INF, F32)
    for cls, g in enumerate((0, 1, n_groups - 1)):
        kr0 = _key_row_start(g, rows)
        for cb in range(GRID_W // Q_COLS):
            kstart = _key_col_start(cb)
            cs = jnp.clip(qcl + (Q_COLS * cb - WIN_COLS // 2), 0, GRID_W - WIN_COLS) - kstart
            col_ok = (t >= cs) & (t < cs + WIN_COLS)
            delta = kstart - Q_COLS * cb + WIN_COLS - 1
            for hh in range(HEADS_PER_STEP):
                for qr in range(Q_ROWS):
                    r = Q_ROWS * g + qr
                    rs = _window_row(r, rows)
                    t_lo, t_hi = _visible_tiles(g, qr // BLOCK_ROWS, rows)
                    for tile in range(t_lo, t_hi):
                        krow = [kr0 + ROWS_PER_TILE * tile + j for j in range(ROWS_PER_TILE)]
                        seen = [j for j in range(ROWS_PER_TILE) if rs <= krow[j] < rs + WIN_ROWS]
                        dst = (cls, hh, cb, slice(qr * Q_COLS, (qr + 1) * Q_COLS),
                               slice(tile * 128, (tile + 1) * 128))
                        if not seen:
                            bias_ref[dst] = neg
                            continue
                        v = None
                        for j in range(ROWS_PER_TILE):
                            dr = min(max(krow[j] - r + WIN_ROWS - 1, 0), 2 * WIN_ROWS - 2)
                            piece = rv_ref[hh, dr, j:j + 1, :]
                            v = piece if v is None else v + piece
                        rolled = pltpu.roll(jnp.broadcast_to(v * LOG2E, shape), (-delta) % 128, axis=1,
                                            stride=1, stride_axis=0)
                        ok = col_ok & (lane >= K_COLS * seen[0]) & (lane < K_COLS * (seen[-1] + 1))
                        bias_ref[dst] = jnp.where(ok, rolled, neg)


def _attn_kernel(q_ref, k_ref, v_ref, kc_ref, vc_ref, rv_ref, o_ref, bias_ref, kf_ref, vf_ref, *,
                 rows, head_dim):
    n_batch = q_ref.shape[0]
    n_groups = rows // Q_ROWS
    n_cb = GRID_W // Q_COLS
    nq = Q_ROWS * Q_COLS
    nk = K_ROWS * K_COLS

    @pl.when(pl.program_id(1) == 0)
    def _():
        _fill_bias(rv_ref, bias_ref, rows=rows)

    lane = lax.broadcasted_iota(jnp.int32, (1, HEADS_PER_STEP * head_dim), 1)
    nt = (((1,), (1,)), ((), ()))

    converted = {}

    def window(src_ref, dst_ref, bi, g, cb):
        r0 = _key_row_start(g, rows)
        c0 = _key_col_start(cb)
        done = converted.get((id(dst_ref), bi), 0)
        if done < r0 + K_ROWS:
            tok = slice(done * GRID_W, (r0 + K_ROWS) * GRID_W)
            dst_ref[bi, tok, :] = src_ref[bi, tok, :].astype(F32)
            converted[(id(dst_ref), bi)] = r0 + K_ROWS
        pieces = [dst_ref[bi, (r0 + j) * GRID_W + c0:(r0 + j) * GRID_W + c0 + K_COLS, :]
                  for j in range(K_ROWS)]
        return jnp.concatenate(pieces, axis=0).astype(BF16)

    def scores(bi, g, cb):
        pieces = [q_ref[bi, (Q_ROWS * g + qr) * GRID_W + Q_COLS * cb:
                        (Q_ROWS * g + qr) * GRID_W + Q_COLS * (cb + 1), :] for qr in range(Q_ROWS)]
        q = jnp.concatenate(pieces, axis=0)
        zero = jnp.zeros_like(q)
        qh = jnp.concatenate(
            [jnp.where((lane >= hh * head_dim) & (lane < (hh + 1) * head_dim), q, zero)
             for hh in range(HEADS_PER_STEP)], axis=0)
        s_loc = lax.dot_general(qh, window(k_ref, kf_ref, bi, g, cb), nt, preferred_element_type=F32)
        s_ctx = lax.dot_general(qh, kc_ref[bi], nt, preferred_element_type=F32)
        return s_loc, s_ctx

    def attend(bi, g, cb, s_loc, s_ctx):
        cls = 0 if g == 0 else (2 if g == n_groups - 1 else 1)
        p_blocks = []
        for hh in range(HEADS_PER_STEP):
            for blk in range(Q_ROWS // BLOCK_ROWS):
                t_lo, t_hi = _visible_tiles(g, blk, rows)
                nb = BLOCK_ROWS * Q_COLS
                qsl = slice(blk * nb, (blk + 1) * nb)
                rsl = slice(hh * nq + qsl.start, hh * nq + qsl.stop)
                tiles = [s_loc[rsl, t * 128:(t + 1) * 128] + bias_ref[cls, hh, cb, qsl, t * 128:(t + 1) * 128]
                         for t in range(t_lo, t_hi)]
                tiles += [s_ctx[rsl, t * 128:(t + 1) * 128] for t in range(s_ctx.shape[1] // 128)]
                m = jnp.max(functools.reduce(jnp.maximum, tiles), axis=-1, keepdims=True)
                probs = [jnp.exp2(t - m).astype(BF16) for t in tiles]
                n_loc = t_hi - t_lo
                pieces = [jnp.zeros((nb, 128), BF16)] * t_lo + probs[:n_loc]
                pieces += [jnp.zeros((nb, 128), BF16)] * (nk // 128 - t_hi) + probs[n_loc:]
                p_blocks.append(jnp.concatenate(pieces, axis=1))
        p = jnp.concatenate(p_blocks, axis=0)
        vw = jnp.concatenate([window(v_ref, vf_ref, bi, g, cb), jnp.ones((nk, 128), BF16)], axis=1)
        vc = jnp.concatenate([vc_ref[bi], jnp.ones((vc_ref.shape[1], 128), BF16)], axis=1)
        o = (jnp.dot(p[:, :nk], vw, preferred_element_type=F32)
             + jnp.dot(p[:, nk:], vc, preferred_element_type=F32))
        o = o[:, :128] / o[:, 128:]
        out = o[(HEADS_PER_STEP - 1) * nq:]
        for hh in range(HEADS_PER_STEP - 2, -1, -1):
            out = jnp.where(lane < (hh + 1) * head_dim, o[hh * nq:(hh + 1) * nq], out)
        out = out.astype(BF16)
        for qr in range(Q_ROWS):
            tok = (Q_ROWS * g + qr) * GRID_W + Q_COLS * cb
            o_ref[bi, tok:tok + Q_COLS, :] = out[qr * Q_COLS:(qr + 1) * Q_COLS]

    units = [(bi, g, cb) for bi in range(n_batch) for g in range(n_groups) for cb in range(n_cb)]
    nxt = scores(*units[0])
    for u, unit in enumerate(units):
        cur = nxt
        if u + 1 < len(units):
            nxt = scores(*units[u + 1])
        attend(*unit, *cur)


def _attn_call(qkv, ckv, rpb_rows):
    b, s, d3 = qkv.shape
    d = d3 // 3
    ctx_len = ckv.shape[1]
    head_dim = d // N_HEADS
    w = HEADS_PER_STEP * head_dim
    assert w == 128
    n_hp = d // w
    rows = s // GRID_W
    n_dr = rpb_rows.shape[1]
    bb = ATTN_BATCH if b % ATTN_BATCH == 0 else 1
    kern = functools.partial(_attn_kernel, rows=rows, head_dim=head_dim)
    return pl.pallas_call(
        kern,
        grid=(n_hp, b // bb),
        in_specs=[
            pl.BlockSpec((bb, s, w), lambda h, i: (i, 0, h)),
            pl.BlockSpec((bb, s, w), lambda h, i: (i, 0, n_hp + h)),
            pl.BlockSpec((bb, s, w), lambda h, i: (i, 0, 2 * n_hp + h)),
            pl.BlockSpec((bb, ctx_len, w), lambda h, i: (i, 0, h)),
            pl.BlockSpec((bb, ctx_len, w), lambda h, i: (i, 0, n_hp + h)),
            pl.BlockSpec((HEADS_PER_STEP, n_dr, ROWS_PER_TILE, 128), lambda h, i: (h, 0, 0, 0)),
        ],
        out_specs=pl.BlockSpec((bb, s, w), lambda h, i: (i, 0, h)),
        out_shape=jax.ShapeDtypeStruct((b, s, d), BF16),
        scratch_shapes=[
            pltpu.VMEM((3, HEADS_PER_STEP, GRID_W // Q_COLS, Q_ROWS * Q_COLS, K_ROWS * K_COLS), F32),
            pltpu.VMEM((bb, s, w), F32),
            pltpu.VMEM((bb, s, w), F32),
        ],
        compiler_params=pltpu.CompilerParams(
            dimension_semantics=("arbitrary", "arbitrary"), vmem_limit_bytes=VMEM_LIMIT),
    )(qkv, qkv, qkv, ckv, ckv, rpb_rows)


def kernel(x, c, ctx, c_ctx, norm1_g, norm2_g, ada_w, ada_b, conv_in_w, conv_w, conv_out_w,
           attn_qkv_w, attn_rpb, attn_out_w, mlp_w1, mlp_w2, final_g):
    b, s, d = x.shape
    depth = ada_w.shape[0]
    assert depth == 2 and s % (GRID_W * Q_ROWS) == 0 and s // GRID_W >= K_ROWS and d % N_HEADS == 0
    head_dim = d // N_HEADS
    tm_pre, tm_post = 1024, 1024

    pad = (-(b + 1)) % 8
    cond = jnp.concatenate([c, c_ctx[None, :], jnp.zeros((pad, d), F32)], axis=0)
    mods = _ada_call(cond, ada_w, ada_b).reshape(depth, b + 1 + pad, 6, d)
    mod_lat = mods[:, :b]
    mod_ctx = mods[:, b:b + 1]

    n1 = norm1_g.reshape(depth, 1, d)
    n2 = norm2_g.reshape(depth, 1, d)
    fg = final_g.reshape(1, 1, d)
    w_in = conv_in_w.astype(BF16)

    y, yc, (w_co, w1, w2, w_qkv, w_ao) = _conv_pre_call(
        x, ctx, mod_lat, mod_ctx, n1, w_in, conv_w, 0, 0, tm_pre, 2,
        to_cast=(conv_out_w, mlp_w1, mlp_w2, attn_qkv_w, attn_out_w))
    x, ctx = _post_call(x, y, ctx, yc, mod_lat, mod_ctx, n2, w_co, w1, w2, None, 0, 0, tm_post, 4)

    qkv, ckv = _proj_pre_call(x, ctx, mod_lat, mod_ctx, n1, w_qkv, 1, 0, tm_pre, head_dim ** -0.5 * LOG2E, 2)
    o = _attn_call(qkv, ckv, _rpb_lane_rows(attn_rpb[0]))
    (out,) = _post_call(x, o, None, None, mod_lat, None, n2, w_ao, w1, w2, fg, 1, 0, tm_post, 4)
    return out
```

```python
import functools

import jax
import jax.numpy as jnp
from jax import lax
from jax.experimental import pallas as pl
from jax.experimental.pallas import tpu as pltpu

GRID_W = 64
N_HEADS = 16
WIN_ROWS = 8
WIN_COLS = 16
RMS_EPS = 1e-6
NEG_INF = -1e30
LOG2E = 1.4426950408889634

HALO = 8
HEADS_PER_STEP = 2
Q_ROWS, Q_COLS = 8, 16
K_ROWS, K_COLS = 16, 32
ROWS_PER_TILE = 128 // K_COLS
BLOCK_ROWS = 4
ATTN_BATCH = 2
VMEM_LIMIT = 56 * 1024 * 1024

BF16 = jnp.bfloat16
F32 = jnp.float32


def _norm_mod(v, g, shift, scale):
    inv = lax.rsqrt(jnp.mean(v * v, axis=-1, keepdims=True) + RMS_EPS)
    return ((v * inv) * g) * (1.0 + scale) + shift


def _ada_kernel(cond_ref, w_ref, b_ref, o_ref):
    cnd = cond_ref[...]
    s = (cnd / (1.0 + jnp.exp(-cnd))).astype(BF16)
    o_ref[...] = jnp.dot(s, w_ref[...].astype(BF16), preferred_element_type=F32) + b_ref[...]


def _ada_call(cond, ada_w, ada_b):
    depth, d, n = ada_w.shape
    rows = cond.shape[0]
    bn = d
    return pl.pallas_call(
        _ada_kernel,
        grid=(depth, n // bn),
        in_specs=[
            pl.BlockSpec((rows, d), lambda i, j: (0, 0)),
            pl.BlockSpec((None, d, bn), lambda i, j: (i, 0, j)),
            pl.BlockSpec((None, 1, bn), lambda i, j: (i, 0, j)),
        ],
        out_specs=pl.BlockSpec((None, rows, bn), lambda i, j: (i, 0, j)),
        out_shape=jax.ShapeDtypeStruct((depth, rows, n), F32),
        compiler_params=pltpu.CompilerParams(
            dimension_semantics=("arbitrary", "arbitrary"), vmem_limit_bytes=VMEM_LIMIT),
    )(cond, ada_w, ada_b.reshape(depth, 1, n))


def _mod_spec(mod, layer):
    if mod.shape[1] == 1:
        return pl.BlockSpec((None, None, 6, mod.shape[3]), lambda t, b: (layer, 0, 0, 0))
    return pl.BlockSpec((None, None, 6, mod.shape[3]), lambda t, b: (layer, b, 0, 0))


def _layer_spec(arr, layer):
    nd = arr.ndim - 1
    return pl.BlockSpec((None,) + arr.shape[1:], lambda t, b: (layer,) + (0,) * nd,
                        pipeline_mode=pl.Buffered(1))


def _latent_batch(t, b, nt, n_batch):
    return jnp.where(t < nt, b, n_batch - 1)


def _tile_spec(tm, d, nt, n_batch):
    return pl.BlockSpec((None, tm, d), lambda t, b: (_latent_batch(t, b, nt, n_batch), jnp.minimum(t, nt - 1), 0))


def _ctx_spec(rows, d, nt):
    return pl.BlockSpec((None, rows, d), lambda t, b: (jnp.where(t == nt, b, 0), 0, 0))


def _conv_tile(xe, mod_ref, g_ref, w_ref, cw_ref, o_ref, *, n_sub, pad_first, pad_last):
    d = xe.shape[1]
    shift = mod_ref[0:1, :]
    scale = mod_ref[1:2, :]
    sub = (xe.shape[0] - 2 * HALO) // n_sub
    n_ext = sub + 2 * HALO
    row = lax.broadcasted_iota(jnp.int32, (n_ext, 1), 0)

    def project(i):
        he = _norm_mod(xe[i * sub:i * sub + n_ext], g_ref[...], shift, scale).astype(BF16)
        return he, jnp.dot(he, w_ref[:, d:], preferred_element_type=F32)

    def gate_conv(i, he, cv):
        u = cv[:, :d] * cv[:, d:]
        if i == 0:
            u = jnp.where((row >= HALO) | jnp.logical_not(pad_first), u, 0.0)
        if i == n_sub - 1:
            u = jnp.where((row < sub + HALO) | jnp.logical_not(pad_last), u, 0.0)
        u_prev = pltpu.roll(u, 1, axis=0)[HALO:HALO + sub]
        u_next = pltpu.roll(u, n_ext - 1, axis=0)[HALO:HALO + sub]
        conv = u_prev * cw_ref[0:1, :] + u[HALO:HALO + sub] * cw_ref[1:2, :] + u_next * cw_ref[2:3, :]
        bg = jnp.dot(he[HALO:HALO + sub], w_ref[:, :d], preferred_element_type=F32)
        o_ref[i * sub:(i + 1) * sub, :] = (bg * conv).astype(BF16)

    projected = [project(i) for i in range(n_sub)]
    for i, (he, cv) in enumerate(projected):
        gate_conv(i, he, cv)


def _conv_pre_kernel(x_ref, xp_ref, xn_ref, c_ref, modx_ref, modc_ref, g_ref, w_ref, cw_ref, *rest, nt, n_sub):
    n_cast = (len(rest) - 2) // 2
    cast_in = rest[:n_cast]
    y_ref, yc_ref = rest[n_cast:n_cast + 2]
    cast_out = rest[n_cast + 2:]
    t = pl.program_id(0)

    @pl.when(t < nt)
    def _():
        xe = jnp.concatenate([xp_ref[...], x_ref[...], xn_ref[...]], axis=0)
        _conv_tile(xe, modx_ref, g_ref, w_ref, cw_ref, y_ref, n_sub=n_sub,
                   pad_first=t == 0, pad_last=t == nt - 1)
        for src, dst in zip(cast_in, cast_out):
            dst[...] = src[...].astype(BF16)

    @pl.when(t == nt)
    def _():
        z = jnp.zeros((HALO, c_ref.shape[1]), F32)
        xe = jnp.concatenate([z, c_ref[...], z], axis=0)
        _conv_tile(xe, modc_ref, g_ref, w_ref, cw_ref, yc_ref, n_sub=1, pad_first=True, pad_last=True)


def _conv_pre_call(x, ctx, mod_x, mod_c, g, w_in, conv_w, layer, mixer, tm, n_sub, to_cast):
    b, l, d = x.shape
    lc = ctx.shape[1]
    nt = l // tm
    hb = tm // HALO
    last_hb = l // HALO - 1
    n_steps = nt * b
    kern = functools.partial(_conv_pre_kernel, nt=nt, n_sub=n_sub)

    def prev_halo(t, i):
        return (_latent_batch(t, i, nt, b), jnp.maximum(jnp.minimum(t, nt - 1) * hb - 1, 0), 0)

    def next_halo(t, i):
        return (_latent_batch(t, i, nt, b), jnp.minimum((jnp.minimum(t, nt - 1) + 1) * hb, last_hb), 0)

    def cast_block(t, i):
        return (jnp.minimum(t * b + i, n_steps - 1), 0)

    flat = [w.reshape(-1, w.shape[-1]) for w in to_cast]
    cast_specs = [pl.BlockSpec((w.shape[0] // n_steps, w.shape[1]), cast_block) for w in flat]
    assert all(w.shape[0] % (n_steps * 16) == 0 for w in flat)
    outs = pl.pallas_call(
        kern,
        grid=(nt + 1, b),
        in_specs=[
            _tile_spec(tm, d, nt, b),
            pl.BlockSpec((None, HALO, d), prev_halo),
            pl.BlockSpec((None, HALO, d), next_halo),
            _ctx_spec(lc, d, nt),
            _mod_spec(mod_x, layer),
            _mod_spec(mod_c, layer),
            _layer_spec(g, layer),
            _layer_spec(w_in, mixer),
            _layer_spec(conv_w, mixer),
        ] + cast_specs,
        out_specs=[_tile_spec(tm, d, nt, b), _ctx_spec(lc, d, nt)] + cast_specs,
        out_shape=[jax.ShapeDtypeStruct((b, l, d), BF16), jax.ShapeDtypeStruct((b, lc, d), BF16)]
        + [jax.ShapeDtypeStruct(w.shape, BF16) for w in flat],
        compiler_params=pltpu.CompilerParams(
            dimension_semantics=("arbitrary", "arbitrary"), vmem_limit_bytes=VMEM_LIMIT),
    )(x, x, x, ctx, mod_x, mod_c, g, w_in, conv_w, *flat)
    return outs[0], outs[1], [o.reshape(w.shape) for o, w in zip(outs[2:], to_cast)]


def _proj_tile(x_ref, mod_ref, g_ref, w_ref, o_ref, *, col0, n_scaled, q_scale, n_sub):
    sub = x_ref.shape[0] // n_sub
    for i in range(n_sub):
        rs = slice(i * sub, (i + 1) * sub)
        h = _norm_mod(x_ref[rs, :], g_ref[...], mod_ref[0:1, :], mod_ref[1:2, :]).astype(BF16)
        if n_scaled:
            o_ref[rs, :n_scaled] = (
                jnp.dot(h, w_ref[:, col0:col0 + n_scaled], preferred_element_type=F32) * q_scale).astype(BF16)
        o_ref[rs, n_scaled:] = jnp.dot(h, w_ref[:, col0 + n_scaled:], preferred_element_type=F32).astype(BF16)


def _proj_pre_kernel(x_ref, c_ref, modx_ref, modc_ref, g_ref, w_ref, qkv_ref, ckv_ref, *, nt, d, q_scale, n_sub):
    t = pl.program_id(0)

    @pl.when(t < nt)
    def _():
        _proj_tile(x_ref, modx_ref, g_ref, w_ref, qkv_ref, col0=0, n_scaled=d, q_scale=q_scale, n_sub=n_sub)

    @pl.when(t == nt)
    def _():
        _proj_tile(c_ref, modc_ref, g_ref, w_ref, ckv_ref, col0=d, n_scaled=0, q_scale=1.0, n_sub=1)


def _proj_pre_call(x, ctx, mod_x, mod_c, g, w, layer, mixer, tm, q_scale, n_sub):
    b, l, d = x.shape
    lc = ctx.shape[1]
    n = w.shape[2]
    nt = l // tm
    kern = functools.partial(_proj_pre_kernel, nt=nt, d=d, q_scale=q_scale, n_sub=n_sub)
    return pl.pallas_call(
        kern,
        grid=(nt + 1, b),
        in_specs=[
            _tile_spec(tm, d, nt, b),
            _ctx_spec(lc, d, nt),
            _mod_spec(mod_x, layer),
            _mod_spec(mod_c, layer),
            _layer_spec(g, layer),
            _layer_spec(w, mixer),
        ],
        out_specs=[_tile_spec(tm, n, nt, b), _ctx_spec(lc, n - d, nt)],
        out_shape=[jax.ShapeDtypeStruct((b, l, n), BF16), jax.ShapeDtypeStruct((b, lc, n - d), BF16)],
        compiler_params=pltpu.CompilerParams(
            dimension_semantics=("arbitrary", "arbitrary"), vmem_limit_bytes=VMEM_LIMIT),
    )(x, ctx, mod_x, mod_c, g, w)


def _post_tile(x_ref, y_ref, mod_ref, g_ref, wo_ref, w1_ref, w2_ref, fg_ref, o_ref, *, ff_chunk, n_sub):
    gate1 = mod_ref[2:3, :]
    shift2 = mod_ref[3:4, :]
    scale2 = mod_ref[4:5, :]
    gate2 = mod_ref[5:6, :]
    ff = w1_ref.shape[1]
    sub = x_ref.shape[0] // n_sub

    def out_proj(i):
        rs = slice(i * sub, (i + 1) * sub)
        return x_ref[rs, :] + gate1 * jnp.dot(y_ref[rs, :], wo_ref[...], preferred_element_type=F32)

    def mlp(x1):
        h = _norm_mod(x1, g_ref[...], shift2, scale2).astype(BF16)
        acc = None
        for f in range(0, ff, ff_chunk):
            a = jnp.dot(h, w1_ref[:, f:f + ff_chunk], preferred_element_type=F32)
            a = jnp.maximum(a, 0.0)
            a = (a * a).astype(BF16)
            part = jnp.dot(a, w2_ref[f:f + ff_chunk, :], preferred_element_type=F32)
            acc = part if acc is None else acc + part
        return x1 + gate2 * acc

    def finish(i, x2):
        if fg_ref is not None:
            inv = lax.rsqrt(jnp.mean(x2 * x2, axis=-1, keepdims=True) + RMS_EPS)
            x2 = (x2 * inv) * fg_ref[...]
        o_ref[i * sub:(i + 1) * sub, :] = x2

    x1_next = out_proj(0)
    for i in range(n_sub):
        x1 = x1_next
        if i + 1 < n_sub:
            x1_next = out_proj(i + 1)
        finish(i, mlp(x1))


def _post_kernel(*refs, nt, ff_chunk, n_sub, with_ctx, final):
    refs = list(refs)
    x_ref, y_ref = refs[:2]
    del refs[:2]
    if with_ctx:
        c_ref, yc_ref = refs[:2]
        del refs[:2]
    modx_ref = refs.pop(0)
    modc_ref = refs.pop(0) if with_ctx else None
    g_ref, wo_ref, w1_ref, w2_ref = refs[:4]
    del refs[:4]
    fg_ref = refs.pop(0) if final else None
    o_ref = refs.pop(0)
    oc_ref = refs.pop(0) if with_ctx else None
    t = pl.program_id(0)

    @pl.when(t < nt)
    def _():
        _post_tile(x_ref, y_ref, modx_ref, g_ref, wo_ref, w1_ref, w2_ref, fg_ref, o_ref,
                   ff_chunk=ff_chunk, n_sub=n_sub)

    if with_ctx:
        @pl.when(t == nt)
        def _():
            _post_tile(c_ref, yc_ref, modc_ref, g_ref, wo_ref, w1_ref, w2_ref, fg_ref, oc_ref,
                       ff_chunk=ff_chunk, n_sub=1)


def _post_call(x, y, ctx, yc, mod_x, mod_c, g, w_out, w1, w2, final_g, layer, mixer, tm, n_sub, ff_chunk=1024):
    b, l, d = x.shape
    nt = l // tm
    with_ctx = ctx is not None
    final = final_g is not None
    kern = functools.partial(_post_kernel, nt=nt, ff_chunk=ff_chunk, n_sub=n_sub, with_ctx=with_ctx, final=final)
    in_specs = [_tile_spec(tm, d, nt, b), _tile_spec(tm, d, nt, b)]
    args = [x, y]
    out_specs = [_tile_spec(tm, d, nt, b)]
    out_shape = [jax.ShapeDtypeStruct((b, l, d), F32)]
    if with_ctx:
        lc = ctx.shape[1]
        in_specs += [_ctx_spec(lc, d, nt), _ctx_spec(lc, d, nt)]
        args += [ctx, yc]
        out_specs.append(_ctx_spec(lc, d, nt))
        out_shape.append(jax.ShapeDtypeStruct((b, lc, d), F32))
    in_specs.append(_mod_spec(mod_x, layer))
    args.append(mod_x)
    if with_ctx:
        in_specs.append(_mod_spec(mod_c, layer))
        args.append(mod_c)
    in_specs += [_layer_spec(g, layer), _layer_spec(w_out, mixer), _layer_spec(w1, layer), _layer_spec(w2, layer)]
    args += [g, w_out, w1, w2]
    if final:
        in_specs.append(_layer_spec(final_g, 0))
        args.append(final_g)
    return pl.pallas_call(
        kern,
        grid=(nt + (1 if with_ctx else 0), b),
        in_specs=in_specs,
        out_specs=out_specs,
        out_shape=out_shape,
        compiler_params=pltpu.CompilerParams(
            dimension_semantics=("arbitrary", "arbitrary"), vmem_limit_bytes=VMEM_LIMIT),
    )(*args)


def _key_row_start(g, rows):
    return min(max(Q_ROWS * g - WIN_ROWS // 2, 0), rows - K_ROWS)


def _key_col_start(cb):
    lo = min(max(Q_COLS * cb - WIN_COLS // 2, 0), GRID_W - WIN_COLS)
    hi = min(max(Q_COLS * (cb + 1) - 1 - WIN_COLS // 2, 0), GRID_W - WIN_COLS) + WIN_COLS
    start = min(lo // HALO * HALO, GRID_W - K_COLS)
    assert start <= lo and hi <= start + K_COLS
    return start


def _window_row(r, rows):
    return min(max(r - WIN_ROWS // 2, 0), rows - WIN_ROWS)


def _visible_tiles(g, blk, rows):
    rel = [_window_row(Q_ROWS * g + BLOCK_ROWS * blk + i, rows) - _key_row_start(g, rows)
           for i in range(BLOCK_ROWS)]
    return min(rel) // ROWS_PER_TILE, -(-(max(rel) + WIN_ROWS) // ROWS_PER_TILE)


def _rpb_lane_rows(rpb):
    nc = rpb.shape[-1]
    placed = [jnp.pad(rpb, ((0, 0), (0, 0), (K_COLS * j, 128 - K_COLS * j - nc))) for j in range(ROWS_PER_TILE)]
    return jnp.stack(placed, axis=2)


def _fill_bias(rv_ref, bias_ref, *, rows):
    n_groups = rows // Q_ROWS
    shape = (Q_COLS, 128)
    qcl = lax.broadcasted_iota(jnp.int32, shape, 0)
    lane = lax.broadcasted_iota(jnp.int32, shape, 1)
    t = lane & (K_COLS - 1)
    neg = jnp.full(shape, NEG_INF, F32)
    for cls, g in enumerate((0, 1, n_groups - 1)):
        kr0 = _key_row_start(g, rows)
        for cb in range(GRID_W // Q_COLS):
            kstart = _key_col_start(cb)
            cs = jnp.clip(qcl + (Q_COLS * cb - WIN_COLS // 2), 0, GRID_W - WIN_COLS) - kstart
            col_ok = (t >= cs) & (t < cs + WIN_COLS)
            delta = kstart - Q_COLS * cb + WIN_COLS - 1
            for hh in range(HEADS_PER_STEP):
                for qr in range(Q_ROWS):
                    r = Q_ROWS * g + qr
                    rs = _window_row(r, rows)
                    t_lo, t_hi = _visible_tiles(g, qr // BLOCK_ROWS, rows)
                    for tile in range(t_lo, t_hi):
                        krow = [kr0 + ROWS_PER_TILE * tile + j for j in range(ROWS_PER_TILE)]
                        seen = [j for j in range(ROWS_PER_TILE) if rs <= krow[j] < rs + WIN_ROWS]
                        dst = (cls, hh, cb, slice(qr * Q_COLS, (qr + 1) * Q_COLS),
                               slice(tile * 128, (tile + 1) * 128))
                        if not seen:
                            bias_ref[dst] = neg
                            continue
                        v = None
                        for j in range(ROWS_PER_TILE):
                            dr = min(max(krow[j] - r + WIN_ROWS - 1, 0), 2 * WIN_ROWS - 2)
                            piece = rv_ref[hh, dr, j:j + 1, :]
                            v = piece if v is None else v + piece
                        rolled = pltpu.roll(jnp.broadcast_to(v * LOG2E, shape), (-delta) % 128, axis=1,
                                            stride=1, stride_axis=0)
                        ok = col_ok & (lane >= K_COLS * seen[0]) & (lane < K_COLS * (seen[-1] + 1))
                        bias_ref[dst] = jnp.where(ok, rolled, neg)


def _attn_kernel(q_ref, k_ref, v_ref, kc_ref, vc_ref, rv_ref, o_ref, bias_ref, kf_ref, vf_ref, *,
                 rows, head_dim):
    n_batch = q_ref.shape[0]
    n_groups = rows // Q_ROWS
    n_cb = GRID_W // Q_COLS
    nq = Q_ROWS * Q_COLS
    nk = K_ROWS * K_COLS

    @pl.when(pl.program_id(1) == 0)
    def _():
        _fill_bias(rv_ref, bias_ref, rows=rows)

    lane = lax.broadcasted_iota(jnp.int32, (1, HEADS_PER_STEP * head_dim), 1)
    nt = (((1,), (1,)), ((), ()))

    converted = {}

    def window(src_ref, dst_ref, bi, g, cb):
        r0 = _key_row_start(g, rows)
        c0 = _key_col_start(cb)
        done = converted.get((id(dst_ref), bi), 0)
        if done < r0 + K_ROWS:
            tok = slice(done * GRID_W, (r0 + K_ROWS) * GRID_W)
            dst_ref[bi, tok, :] = src_ref[bi, tok, :].astype(F32)
            converted[(id(dst_ref), bi)] = r0 + K_ROWS
        pieces = [dst_ref[bi, (r0 + j) * GRID_W + c0:(r0 + j) * GRID_W + c0 + K_COLS, :]
                  for j in range(K_ROWS)]
        return jnp.concatenate(pieces, axis=0).astype(BF16)

    def scores(bi, g, cb):
        pieces = [q_ref[bi, (Q_ROWS * g + qr) * GRID_W + Q_COLS * cb:
                        (Q_ROWS * g + qr) * GRID_W + Q_COLS * (cb + 1), :] for qr in range(Q_ROWS)]
        q = jnp.concatenate(pieces, axis=0)
        zero = jnp.zeros_like(q)
        qh = jnp.concatenate(
            [jnp.where((lane >= hh * head_dim) & (lane < (hh + 1) * head_dim), q, zero)
             for hh in range(HEADS_PER_STEP)], axis=0)
        s_loc = lax.dot_general(qh, window(k_ref, kf_ref, bi, g, cb), nt, preferred_element_type=F32)
        s_ctx = lax.dot_general(qh, kc_ref[bi], nt, preferred_element_type=F32)
        return s_loc, s_ctx

    def attend(bi, g, cb, s_loc, s_ctx):
        cls = 0 if g == 0 else (2 if g == n_groups - 1 else 1)
        p_blocks = []
        for hh in range(HEADS_PER_STEP):
            for blk in range(Q_ROWS // BLOCK_ROWS):
                t_lo, t_hi = _visible_tiles(g, blk, rows)
                nb = BLOCK_ROWS * Q_COLS
                qsl = slice(blk * nb, (blk + 1) * nb)
                rsl = slice(hh * nq + qsl.start, hh * nq + qsl.stop)
                tiles = [s_loc[rsl, t * 128:(t + 1) * 128] + bias_ref[cls, hh, cb, qsl, t * 128:(t + 1) * 128]
                         for t in range(t_lo, t_hi)]
                tiles += [s_ctx[rsl, t * 128:(t + 1) * 128] for t in range(s_ctx.shape[1] // 128)]
                m = jnp.max(functools.reduce(jnp.maximum, tiles), axis=-1, keepdims=True)
                probs = [jnp.exp2(t - m).astype(BF16) for t in tiles]
                n_loc = t_hi - t_lo
                pieces = [jnp.zeros((nb, 128), BF16)] * t_lo + probs[:n_loc]
                pieces += [jnp.zeros((nb, 128), BF16)] * (nk // 128 - t_hi) + probs[n_loc:]
                p_blocks.append(jnp.concatenate(pieces, axis=1))
        p = jnp.concatenate(p_blocks, axis=0)
        vw = jnp.concatenate([window(v_ref, vf_ref, bi, g, cb), jnp.ones((nk, 128), BF16)], axis=1)
        vc = jnp.concatenate([vc_ref[bi], jnp.ones((vc_ref.shape[1], 128), BF16)], axis=1)
        o = (jnp.dot(p[:, :nk], vw, preferred_element_type=F32)
             + jnp.dot(p[:, nk:], vc, preferred_element_type=F32))
        o = o[:, :128] / o[:, 128:]
        out = o[(HEADS_PER_STEP - 1) * nq:]
        for hh in range(HEADS_PER_STEP - 2, -1, -1):
            out = jnp.where(lane < (hh + 1) * head_dim, o[hh * nq:(hh + 1) * nq], out)
        out = out.astype(BF16)
        for qr in range(Q_ROWS):
            tok = (Q_ROWS * g + qr) * GRID_W + Q_COLS * cb
            o_ref[bi, tok:tok + Q_COLS, :] = out[qr * Q_COLS:(qr + 1) * Q_COLS]

    units = [(bi, g, cb) for bi in range(n_batch) for g in range(n_groups) for cb in range(n_cb)]
    nxt = scores(*units[0])
    for u, unit in enumerate(units):
        cur = nxt
        if u + 1 < len(units):
            nxt = scores(*units[u + 1])
        attend(*unit, *cur)


def _attn_call(qkv, ckv, rpb_rows):
    b, s, d3 = qkv.shape
    d = d3 // 3
    ctx_len = ckv.shape[1]
    head_dim = d // N_HEADS
    w = HEADS_PER_STEP * head_dim
    assert w == 128
    n_hp = d // w
    rows = s // GRID_W
    n_dr = rpb_rows.shape[1]
    bb = ATTN_BATCH if b % ATTN_BATCH == 0 else 1
    kern = functools.partial(_attn_kernel, rows=rows, head_dim=head_dim)
    return pl.pallas_call(
        kern,
        grid=(n_hp, b // bb),
        in_specs=[
            pl.BlockSpec((bb, s, w), lambda h, i: (i, 0, h)),
            pl.BlockSpec((bb, s, w), lambda h, i: (i, 0, n_hp + h)),
            pl.BlockSpec((bb, s, w), lambda h, i: (i, 0, 2 * n_hp + h)),
            pl.BlockSpec((bb, ctx_len, w), lambda h, i: (i, 0, h)),
            pl.BlockSpec((bb, ctx_len, w), lambda h, i: (i, 0, n_hp + h)),
            pl.BlockSpec((HEADS_PER_STEP, n_dr, ROWS_PER_TILE, 128), lambda h, i: (h, 0, 0, 0)),
        ],
        out_specs=pl.BlockSpec((bb, s, w), lambda h, i: (i, 0, h)),
        out_shape=jax.ShapeDtypeStruct((b, s, d), BF16),
        scratch_shapes=[
            pltpu.VMEM((3, HEADS_PER_STEP, GRID_W // Q_COLS, Q_ROWS * Q_COLS, K_ROWS * K_COLS), F32),
            pltpu.VMEM((bb, s, w), F32),
            pltpu.VMEM((bb, s, w), F32),
        ],
        compiler_params=pltpu.CompilerParams(
            dimension_semantics=("arbitrary", "arbitrary"), vmem_limit_bytes=VMEM_LIMIT),
    )(qkv, qkv, qkv, ckv, ckv, rpb_rows)


def kernel(x, c, ctx, c_ctx, norm1_g, norm2_g, ada_w, ada_b, conv_in_w, conv_w, conv_out_w,
           attn_qkv_w, attn_rpb, attn_out_w, mlp_w1, mlp_w2, final_g):
    b, s, d = x.shape
    depth = ada_w.shape[0]
    assert depth == 2 and s % (GRID_W * Q_ROWS) == 0 and s // GRID_W >= K_ROWS and d % N_HEADS == 0
    head_dim = d // N_HEADS
    tm_pre, tm_post = 1024, 1024

    pad = (-(b + 1)) % 8
    cond = jnp.concatenate([c, c_ctx[None, :], jnp.zeros((pad, d), F32)], axis=0)
    mods = _ada_call(cond, ada_w, ada_b).reshape(depth, b + 1 + pad, 6, d)
    mod_lat = mods[:, :b]
    mod_ctx = mods[:, b:b + 1]

    n1 = norm1_g.reshape(depth, 1, d)
    n2 = norm2_g.reshape(depth, 1, d)
    fg = final_g.reshape(1, 1, d)
    w_in = conv_in_w.astype(BF16)

    y, yc, (w_co, w1, w2, w_qkv, w_ao) = _conv_pre_call(
        x, ctx, mod_lat, mod_ctx, n1, w_in, conv_w, 0, 0, tm_pre, 2,
        to_cast=(conv_out_w, mlp_w1, mlp_w2, attn_qkv_w, attn_out_w))
    x, ctx = _post_call(x, y, ctx, yc, mod_lat, mod_ctx, n2, w_co, w1, w2, None, 0, 0, tm_post, 2)

    qkv, ckv = _proj_pre_call(x, ctx, mod_lat, mod_ctx, n1, w_qkv, 1, 0, tm_pre, head_dim ** -0.5 * LOG2E, 2)
    o = _attn_call(qkv, ckv, _rpb_lane_rows(attn_rpb[0]))
    (out,) = _post_call(x, o, None, None, mod_lat, None, n2, w_ao, w1, w2, fg, 1, 0, tm_post, 2)
    return out
```

```python
import functools

import jax
import jax.numpy as jnp
from jax import lax
from jax.experimental import pallas as pl
from jax.experimental.pallas import tpu as pltpu

GRID_W = 64
N_HEADS = 16
WIN_ROWS = 8
WIN_COLS = 16
RMS_EPS = 1e-6
NEG_INF = -1e30
LOG2E = 1.4426950408889634

LANES = 128
BF16_ROWS = 16
HALO = 8
HEADS_PER_STEP = 2
Q_ROWS, Q_COLS = 8, 16
K_ROWS, K_COLS = 16, 32
ROWS_PER_TILE = LANES // K_COLS
BLOCK_ROWS = 4
ATTN_BATCH = 4
VMEM_LIMIT = 56 * 1024 * 1024

BF16 = jnp.bfloat16
F32 = jnp.float32


def _norm_mod(v, g, shift, scale):
    inv = lax.rsqrt(jnp.mean(v * v, axis=-1, keepdims=True) + RMS_EPS)
    return ((v * inv) * g) * (1.0 + scale) + shift


def _ada_kernel(cond_ref, w_ref, b_ref, o_ref):
    cnd = cond_ref[...]
    s = (cnd / (1.0 + jnp.exp(-cnd))).astype(BF16)
    o_ref[...] = jnp.dot(s, w_ref[...].astype(BF16), preferred_element_type=F32) + b_ref[...]


def _ada_call(cond, ada_w, ada_b):
    depth, d, n = ada_w.shape
    rows = cond.shape[0]
    bn = 2 * d
    return pl.pallas_call(
        _ada_kernel,
        grid=(depth, n // bn),
        in_specs=[
            pl.BlockSpec((rows, d), lambda i, j: (0, 0)),
            pl.BlockSpec((None, d, bn), lambda i, j: (i, 0, j)),
            pl.BlockSpec((None, 1, bn), lambda i, j: (i, 0, j)),
        ],
        out_specs=pl.BlockSpec((None, rows, bn), lambda i, j: (i, 0, j)),
        out_shape=jax.ShapeDtypeStruct((depth, rows, n), F32),
        compiler_params=pltpu.CompilerParams(
            dimension_semantics=("arbitrary", "arbitrary"), vmem_limit_bytes=VMEM_LIMIT),
    )(cond, ada_w, ada_b.reshape(depth, 1, n))


def _mod_spec(mod, layer):
    if mod.shape[1] == 1:
        return pl.BlockSpec((None, None, 6, mod.shape[3]), lambda t, b: (layer, 0, 0, 0))
    return pl.BlockSpec((None, None, 6, mod.shape[3]), lambda t, b: (layer, b, 0, 0))


def _layer_spec(arr, layer):
    nd = arr.ndim - 1
    return pl.BlockSpec((None,) + arr.shape[1:], lambda t, b: (layer,) + (0,) * nd,
                        pipeline_mode=pl.Buffered(1))


def _latent_batch(t, b, nt, n_batch):
    return jnp.where(t < nt, b, n_batch - 1)


def _tile_spec(tm, d, nt, n_batch):
    return pl.BlockSpec((None, tm, d), lambda t, b: (_latent_batch(t, b, nt, n_batch), jnp.minimum(t, nt - 1), 0))


def _ctx_spec(rows, d, nt):
    return pl.BlockSpec((None, rows, d), lambda t, b: (jnp.where(t == nt, b, 0), 0, 0))


def _conv_tile(xe, mod_ref, g_ref, w_ref, cw_ref, o_ref, *, n_sub, pad_first, pad_last):
    d = xe.shape[1]
    shift = mod_ref[0:1, :]
    scale = mod_ref[1:2, :]
    sub = (xe.shape[0] - 2 * HALO) // n_sub
    n_ext = sub + 2 * HALO
    row = lax.broadcasted_iota(jnp.int32, (n_ext, 1), 0)

    def project(i):
        he = _norm_mod(xe[i * sub:i * sub + n_ext], g_ref[...], shift, scale).astype(BF16)
        return he, jnp.dot(he, w_ref[:, d:], preferred_element_type=F32)

    def gate_conv(i, he, cv):
        u = cv[:, :d] * cv[:, d:]
        if i == 0:
            u = jnp.where((row >= HALO) | jnp.logical_not(pad_first), u, 0.0)
        if i == n_sub - 1:
            u = jnp.where((row < sub + HALO) | jnp.logical_not(pad_last), u, 0.0)
        u_prev = pltpu.roll(u, 1, axis=0)[HALO:HALO + sub]
        u_next = pltpu.roll(u, n_ext - 1, axis=0)[HALO:HALO + sub]
        conv = u_prev * cw_ref[0:1, :] + u[HALO:HALO + sub] * cw_ref[1:2, :] + u_next * cw_ref[2:3, :]
        bg = jnp.dot(he[HALO:HALO + sub], w_ref[:, :d], preferred_element_type=F32)
        o_ref[i * sub:(i + 1) * sub, :] = (bg * conv).astype(BF16)

    projected = [project(i) for i in range(n_sub)]
    for i, (he, cv) in enumerate(projected):
        gate_conv(i, he, cv)


def _conv_pre_kernel(x_ref, xp_ref, xn_ref, c_ref, modx_ref, modc_ref, g_ref, w_ref, cw_ref, *rest, nt, n_sub):
    n_cast = (len(rest) - 2) // 2
    cast_in = rest[:n_cast]
    y_ref, yc_ref = rest[n_cast:n_cast + 2]
    cast_out = rest[n_cast + 2:]
    t = pl.program_id(0)

    @pl.when(t < nt)
    def _():
        xe = jnp.concatenate([xp_ref[...], x_ref[...], xn_ref[...]], axis=0)
        _conv_tile(xe, modx_ref, g_ref, w_ref, cw_ref, y_ref, n_sub=n_sub,
                   pad_first=t == 0, pad_last=t == nt - 1)
        for src, dst in zip(cast_in, cast_out):
            dst[...] = src[...].astype(BF16)

    @pl.when(t == nt)
    def _():
        z = jnp.zeros((HALO, c_ref.shape[1]), F32)
        xe = jnp.concatenate([z, c_ref[...], z], axis=0)
        _conv_tile(xe, modc_ref, g_ref, w_ref, cw_ref, yc_ref, n_sub=1, pad_first=True, pad_last=True)


def _conv_pre_call(x, ctx, mod_x, mod_c, g, w_in, conv_w, layer, mixer, tm, n_sub, to_cast):
    b, l, d = x.shape
    lc = ctx.shape[1]
    nt = l // tm
    hb = tm // HALO
    last_hb = l // HALO - 1
    n_steps = nt * b
    kern = functools.partial(_conv_pre_kernel, nt=nt, n_sub=n_sub)

    def prev_halo(t, i):
        return (_latent_batch(t, i, nt, b), jnp.maximum(jnp.minimum(t, nt - 1) * hb - 1, 0), 0)

    def next_halo(t, i):
        return (_latent_batch(t, i, nt, b), jnp.minimum((jnp.minimum(t, nt - 1) + 1) * hb, last_hb), 0)

    def cast_block(t, i):
        return (jnp.minimum(t * b + i, n_steps - 1), 0)

    flat = [w.reshape(-1, w.shape[-1]) for w in to_cast]
    cast_specs = [pl.BlockSpec((w.shape[0] // n_steps, w.shape[1]), cast_block) for w in flat]
    assert all(w.shape[0] % (n_steps * BF16_ROWS) == 0 for w in flat)
    outs = pl.pallas_call(
        kern,
        grid=(nt + 1, b),
        in_specs=[
            _tile_spec(tm, d, nt, b),
            pl.BlockSpec((None, HALO, d), prev_halo),
            pl.BlockSpec((None, HALO, d), next_halo),
            _ctx_spec(lc, d, nt),
            _mod_spec(mod_x, layer),
            _mod_spec(mod_c, layer),
            _layer_spec(g, layer),
            _layer_spec(w_in, mixer),
            _layer_spec(conv_w, mixer),
        ] + cast_specs,
        out_specs=[_tile_spec(tm, d, nt, b), _ctx_spec(lc, d, nt)] + cast_specs,
        out_shape=[jax.ShapeDtypeStruct((b, l, d), BF16), jax.ShapeDtypeStruct((b, lc, d), BF16)]
        + [jax.ShapeDtypeStruct(w.shape, BF16) for w in flat],
        compiler_params=pltpu.CompilerParams(
            dimension_semantics=("arbitrary", "arbitrary"), vmem_limit_bytes=VMEM_LIMIT),
    )(x, x, x, ctx, mod_x, mod_c, g, w_in, conv_w, *flat)
    return outs[0], outs[1], [o.reshape(w.shape) for o, w in zip(outs[2:], to_cast)]


def _proj_tile(x_ref, mod_ref, g_ref, w_ref, o_ref, *, col0, n_scaled, q_scale, n_sub):
    sub = x_ref.shape[0] // n_sub
    for i in range(n_sub):
        rs = slice(i * sub, (i + 1) * sub)
        h = _norm_mod(x_ref[rs, :], g_ref[...], mod_ref[0:1, :], mod_ref[1:2, :]).astype(BF16)
        if n_scaled:
            o_ref[rs, :n_scaled] = (
                jnp.dot(h, w_ref[:, col0:col0 + n_scaled], preferred_element_type=F32) * q_scale).astype(BF16)
        o_ref[rs, n_scaled:] = jnp.dot(h, w_ref[:, col0 + n_scaled:], preferred_element_type=F32).astype(BF16)


def _proj_pre_kernel(x_ref, c_ref, modx_ref, modc_ref, g_ref, w_ref, qkv_ref, ckv_ref, *, nt, d, q_scale, n_sub):
    t = pl.program_id(0)

    @pl.when(t < nt)
    def _():
        _proj_tile(x_ref, modx_ref, g_ref, w_ref, qkv_ref, col0=0, n_scaled=d, q_scale=q_scale, n_sub=n_sub)

    @pl.when(t == nt)
    def _():
        _proj_tile(c_ref, modc_ref, g_ref, w_ref, ckv_ref, col0=d, n_scaled=0, q_scale=1.0, n_sub=1)


def _proj_pre_call(x, ctx, mod_x, mod_c, g, w, layer, mixer, tm, q_scale, n_sub):
    b, l, d = x.shape
    lc = ctx.shape[1]
    n = w.shape[2]
    nt = l // tm
    kern = functools.partial(_proj_pre_kernel, nt=nt, d=d, q_scale=q_scale, n_sub=n_sub)
    return pl.pallas_call(
        kern,
        grid=(nt + 1, b),
        in_specs=[
            _tile_spec(tm, d, nt, b),
            _ctx_spec(lc, d, nt),
            _mod_spec(mod_x, layer),
            _mod_spec(mod_c, layer),
            _layer_spec(g, layer),
            _layer_spec(w, mixer),
        ],
        out_specs=[_tile_spec(tm, n, nt, b), _ctx_spec(lc, n - d, nt)],
        out_shape=[jax.ShapeDtypeStruct((b, l, n), BF16), jax.ShapeDtypeStruct((b, lc, n - d), BF16)],
        compiler_params=pltpu.CompilerParams(
            dimension_semantics=("arbitrary", "arbitrary"), vmem_limit_bytes=VMEM_LIMIT),
    )(x, ctx, mod_x, mod_c, g, w)


def _post_tile(x_ref, y_ref, mod_ref, g_ref, wo_ref, w1_ref, w2_ref, fg_ref, o_ref, *, ff_chunk, n_sub):
    gate1 = mod_ref[2:3, :]
    shift2 = mod_ref[3:4, :]
    scale2 = mod_ref[4:5, :]
    gate2 = mod_ref[5:6, :]
    ff = w1_ref.shape[1]
    sub = x_ref.shape[0] // n_sub

    def out_proj(i):
        rs = slice(i * sub, (i + 1) * sub)
        return x_ref[rs, :] + gate1 * jnp.dot(y_ref[rs, :], wo_ref[...], preferred_element_type=F32)

    def mlp(x1):
        h = _norm_mod(x1, g_ref[...], shift2, scale2).astype(BF16)
        acc = None
        for f in range(0, ff, ff_chunk):
            a = jnp.dot(h, w1_ref[:, f:f + ff_chunk], preferred_element_type=F32)
            a = jnp.maximum(a, 0.0)
            a = (a * a).astype(BF16)
            part = jnp.dot(a, w2_ref[f:f + ff_chunk, :], preferred_element_type=F32)
            acc = part if acc is None else acc + part
        return x1 + gate2 * acc

    def finish(i, x2):
        if fg_ref is not None:
            inv = lax.rsqrt(jnp.mean(x2 * x2, axis=-1, keepdims=True) + RMS_EPS)
            x2 = (x2 * inv) * fg_ref[...]
        o_ref[i * sub:(i + 1) * sub, :] = x2

    x1_next = out_proj(0)
    for i in range(n_sub):
        x1 = x1_next
        if i + 1 < n_sub:
            x1_next = out_proj(i + 1)
        finish(i, mlp(x1))


def _post_kernel(*refs, nt, ff_chunk, n_sub, with_ctx, final):
    refs = list(refs)
    x_ref, y_ref = refs[:2]
    del refs[:2]
    if with_ctx:
        c_ref, yc_ref = refs[:2]
        del refs[:2]
    modx_ref = refs.pop(0)
    modc_ref = refs.pop(0) if with_ctx else None
    g_ref, wo_ref, w1_ref, w2_ref = refs[:4]
    del refs[:4]
    fg_ref = refs.pop(0) if final else None
    o_ref = refs.pop(0)
    oc_ref = refs.pop(0) if with_ctx else None
    t = pl.program_id(0)

    @pl.when(t < nt)
    def _():
        _post_tile(x_ref, y_ref, modx_ref, g_ref, wo_ref, w1_ref, w2_ref, fg_ref, o_ref,
                   ff_chunk=ff_chunk, n_sub=n_sub)

    if with_ctx:
        @pl.when(t == nt)
        def _():
            _post_tile(c_ref, yc_ref, modc_ref, g_ref, wo_ref, w1_ref, w2_ref, fg_ref, oc_ref,
                       ff_chunk=ff_chunk, n_sub=1)


def _post_call(x, y, ctx, yc, mod_x, mod_c, g, w_out, w1, w2, final_g, layer, mixer, tm, n_sub, ff_chunk=1024):
    b, l, d = x.shape
    nt = l // tm
    with_ctx = ctx is not None
    final = final_g is not None
    kern = functools.partial(_post_kernel, nt=nt, ff_chunk=ff_chunk, n_sub=n_sub, with_ctx=with_ctx, final=final)
    in_specs = [_tile_spec(tm, d, nt, b), _tile_spec(tm, d, nt, b)]
    args = [x, y]
    out_specs = [_tile_spec(tm, d, nt, b)]
    out_shape = [jax.ShapeDtypeStruct((b, l, d), F32)]
    if with_ctx:
        lc = ctx.shape[1]
        in_specs += [_ctx_spec(lc, d, nt), _ctx_spec(lc, d, nt)]
        args += [ctx, yc]
        out_specs.append(_ctx_spec(lc, d, nt))
        out_shape.append(jax.ShapeDtypeStruct((b, lc, d), F32))
    in_specs.append(_mod_spec(mod_x, layer))
    args.append(mod_x)
    if with_ctx:
        in_specs.append(_mod_spec(mod_c, layer))
        args.append(mod_c)
    in_specs += [_layer_spec(g, layer), _layer_spec(w_out, mixer), _layer_spec(w1, layer), _layer_spec(w2, layer)]
    args += [g, w_out, w1, w2]
    if final:
        in_specs.append(_layer_spec(final_g, 0))
        args.append(final_g)
    return pl.pallas_call(
        kern,
        grid=(nt + (1 if with_ctx else 0), b),
        in_specs=in_specs,
        out_specs=out_specs,
        out_shape=out_shape,
        compiler_params=pltpu.CompilerParams(
            dimension_semantics=("arbitrary", "arbitrary"), vmem_limit_bytes=VMEM_LIMIT),
    )(*args)


def _key_row_start(g, rows):
    return min(max(Q_ROWS * g - WIN_ROWS // 2, 0), rows - K_ROWS)


def _key_col_start(cb):
    lo = min(max(Q_COLS * cb - WIN_COLS // 2, 0), GRID_W - WIN_COLS)
    hi = min(max(Q_COLS * (cb + 1) - 1 - WIN_COLS // 2, 0), GRID_W - WIN_COLS) + WIN_COLS
    start = min(lo // HALO * HALO, GRID_W - K_COLS)
    assert start <= lo and hi <= start + K_COLS
    return start


def _window_row(r, rows):
    return min(max(r - WIN_ROWS // 2, 0), rows - WIN_ROWS)


def _visible_tiles(g, blk, rows):
    rel = [_window_row(Q_ROWS * g + BLOCK_ROWS * blk + i, rows) - _key_row_start(g, rows)
           for i in range(BLOCK_ROWS)]
    return min(rel) // ROWS_PER_TILE, -(-(max(rel) + WIN_ROWS) // ROWS_PER_TILE)


def _rpb_lane_rows(rpb):
    nc = rpb.shape[-1]
    placed = [jnp.pad(rpb, ((0, 0), (0, 0), (K_COLS * j, LANES - K_COLS * j - nc))) for j in range(ROWS_PER_TILE)]
    return jnp.stack(placed, axis=2)


def _fill_bias(rv_ref, bias_ref, *, rows):
    n_groups = rows // Q_ROWS
    shape = (Q_COLS, LANES)
    qcl = lax.broadcasted_iota(jnp.int32, shape, 0)
    lane = lax.broadcasted_iota(jnp.int32, shape, 1)
    t = lane & (K_COLS - 1)
    neg = jnp.full(shape, NEG_INF, F32)
    for cls, g in enumerate((0, 1, n_groups - 1)):
        kr0 = _key_row_start(g, rows)
        for cb in range(GRID_W // Q_COLS):
            kstart = _key_col_start(cb)
            cs = jnp.clip(qcl + (Q_COLS * cb - WIN_COLS // 2), 0, GRID_W - WIN_COLS) - kstart
            col_ok = (t >= cs) & (t < cs + WIN_COLS)
            delta = kstart - Q_COLS * cb + WIN_COLS - 1
            for hh in range(HEADS_PER_STEP):
                for qr in range(Q_ROWS):
                    r = Q_ROWS * g + qr
                    rs = _window_row(r, rows)
                    t_lo, t_hi = _visible_tiles(g, qr // BLOCK_ROWS, rows)
                    for tile in range(t_lo, t_hi):
                        krow = [kr0 + ROWS_PER_TILE * tile + j for j in range(ROWS_PER_TILE)]
                        seen = [j for j in range(ROWS_PER_TILE) if rs <= krow[j] < rs + WIN_ROWS]
                        dst = (cls, hh, cb, slice(qr * Q_COLS, (qr + 1) * Q_COLS),
                               slice(tile * LANES, (tile + 1) * LANES))
                        if not seen:
                            bias_ref[dst] = neg
                            continue
                        v = None
                        for j in range(ROWS_PER_TILE):
                            dr = min(max(krow[j] - r + WIN_ROWS - 1, 0), 2 * WIN_ROWS - 2)
                            piece = rv_ref[hh, dr, j:j + 1, :]
                            v = piece if v is None else v + piece
                        rolled = pltpu.roll(jnp.broadcast_to(v * LOG2E, shape), (-delta) % LANES, axis=1,
                                            stride=1, stride_axis=0)
                        ok = col_ok & (lane >= K_COLS * seen[0]) & (lane < K_COLS * (seen[-1] + 1))
                        bias_ref[dst] = jnp.where(ok, rolled, neg)


def _attn_kernel(q_ref, k_ref, v_ref, kc_ref, vc_ref, rv_ref, o_ref, bias_ref, kf_ref, vf_ref, *,
                 rows, head_dim):
    n_batch = q_ref.shape[0]
    n_groups = rows // Q_ROWS
    n_cb = GRID_W // Q_COLS
    nq = Q_ROWS * Q_COLS
    nk = K_ROWS * K_COLS

    @pl.when(pl.program_id(1) == 0)
    def _():
        _fill_bias(rv_ref, bias_ref, rows=rows)

    lane = lax.broadcasted_iota(jnp.int32, (1, HEADS_PER_STEP * head_dim), 1)
    nt = (((1,), (1,)), ((), ()))

    converted = {}

    def window(src_ref, dst_ref, bi, g, cb):
        r0 = _key_row_start(g, rows)
        c0 = _key_col_start(cb)
        done = converted.get((id(dst_ref), bi), 0)
        if done < r0 + K_ROWS:
            tok = slice(done * GRID_W, (r0 + K_ROWS) * GRID_W)
            dst_ref[bi, tok, :] = src_ref[bi, tok, :].astype(F32)
            converted[(id(dst_ref), bi)] = r0 + K_ROWS
        pieces = [dst_ref[bi, (r0 + j) * GRID_W + c0:(r0 + j) * GRID_W + c0 + K_COLS, :]
                  for j in range(K_ROWS)]
        return jnp.concatenate(pieces, axis=0).astype(BF16)

    def scores(bi, g, cb):
        pieces = [q_ref[bi, (Q_ROWS * g + qr) * GRID_W + Q_COLS * cb:
                        (Q_ROWS * g + qr) * GRID_W + Q_COLS * (cb + 1), :] for qr in range(Q_ROWS)]
        q = jnp.concatenate(pieces, axis=0)
        zero = jnp.zeros_like(q)
        qh = jnp.concatenate(
            [jnp.where((lane >= hh * head_dim) & (lane < (hh + 1) * head_dim), q, zero)
             for hh in range(HEADS_PER_STEP)], axis=0)
        s_loc = lax.dot_general(qh, window(k_ref, kf_ref, bi, g, cb), nt, preferred_element_type=F32)
        s_ctx = lax.dot_general(qh, kc_ref[bi], nt, preferred_element_type=F32)
        return s_loc, s_ctx

    def attend(bi, g, cb, s_loc, s_ctx):
        cls = 0 if g == 0 else (2 if g == n_groups - 1 else 1)
        p_blocks = []
        for hh in range(HEADS_PER_STEP):
            for blk in range(Q_ROWS // BLOCK_ROWS):
                t_lo, t_hi = _visible_tiles(g, blk, rows)
                nb = BLOCK_ROWS * Q_COLS
                qsl = slice(blk * nb, (blk + 1) * nb)
                rsl = slice(hh * nq + qsl.start, hh * nq + qsl.stop)
                tiles = [s_loc[rsl, t * LANES:(t + 1) * LANES] + bias_ref[cls, hh, cb, qsl, t * LANES:(t + 1) * LANES]
                         for t in range(t_lo, t_hi)]
                tiles += [s_ctx[rsl, t * LANES:(t + 1) * LANES] for t in range(s_ctx.shape[1] // LANES)]
                m = jnp.max(functools.reduce(jnp.maximum, tiles), axis=-1, keepdims=True)
                probs = [jnp.exp2(t - m).astype(BF16) for t in tiles]
                n_loc = t_hi - t_lo
                pieces = [jnp.zeros((nb, LANES), BF16)] * t_lo + probs[:n_loc]
                pieces += [jnp.zeros((nb, LANES), BF16)] * (nk // LANES - t_hi) + probs[n_loc:]
                p_blocks.append(jnp.concatenate(pieces, axis=1))
        p = jnp.concatenate(p_blocks, axis=0)
        vw = jnp.concatenate([window(v_ref, vf_ref, bi, g, cb), jnp.ones((nk, LANES), BF16)], axis=1)
        vc = jnp.concatenate([vc_ref[bi], jnp.ones((vc_ref.shape[1], LANES), BF16)], axis=1)
        o = (jnp.dot(p[:, :nk], vw, preferred_element_type=F32)
             + jnp.dot(p[:, nk:], vc, preferred_element_type=F32))
        o = o[:, :LANES] / o[:, LANES:]
        out = o[(HEADS_PER_STEP - 1) * nq:]
        for hh in range(HEADS_PER_STEP - 2, -1, -1):
            out = jnp.where(lane < (hh + 1) * head_dim, o[hh * nq:(hh + 1) * nq], out)
        out = out.astype(BF16)
        for qr in range(Q_ROWS):
            tok = (Q_ROWS * g + qr) * GRID_W + Q_COLS * cb
            o_ref[bi, tok:tok + Q_COLS, :] = out[qr * Q_COLS:(qr + 1) * Q_COLS]

    units = [(bi, g, cb) for bi in range(n_batch) for g in range(n_groups) for cb in range(n_cb)]
    nxt = scores(*units[0])
    for u, unit in enumerate(units):
        cur = nxt
        if u + 1 < len(units):
            nxt = scores(*units[u + 1])
        attend(*unit, *cur)


def _attn_call(qkv, ckv, rpb_rows):
    b, s, d3 = qkv.shape
    d = d3 // 3
    ctx_len = ckv.shape[1]
    head_dim = d // N_HEADS
    w = HEADS_PER_STEP * head_dim
    assert w == LANES
    n_hp = d // w
    rows = s // GRID_W
    n_dr = rpb_rows.shape[1]
    bb = ATTN_BATCH if b % ATTN_BATCH == 0 else 1
    kern = functools.partial(_attn_kernel, rows=rows, head_dim=head_dim)
    return pl.pallas_call(
        kern,
        grid=(n_hp, b // bb),
        in_specs=[
            pl.BlockSpec((bb, s, w), lambda h, i: (i, 0, h)),
            pl.BlockSpec((bb, s, w), lambda h, i: (i, 0, n_hp + h)),
            pl.BlockSpec((bb, s, w), lambda h, i: (i, 0, 2 * n_hp + h)),
            pl.BlockSpec((bb, ctx_len, w), lambda h, i: (i, 0, h)),
            pl.BlockSpec((bb, ctx_len, w), lambda h, i: (i, 0, n_hp + h)),
            pl.BlockSpec((HEADS_PER_STEP, n_dr, ROWS_PER_TILE, LANES), lambda h, i: (h, 0, 0, 0)),
        ],
        out_specs=pl.BlockSpec((bb, s, w), lambda h, i: (i, 0, h)),
        out_shape=jax.ShapeDtypeStruct((b, s, d), BF16),
        scratch_shapes=[
            pltpu.VMEM((3, HEADS_PER_STEP, GRID_W // Q_COLS, Q_ROWS * Q_COLS, K_ROWS * K_COLS), F32),
            pltpu.VMEM((bb, s, w), F32),
            pltpu.VMEM((bb, s, w), F32),
        ],
        compiler_params=pltpu.CompilerParams(
            dimension_semantics=("arbitrary", "arbitrary"), vmem_limit_bytes=VMEM_LIMIT),
    )(qkv, qkv, qkv, ckv, ckv, rpb_rows)


def kernel(x, c, ctx, c_ctx, norm1_g, norm2_g, ada_w, ada_b, conv_in_w, conv_w, conv_out_w,
           attn_qkv_w, attn_rpb, attn_out_w, mlp_w1, mlp_w2, final_g):
    b, s, d = x.shape
    depth = ada_w.shape[0]
    assert depth == 2 and s % (GRID_W * Q_ROWS) == 0 and s // GRID_W >= K_ROWS and d % N_HEADS == 0
    head_dim = d // N_HEADS
    tm_pre, tm_post = 1024, 1024

    pad = (-(b + 1)) % 8
    cond = jnp.concatenate([c, c_ctx[None, :], jnp.zeros((pad, d), F32)], axis=0)
    mods = _ada_call(cond, ada_w, ada_b).reshape(depth, b + 1 + pad, 6, d)
    mod_lat = mods[:, :b]
    mod_ctx = mods[:, b:b + 1]

    n1 = norm1_g.reshape(depth, 1, d)
    n2 = norm2_g.reshape(depth, 1, d)
    fg = final_g.reshape(1, 1, d)
    w_in = conv_in_w.astype(BF16)

    y, yc, (w_co, w1, w2, w_qkv, w_ao) = _conv_pre_call(
        x, ctx, mod_lat, mod_ctx, n1, w_in, conv_w, 0, 0, tm_pre, 2,
        to_cast=(conv_out_w, mlp_w1, mlp_w2, attn_qkv_w, attn_out_w))
    x, ctx = _post_call(x, y, ctx, yc, mod_lat, mod_ctx, n2, w_co, w1, w2, None, 0, 0, tm_post, 2)

    qkv, ckv = _proj_pre_call(x, ctx, mod_lat, mod_ctx, n1, w_qkv, 1, 0, tm_pre, head_dim ** -0.5 * LOG2E, 2)
    o = _attn_call(qkv, ckv, _rpb_lane_rows(attn_rpb[0]))
    (out,) = _post_call(x, o, None, None, mod_lat, None, n2, w_ao, w1, w2, fg, 1, 0, tm_post, 2)
    return out
```

```python
import functools

import jax
import jax.numpy as jnp
from jax import lax
from jax.experimental import pallas as pl
from jax.experimental.pallas import tpu as pltpu

GRID_W = 64
N_HEADS = 16
WIN_ROWS = 8
WIN_COLS = 16
RMS_EPS = 1e-6
NEG_INF = -1e30
LOG2E = 1.4426950408889634

LANES = 128
BF16_ROWS = 16
HALO = 8
HEADS_PER_STEP = 2
Q_ROWS, Q_COLS = 8, 16
K_ROWS, K_COLS = 16, 32
ROWS_PER_TILE = LANES // K_COLS
BLOCK_ROWS = 4
ATTN_BATCH = 8
VMEM_LIMIT = 56 * 1024 * 1024

BF16 = jnp.bfloat16
F32 = jnp.float32


def _norm_mod(v, g, shift, scale):
    inv = lax.rsqrt(jnp.mean(v * v, axis=-1, keepdims=True) + RMS_EPS)
    return ((v * inv) * g) * (1.0 + scale) + shift


def _ada_kernel(cond_ref, w_ref, b_ref, o_ref):
    cnd = cond_ref[...]
    s = (cnd / (1.0 + jnp.exp(-cnd))).astype(BF16)
    o_ref[...] = jnp.dot(s, w_ref[...].astype(BF16), preferred_element_type=F32) + b_ref[...]


def _ada_call(cond, ada_w, ada_b):
    depth, d, n = ada_w.shape
    rows = cond.shape[0]
    bn = 2 * d
    return pl.pallas_call(
        _ada_kernel,
        grid=(depth, n // bn),
        in_specs=[
            pl.BlockSpec((rows, d), lambda i, j: (0, 0)),
            pl.BlockSpec((None, d, bn), lambda i, j: (i, 0, j)),
            pl.BlockSpec((None, 1, bn), lambda i, j: (i, 0, j)),
        ],
        out_specs=pl.BlockSpec((None, rows, bn), lambda i, j: (i, 0, j)),
        out_shape=jax.ShapeDtypeStruct((depth, rows, n), F32),
        compiler_params=pltpu.CompilerParams(
            dimension_semantics=("arbitrary", "arbitrary"), vmem_limit_bytes=VMEM_LIMIT),
    )(cond, ada_w, ada_b.reshape(depth, 1, n))


def _mod_spec(mod, layer):
    if mod.shape[1] == 1:
        return pl.BlockSpec((None, None, 6, mod.shape[3]), lambda t, b: (layer, 0, 0, 0))
    return pl.BlockSpec((None, None, 6, mod.shape[3]), lambda t, b: (layer, b, 0, 0))


def _layer_spec(arr, layer):
    nd = arr.ndim - 1
    return pl.BlockSpec((None,) + arr.shape[1:], lambda t, b: (layer,) + (0,) * nd,
                        pipeline_mode=pl.Buffered(1))


def _latent_batch(t, b, nt, n_batch):
    return jnp.where(t < nt, b, n_batch - 1)


def _tile_spec(tm, d, nt, n_batch):
    return pl.BlockSpec((None, tm, d), lambda t, b: (_latent_batch(t, b, nt, n_batch), jnp.minimum(t, nt - 1), 0))


def _ctx_spec(rows, d, nt):
    return pl.BlockSpec((None, rows, d), lambda t, b: (jnp.where(t == nt, b, 0), 0, 0))


def _conv_tile(xe, mod_ref, g_ref, w_ref, cw_ref, o_ref, *, n_sub, pad_first, pad_last):
    d = xe.shape[1]
    shift = mod_ref[0:1, :]
    scale = mod_ref[1:2, :]
    sub = (xe.shape[0] - 2 * HALO) // n_sub
    n_ext = sub + 2 * HALO
    row = lax.broadcasted_iota(jnp.int32, (n_ext, 1), 0)

    def project(i):
        he = _norm_mod(xe[i * sub:i * sub + n_ext], g_ref[...], shift, scale).astype(BF16)
        return he, jnp.dot(he, w_ref[:, d:], preferred_element_type=F32)

    def gate_conv(i, he, cv):
        u = cv[:, :d] * cv[:, d:]
        if i == 0:
            u = jnp.where((row >= HALO) | jnp.logical_not(pad_first), u, 0.0)
        if i == n_sub - 1:
            u = jnp.where((row < sub + HALO) | jnp.logical_not(pad_last), u, 0.0)
        u_prev = pltpu.roll(u, 1, axis=0)[HALO:HALO + sub]
        u_next = pltpu.roll(u, n_ext - 1, axis=0)[HALO:HALO + sub]
        conv = u_prev * cw_ref[0:1, :] + u[HALO:HALO + sub] * cw_ref[1:2, :] + u_next * cw_ref[2:3, :]
        bg = jnp.dot(he[HALO:HALO + sub], w_ref[:, :d], preferred_element_type=F32)
        o_ref[i * sub:(i + 1) * sub, :] = (bg * conv).astype(BF16)

    projected = [project(i) for i in range(n_sub)]
    for i, (he, cv) in enumerate(projected):
        gate_conv(i, he, cv)


def _conv_pre_kernel(x_ref, xp_ref, xn_ref, c_ref, modx_ref, modc_ref, g_ref, w_ref, cw_ref, *rest, nt, n_sub):
    n_cast = (len(rest) - 2) // 2
    cast_in = rest[:n_cast]
    y_ref, yc_ref = rest[n_cast:n_cast + 2]
    cast_out = rest[n_cast + 2:]
    t = pl.program_id(0)

    @pl.when(t < nt)
    def _():
        xe = jnp.concatenate([xp_ref[...], x_ref[...], xn_ref[...]], axis=0)
        _conv_tile(xe, modx_ref, g_ref, w_ref, cw_ref, y_ref, n_sub=n_sub,
                   pad_first=t == 0, pad_last=t == nt - 1)
        for src, dst in zip(cast_in, cast_out):
            dst[...] = src[...].astype(BF16)

    @pl.when(t == nt)
    def _():
        z = jnp.zeros((HALO, c_ref.shape[1]), F32)
        xe = jnp.concatenate([z, c_ref[...], z], axis=0)
        _conv_tile(xe, modc_ref, g_ref, w_ref, cw_ref, yc_ref, n_sub=1, pad_first=True, pad_last=True)


def _conv_pre_call(x, ctx, mod_x, mod_c, g, w_in, conv_w, layer, mixer, tm, n_sub, to_cast):
    b, l, d = x.shape
    lc = ctx.shape[1]
    nt = l // tm
    hb = tm // HALO
    last_hb = l // HALO - 1
    n_steps = nt * b
    kern = functools.partial(_conv_pre_kernel, nt=nt, n_sub=n_sub)

    def prev_halo(t, i):
        return (_latent_batch(t, i, nt, b), jnp.maximum(jnp.minimum(t, nt - 1) * hb - 1, 0), 0)

    def next_halo(t, i):
        return (_latent_batch(t, i, nt, b), jnp.minimum((jnp.minimum(t, nt - 1) + 1) * hb, last_hb), 0)

    def cast_block(t, i):
        return (jnp.minimum(t * b + i, n_steps - 1), 0)

    flat = [w.reshape(-1, w.shape[-1]) for w in to_cast]
    cast_specs = [pl.BlockSpec((w.shape[0] // n_steps, w.shape[1]), cast_block) for w in flat]
    assert all(w.shape[0] % (n_steps * BF16_ROWS) == 0 for w in flat)
    outs = pl.pallas_call(
        kern,
        grid=(nt + 1, b),
        in_specs=[
            _tile_spec(tm, d, nt, b),
            pl.BlockSpec((None, HALO, d), prev_halo),
            pl.BlockSpec((None, HALO, d), next_halo),
            _ctx_spec(lc, d, nt),
            _mod_spec(mod_x, layer),
            _mod_spec(mod_c, layer),
            _layer_spec(g, layer),
            _layer_spec(w_in, mixer),
            _layer_spec(conv_w, mixer),
        ] + cast_specs,
        out_specs=[_tile_spec(tm, d, nt, b), _ctx_spec(lc, d, nt)] + cast_specs,
        out_shape=[jax.ShapeDtypeStruct((b, l, d), BF16), jax.ShapeDtypeStruct((b, lc, d), BF16)]
        + [jax.ShapeDtypeStruct(w.shape, BF16) for w in flat],
        compiler_params=pltpu.CompilerParams(
            dimension_semantics=("arbitrary", "arbitrary"), vmem_limit_bytes=VMEM_LIMIT),
    )(x, x, x, ctx, mod_x, mod_c, g, w_in, conv_w, *flat)
    return outs[0], outs[1], [o.reshape(w.shape) for o, w in zip(outs[2:], to_cast)]


def _proj_tile(x_ref, mod_ref, g_ref, w_ref, o_ref, *, col0, n_scaled, q_scale, n_sub):
    sub = x_ref.shape[0] // n_sub
    for i in range(n_sub):
        rs = slice(i * sub, (i + 1) * sub)
        h = _norm_mod(x_ref[rs, :], g_ref[...], mod_ref[0:1, :], mod_ref[1:2, :]).astype(BF16)
        if n_scaled:
            o_ref[rs, :n_scaled] = (
                jnp.dot(h, w_ref[:, col0:col0 + n_scaled], preferred_element_type=F32) * q_scale).astype(BF16)
        o_ref[rs, n_scaled:] = jnp.dot(h, w_ref[:, col0 + n_scaled:], preferred_element_type=F32).astype(BF16)


def _proj_pre_kernel(x_ref, c_ref, modx_ref, modc_ref, g_ref, w_ref, qkv_ref, ckv_ref, *, nt, d, q_scale, n_sub):
    t = pl.program_id(0)

    @pl.when(t < nt)
    def _():
        _proj_tile(x_ref, modx_ref, g_ref, w_ref, qkv_ref, col0=0, n_scaled=d, q_scale=q_scale, n_sub=n_sub)

    @pl.when(t == nt)
    def _():
        _proj_tile(c_ref, modc_ref, g_ref, w_ref, ckv_ref, col0=d, n_scaled=0, q_scale=1.0, n_sub=1)


def _proj_pre_call(x, ctx, mod_x, mod_c, g, w, layer, mixer, tm, q_scale, n_sub):
    b, l, d = x.shape
    lc = ctx.shape[1]
    n = w.shape[2]
    nt = l // tm
    kern = functools.partial(_proj_pre_kernel, nt=nt, d=d, q_scale=q_scale, n_sub=n_sub)
    return pl.pallas_call(
        kern,
        grid=(nt + 1, b),
        in_specs=[
            _tile_spec(tm, d, nt, b),
            _ctx_spec(lc, d, nt),
            _mod_spec(mod_x, layer),
            _mod_spec(mod_c, layer),
            _layer_spec(g, layer),
            _layer_spec(w, mixer),
        ],
        out_specs=[_tile_spec(tm, n, nt, b), _ctx_spec(lc, n - d, nt)],
        out_shape=[jax.ShapeDtypeStruct((b, l, n), BF16), jax.ShapeDtypeStruct((b, lc, n - d), BF16)],
        compiler_params=pltpu.CompilerParams(
            dimension_semantics=("arbitrary", "arbitrary"), vmem_limit_bytes=VMEM_LIMIT),
    )(x, ctx, mod_x, mod_c, g, w)


def _post_tile(x_ref, y_ref, mod_ref, g_ref, wo_ref, w1_ref, w2_ref, fg_ref, o_ref, *, ff_chunk, n_sub):
    gate1 = mod_ref[2:3, :]
    shift2 = mod_ref[3:4, :]
    scale2 = mod_ref[4:5, :]
    gate2 = mod_ref[5:6, :]
    ff = w1_ref.shape[1]
    sub = x_ref.shape[0] // n_sub

    def out_proj(i):
        rs = slice(i * sub, (i + 1) * sub)
        return x_ref[rs, :] + gate1 * jnp.dot(y_ref[rs, :], wo_ref[...], preferred_element_type=F32)

    def mlp(x1):
        h = _norm_mod(x1, g_ref[...], shift2, scale2).astype(BF16)
        acc = None
        for f in range(0, ff, ff_chunk):
            a = jnp.dot(h, w1_ref[:, f:f + ff_chunk], preferred_element_type=F32)
            a = jnp.maximum(a, 0.0)
            a = (a * a).astype(BF16)
            part = jnp.dot(a, w2_ref[f:f + ff_chunk, :], preferred_element_type=F32)
            acc = part if acc is None else acc + part
        return x1 + gate2 * acc

    def finish(i, x2):
        if fg_ref is not None:
            inv = lax.rsqrt(jnp.mean(x2 * x2, axis=-1, keepdims=True) + RMS_EPS)
            x2 = (x2 * inv) * fg_ref[...]
        o_ref[i * sub:(i + 1) * sub, :] = x2

    x1_next = out_proj(0)
    for i in range(n_sub):
        x1 = x1_next
        if i + 1 < n_sub:
            x1_next = out_proj(i + 1)
        finish(i, mlp(x1))


def _post_kernel(*refs, nt, ff_chunk, n_sub, with_ctx, final):
    refs = list(refs)
    x_ref, y_ref = refs[:2]
    del refs[:2]
    if with_ctx:
        c_ref, yc_ref = refs[:2]
        del refs[:2]
    modx_ref = refs.pop(0)
    modc_ref = refs.pop(0) if with_ctx else None
    g_ref, wo_ref, w1_ref, w2_ref = refs[:4]
    del refs[:4]
    fg_ref = refs.pop(0) if final else None
    o_ref = refs.pop(0)
    oc_ref = refs.pop(0) if with_ctx else None
    t = pl.program_id(0)

    @pl.when(t < nt)
    def _():
        _post_tile(x_ref, y_ref, modx_ref, g_ref, wo_ref, w1_ref, w2_ref, fg_ref, o_ref,
                   ff_chunk=ff_chunk, n_sub=n_sub)

    if with_ctx:
        @pl.when(t == nt)
        def _():
            _post_tile(c_ref, yc_ref, modc_ref, g_ref, wo_ref, w1_ref, w2_ref, fg_ref, oc_ref,
                       ff_chunk=ff_chunk, n_sub=1)


def _post_call(x, y, ctx, yc, mod_x, mod_c, g, w_out, w1, w2, final_g, layer, mixer, tm, n_sub, ff_chunk=1024):
    b, l, d = x.shape
    nt = l // tm
    with_ctx = ctx is not None
    final = final_g is not None
    kern = functools.partial(_post_kernel, nt=nt, ff_chunk=ff_chunk, n_sub=n_sub, with_ctx=with_ctx, final=final)
    in_specs = [_tile_spec(tm, d, nt, b), _tile_spec(tm, d, nt, b)]
    args = [x, y]
    out_specs = [_tile_spec(tm, d, nt, b)]
    out_shape = [jax.ShapeDtypeStruct((b, l, d), F32)]
    if with_ctx:
        lc = ctx.shape[1]
        in_specs += [_ctx_spec(lc, d, nt), _ctx_spec(lc, d, nt)]
        args += [ctx, yc]
        out_specs.append(_ctx_spec(lc, d, nt))
        out_shape.append(jax.ShapeDtypeStruct((b, lc, d), F32))
    in_specs.append(_mod_spec(mod_x, layer))
    args.append(mod_x)
    if with_ctx:
        in_specs.append(_mod_spec(mod_c, layer))
        args.append(mod_c)
    in_specs += [_layer_spec(g, layer), _layer_spec(w_out, mixer), _layer_spec(w1, layer), _layer_spec(w2, layer)]
    args += [g, w_out, w1, w2]
    if final:
        in_specs.append(_layer_spec(final_g, 0))
        args.append(final_g)
    return pl.pallas_call(
        kern,
        grid=(nt + (1 if with_ctx else 0), b),
        in_specs=in_specs,
        out_specs=out_specs,
        out_shape=out_shape,
        compiler_params=pltpu.CompilerParams(
            dimension_semantics=("arbitrary", "arbitrary"), vmem_limit_bytes=VMEM_LIMIT),
    )(*args)


def _key_row_start(g, rows):
    return min(max(Q_ROWS * g - WIN_ROWS // 2, 0), rows - K_ROWS)


def _key_col_start(cb):
    lo = min(max(Q_COLS * cb - WIN_COLS // 2, 0), GRID_W - WIN_COLS)
    hi = min(max(Q_COLS * (cb + 1) - 1 - WIN_COLS // 2, 0), GRID_W - WIN_COLS) + WIN_COLS
    start = min(lo // HALO * HALO, GRID_W - K_COLS)
    assert start <= lo and hi <= start + K_COLS
    return start


def _window_row(r, rows):
    return min(max(r - WIN_ROWS // 2, 0), rows - WIN_ROWS)


def _visible_tiles(g, blk, rows):
    rel = [_window_row(Q_ROWS * g + BLOCK_ROWS * blk + i, rows) - _key_row_start(g, rows)
           for i in range(BLOCK_ROWS)]
    return min(rel) // ROWS_PER_TILE, -(-(max(rel) + WIN_ROWS) // ROWS_PER_TILE)


def _rpb_lane_rows(rpb):
    nc = rpb.shape[-1]
    placed = [jnp.pad(rpb, ((0, 0), (0, 0), (K_COLS * j, LANES - K_COLS * j - nc))) for j in range(ROWS_PER_TILE)]
    return jnp.stack(placed, axis=2)


def _fill_bias(rv_ref, bias_ref, *, rows):
    n_groups = rows // Q_ROWS
    shape = (Q_COLS, LANES)
    qcl = lax.broadcasted_iota(jnp.int32, shape, 0)
    lane = lax.broadcasted_iota(jnp.int32, shape, 1)
    t = lane & (K_COLS - 1)
    neg = jnp.full(shape, NEG_INF, F32)
    for cls, g in enumerate((0, 1, n_groups - 1)):
        kr0 = _key_row_start(g, rows)
        for cb in range(GRID_W // Q_COLS):
            kstart = _key_col_start(cb)
            cs = jnp.clip(qcl + (Q_COLS * cb - WIN_COLS // 2), 0, GRID_W - WIN_COLS) - kstart
            col_ok = (t >= cs) & (t < cs + WIN_COLS)
            delta = kstart - Q_COLS * cb + WIN_COLS - 1
            for hh in range(HEADS_PER_STEP):
                for qr in range(Q_ROWS):
                    r = Q_ROWS * g + qr
                    rs = _window_row(r, rows)
                    t_lo, t_hi = _visible_tiles(g, qr // BLOCK_ROWS, rows)
                    for tile in range(t_lo, t_hi):
                        krow = [kr0 + ROWS_PER_TILE * tile + j for j in range(ROWS_PER_TILE)]
                        seen = [j for j in range(ROWS_PER_TILE) if rs <= krow[j] < rs + WIN_ROWS]
                        dst = (cls, hh, cb, slice(qr * Q_COLS, (qr + 1) * Q_COLS),
                               slice(tile * LANES, (tile + 1) * LANES))
                        if not seen:
                            bias_ref[dst] = neg
                            continue
                        v = None
                        for j in range(ROWS_PER_TILE):
                            dr = min(max(krow[j] - r + WIN_ROWS - 1, 0), 2 * WIN_ROWS - 2)
                            piece = rv_ref[hh, dr, j:j + 1, :]
                            v = piece if v is None else v + piece
                        rolled = pltpu.roll(jnp.broadcast_to(v * LOG2E, shape), (-delta) % LANES, axis=1,
                                            stride=1, stride_axis=0)
                        ok = col_ok & (lane >= K_COLS * seen[0]) & (lane < K_COLS * (seen[-1] + 1))
                        bias_ref[dst] = jnp.where(ok, rolled, neg)


def _attn_kernel(q_ref, k_ref, v_ref, kc_ref, vc_ref, rv_ref, o_ref, bias_ref, kf_ref, vf_ref, *,
                 rows, head_dim):
    n_batch = q_ref.shape[0]
    n_groups = rows // Q_ROWS
    n_cb = GRID_W // Q_COLS
    nq = Q_ROWS * Q_COLS
    nk = K_ROWS * K_COLS

    @pl.when(pl.program_id(1) == 0)
    def _():
        _fill_bias(rv_ref, bias_ref, rows=rows)

    lane = lax.broadcasted_iota(jnp.int32, (1, HEADS_PER_STEP * head_dim), 1)
    nt = (((1,), (1,)), ((), ()))

    converted = {}

    def window(src_ref, dst_ref, bi, g, cb):
        r0 = _key_row_start(g, rows)
        c0 = _key_col_start(cb)
        done = converted.get((id(dst_ref), bi), 0)
        if done < r0 + K_ROWS:
            tok = slice(done * GRID_W, (r0 + K_ROWS) * GRID_W)
            dst_ref[bi, tok, :] = src_ref[bi, tok, :].astype(F32)
            converted[(id(dst_ref), bi)] = r0 + K_ROWS
        pieces = [dst_ref[bi, (r0 + j) * GRID_W + c0:(r0 + j) * GRID_W + c0 + K_COLS, :]
                  for j in range(K_ROWS)]
        return jnp.concatenate(pieces, axis=0).astype(BF16)

    def scores(bi, g, cb):
        pieces = [q_ref[bi, (Q_ROWS * g + qr) * GRID_W + Q_COLS * cb:
                        (Q_ROWS * g + qr) * GRID_W + Q_COLS * (cb + 1), :] for qr in range(Q_ROWS)]
        q = jnp.concatenate(pieces, axis=0)
        zero = jnp.zeros_like(q)
        qh = jnp.concatenate(
            [jnp.where((lane >= hh * head_dim) & (lane < (hh + 1) * head_dim), q, zero)
             for hh in range(HEADS_PER_STEP)], axis=0)
        s_loc = lax.dot_general(qh, window(k_ref, kf_ref, bi, g, cb), nt, preferred_element_type=F32)
        s_ctx = lax.dot_general(qh, kc_ref[bi], nt, preferred_element_type=F32)
        return s_loc, s_ctx

    def attend(bi, g, cb, s_loc, s_ctx):
        cls = 0 if g == 0 else (2 if g == n_groups - 1 else 1)
        p_blocks = []
        for hh in range(HEADS_PER_STEP):
            for blk in range(Q_ROWS // BLOCK_ROWS):
                t_lo, t_hi = _visible_tiles(g, blk, rows)
                nb = BLOCK_ROWS * Q_COLS
                qsl = slice(blk * nb, (blk + 1) * nb)
                rsl = slice(hh * nq + qsl.start, hh * nq + qsl.stop)
                tiles = [s_loc[rsl, t * LANES:(t + 1) * LANES] + bias_ref[cls, hh, cb, qsl, t * LANES:(t + 1) * LANES]
                         for t in range(t_lo, t_hi)]
                tiles += [s_ctx[rsl, t * LANES:(t + 1) * LANES] for t in range(s_ctx.shape[1] // LANES)]
                m = jnp.max(functools.reduce(jnp.maximum, tiles), axis=-1, keepdims=True)
                probs = [jnp.exp2(t - m).astype(BF16) for t in tiles]
                n_loc = t_hi - t_lo
                pieces = [jnp.zeros((nb, LANES), BF16)] * t_lo + probs[:n_loc]
                pieces += [jnp.zeros((nb, LANES), BF16)] * (nk // LANES - t_hi) + probs[n_loc:]
                p_blocks.append(jnp.concatenate(pieces, axis=1))
        p = jnp.concatenate(p_blocks, axis=0)
        vw = jnp.concatenate([window(v_ref, vf_ref, bi, g, cb), jnp.ones((nk, LANES), BF16)], axis=1)
        vc = jnp.concatenate([vc_ref[bi], jnp.ones((vc_ref.shape[1], LANES), BF16)], axis=1)
        o = (jnp.dot(p[:, :nk], vw, preferred_element_type=F32)
             + jnp.dot(p[:, nk:], vc, preferred_element_type=F32))
        o = o[:, :LANES] / o[:, LANES:]
        out = o[(HEADS_PER_STEP - 1) * nq:]
        for hh in range(HEADS_PER_STEP - 2, -1, -1):
            out = jnp.where(lane < (hh + 1) * head_dim, o[hh * nq:(hh + 1) * nq], out)
        out = out.astype(BF16)
        for qr in range(Q_ROWS):
            tok = (Q_ROWS * g + qr) * GRID_W + Q_COLS * cb
            o_ref[bi, tok:tok + Q_COLS, :] = out[qr * Q_COLS:(qr + 1) * Q_COLS]

    units = [(bi, g, cb) for bi in range(n_batch) for g in range(n_groups) for cb in range(n_cb)]
    nxt = scores(*units[0])
    for u, unit in enumerate(units):
        cur = nxt
        if u + 1 < len(units):
            nxt = scores(*units[u + 1])
        attend(*unit, *cur)


def _attn_call(qkv, ckv, rpb_rows):
    b, s, d3 = qkv.shape
    d = d3 // 3
    ctx_len = ckv.shape[1]
    head_dim = d // N_HEADS
    w = HEADS_PER_STEP * head_dim
    assert w == LANES
    n_hp = d // w
    rows = s // GRID_W
    n_dr = rpb_rows.shape[1]
    bb = ATTN_BATCH if b % ATTN_BATCH == 0 else 1
    kern = functools.partial(_attn_kernel, rows=rows, head_dim=head_dim)
    return pl.pallas_call(
        kern,
        grid=(n_hp, b // bb),
        in_specs=[
            pl.BlockSpec((bb, s, w), lambda h, i: (i, 0, h)),
            pl.BlockSpec((bb, s, w), lambda h, i: (i, 0, n_hp + h)),
            pl.BlockSpec((bb, s, w), lambda h, i: (i, 0, 2 * n_hp + h)),
            pl.BlockSpec((bb, ctx_len, w), lambda h, i: (i, 0, h)),
            pl.BlockSpec((bb, ctx_len, w), lambda h, i: (i, 0, n_hp + h)),
            pl.BlockSpec((HEADS_PER_STEP, n_dr, ROWS_PER_TILE, LANES), lambda h, i: (h, 0, 0, 0)),
        ],
        out_specs=pl.BlockSpec((bb, s, w), lambda h, i: (i, 0, h)),
        out_shape=jax.ShapeDtypeStruct((b, s, d), BF16),
        scratch_shapes=[
            pltpu.VMEM((3, HEADS_PER_STEP, GRID_W // Q_COLS, Q_ROWS * Q_COLS, K_ROWS * K_COLS), F32),
            pltpu.VMEM((bb, s, w), F32),
            pltpu.VMEM((bb, s, w), F32),
        ],
        compiler_params=pltpu.CompilerParams(
            dimension_semantics=("arbitrary", "arbitrary"), vmem_limit_bytes=VMEM_LIMIT),
    )(qkv, qkv, qkv, ckv, ckv, rpb_rows)


def kernel(x, c, ctx, c_ctx, norm1_g, norm2_g, ada_w, ada_b, conv_in_w, conv_w, conv_out_w,
           attn_qkv_w, attn_rpb, attn_out_w, mlp_w1, mlp_w2, final_g):
    b, s, d = x.shape
    depth = ada_w.shape[0]
    assert depth == 2 and s % (GRID_W * Q_ROWS) == 0 and s // GRID_W >= K_ROWS and d % N_HEADS == 0
    head_dim = d // N_HEADS
    tm_pre, tm_post = 1024, 1024

    pad = (-(b + 1)) % 8
    cond = jnp.concatenate([c, c_ctx[None, :], jnp.zeros((pad, d), F32)], axis=0)
    mods = _ada_call(cond, ada_w, ada_b).reshape(depth, b + 1 + pad, 6, d)
    mod_lat = mods[:, :b]
    mod_ctx = mods[:, b:b + 1]

    n1 = norm1_g.reshape(depth, 1, d)
    n2 = norm2_g.reshape(depth, 1, d)
    fg = final_g.reshape(1, 1, d)
    w_in = conv_in_w.astype(BF16)

    y, yc, (w_co, w1, w2, w_qkv, w_ao) = _conv_pre_call(
        x, ctx, mod_lat, mod_ctx, n1, w_in, conv_w, 0, 0, tm_pre, 2,
        to_cast=(conv_out_w, mlp_w1, mlp_w2, attn_qkv_w, attn_out_w))
    x, ctx = _post_call(x, y, ctx, yc, mod_lat, mod_ctx, n2, w_co, w1, w2, None, 0, 0, tm_post, 2)

    qkv, ckv = _proj_pre_call(x, ctx, mod_lat, mod_ctx, n1, w_qkv, 1, 0, tm_pre, head_dim ** -0.5 * LOG2E, 2)
    o = _attn_call(qkv, ckv, _rpb_lane_rows(attn_rpb[0]))
    (out,) = _post_call(x, o, None, None, mod_lat, None, n2, w_ao, w1, w2, fg, 1, 0, tm_post, 2)
    return out
```

```python
import functools

import jax
import jax.numpy as jnp
from jax import lax
from jax.experimental import pallas as pl
from jax.experimental.pallas import tpu as pltpu

GRID_W = 64
N_HEADS = 16
WIN_ROWS = 8
WIN_COLS = 16
RMS_EPS = 1e-6
NEG_INF = -1e30
LOG2E = 1.4426950408889634

LANES = 128
BF16_ROWS = 16
HALO = 8
HEADS_PER_STEP = 2
Q_ROWS, Q_COLS = 8, 16
K_ROWS, K_COLS = 16, 32
ROWS_PER_TILE = LANES // K_COLS
BLOCK_ROWS = 4
ATTN_BATCH = 4
VMEM_LIMIT = 56 * 1024 * 1024

BF16 = jnp.bfloat16
F32 = jnp.float32


def _norm_mod(v, g, shift, scale):
    inv = lax.rsqrt(jnp.mean(v * v, axis=-1, keepdims=True) + RMS_EPS)
    return ((v * inv) * g) * (1.0 + scale) + shift


def _ada_kernel(cond_ref, w_ref, b_ref, o_ref):
    cnd = cond_ref[...]
    s = (cnd / (1.0 + jnp.exp(-cnd))).astype(BF16)
    o_ref[...] = jnp.dot(s, w_ref[...].astype(BF16), preferred_element_type=F32) + b_ref[...]


def _ada_call(cond, ada_w, ada_b):
    depth, d, n = ada_w.shape
    rows = cond.shape[0]
    bn = 2 * d
    return pl.pallas_call(
        _ada_kernel,
        grid=(depth, n // bn),
        in_specs=[
            pl.BlockSpec((rows, d), lambda i, j: (0, 0)),
            pl.BlockSpec((None, d, bn), lambda i, j: (i, 0, j)),
            pl.BlockSpec((None, 1, bn), lambda i, j: (i, 0, j)),
        ],
        out_specs=pl.BlockSpec((None, rows, bn), lambda i, j: (i, 0, j)),
        out_shape=jax.ShapeDtypeStruct((depth, rows, n), F32),
        compiler_params=pltpu.CompilerParams(
            dimension_semantics=("arbitrary", "arbitrary"), vmem_limit_bytes=VMEM_LIMIT),
    )(cond, ada_w, ada_b.reshape(depth, 1, n))


def _mod_spec(mod, layer):
    if mod.shape[1] == 1:
        return pl.BlockSpec((None, None, 6, mod.shape[3]), lambda t, b: (layer, 0, 0, 0))
    return pl.BlockSpec((None, None, 6, mod.shape[3]), lambda t, b: (layer, b, 0, 0))


def _layer_spec(arr, layer):
    nd = arr.ndim - 1
    return pl.BlockSpec((None,) + arr.shape[1:], lambda t, b: (layer,) + (0,) * nd,
                        pipeline_mode=pl.Buffered(1))


def _latent_batch(t, b, nt, n_batch):
    return jnp.where(t < nt, b, n_batch - 1)


def _tile_spec(tm, d, nt, n_batch):
    return pl.BlockSpec((None, tm, d), lambda t, b: (_latent_batch(t, b, nt, n_batch), jnp.minimum(t, nt - 1), 0))


def _ctx_spec(rows, d, nt):
    return pl.BlockSpec((None, rows, d), lambda t, b: (jnp.where(t == nt, b, 0), 0, 0))


def _conv_tile(xe, mod_ref, g_ref, w_ref, cw_ref, o_ref, *, n_sub, pad_first, pad_last):
    d = xe.shape[1]
    shift = mod_ref[0:1, :]
    scale = mod_ref[1:2, :]
    sub = (xe.shape[0] - 2 * HALO) // n_sub
    n_ext = sub + 2 * HALO
    row = lax.broadcasted_iota(jnp.int32, (n_ext, 1), 0)

    def project(i):
        he = _norm_mod(xe[i * sub:i * sub + n_ext], g_ref[...], shift, scale).astype(BF16)
        return he, jnp.dot(he, w_ref[:, d:], preferred_element_type=F32)

    def gate_conv(i, he, cv):
        u = cv[:, :d] * cv[:, d:]
        if i == 0:
            u = jnp.where((row >= HALO) | jnp.logical_not(pad_first), u, 0.0)
        if i == n_sub - 1:
            u = jnp.where((row < sub + HALO) | jnp.logical_not(pad_last), u, 0.0)
        u_prev = pltpu.roll(u, 1, axis=0)[HALO:HALO + sub]
        u_next = pltpu.roll(u, n_ext - 1, axis=0)[HALO:HALO + sub]
        conv = u_prev * cw_ref[0:1, :] + u[HALO:HALO + sub] * cw_ref[1:2, :] + u_next * cw_ref[2:3, :]
        bg = jnp.dot(he[HALO:HALO + sub], w_ref[:, :d], preferred_element_type=F32)
        o_ref[i * sub:(i + 1) * sub, :] = (bg * conv).astype(BF16)

    projected = [project(i) for i in range(n_sub)]
    for i, (he, cv) in enumerate(projected):
        gate_conv(i, he, cv)


def _conv_pre_kernel(x_ref, xp_ref, xn_ref, c_ref, modx_ref, modc_ref, g_ref, w_ref, cw_ref, *rest, nt, n_sub):
    n_cast = (len(rest) - 2) // 2
    cast_in = rest[:n_cast]
    y_ref, yc_ref = rest[n_cast:n_cast + 2]
    cast_out = rest[n_cast + 2:]
    t = pl.program_id(0)

    @pl.when(t < nt)
    def _():
        xe = jnp.concatenate([xp_ref[...], x_ref[...], xn_ref[...]], axis=0)
        _conv_tile(xe, modx_ref, g_ref, w_ref, cw_ref, y_ref, n_sub=n_sub,
                   pad_first=t == 0, pad_last=t == nt - 1)
        for src, dst in zip(cast_in, cast_out):
            dst[...] = src[...].astype(BF16)

    @pl.when(t == nt)
    def _():
        z = jnp.zeros((HALO, c_ref.shape[1]), F32)
        xe = jnp.concatenate([z, c_ref[...], z], axis=0)
        _conv_tile(xe, modc_ref, g_ref, w_ref, cw_ref, yc_ref, n_sub=1, pad_first=True, pad_last=True)


def _conv_pre_call(x, ctx, mod_x, mod_c, g, w_in, conv_w, layer, mixer, tm, n_sub, to_cast):
    b, l, d = x.shape
    lc = ctx.shape[1]
    nt = l // tm
    hb = tm // HALO
    last_hb = l // HALO - 1
    n_steps = nt * b
    kern = functools.partial(_conv_pre_kernel, nt=nt, n_sub=n_sub)

    def prev_halo(t, i):
        return (_latent_batch(t, i, nt, b), jnp.maximum(jnp.minimum(t, nt - 1) * hb - 1, 0), 0)

    def next_halo(t, i):
        return (_latent_batch(t, i, nt, b), jnp.minimum((jnp.minimum(t, nt - 1) + 1) * hb, last_hb), 0)

    def cast_block(t, i):
        return (jnp.minimum(t * b + i, n_steps - 1), 0)

    flat = [w.reshape(-1, w.shape[-1]) for w in to_cast]
    cast_specs = [pl.BlockSpec((w.shape[0] // n_steps, w.shape[1]), cast_block) for w in flat]
    assert all(w.shape[0] % (n_steps * BF16_ROWS) == 0 for w in flat)
    outs = pl.pallas_call(
        kern,
        grid=(nt + 1, b),
        in_specs=[
            _tile_spec(tm, d, nt, b),
            pl.BlockSpec((None, HALO, d), prev_halo),
            pl.BlockSpec((None, HALO, d), next_halo),
            _ctx_spec(lc, d, nt),
            _mod_spec(mod_x, layer),
            _mod_spec(mod_c, layer),
            _layer_spec(g, layer),
            _layer_spec(w_in, mixer),
            _layer_spec(conv_w, mixer),
        ] + cast_specs,
        out_specs=[_tile_spec(tm, d, nt, b), _ctx_spec(lc, d, nt)] + cast_specs,
        out_shape=[jax.ShapeDtypeStruct((b, l, d), BF16), jax.ShapeDtypeStruct((b, lc, d), BF16)]
        + [jax.ShapeDtypeStruct(w.shape, BF16) for w in flat],
        compiler_params=pltpu.CompilerParams(
            dimension_semantics=("arbitrary", "arbitrary"), vmem_limit_bytes=VMEM_LIMIT),
    )(x, x, x, ctx, mod_x, mod_c, g, w_in, conv_w, *flat)
    return outs[0], outs[1], [o.reshape(w.shape) for o, w in zip(outs[2:], to_cast)]


def _proj_tile(x_ref, mod_ref, g_ref, w_ref, o_ref, *, col0, n_scaled, q_scale, n_sub):
    sub = x_ref.shape[0] // n_sub
    for i in range(n_sub):
        rs = slice(i * sub, (i + 1) * sub)
        h = _norm_mod(x_ref[rs, :], g_ref[...], mod_ref[0:1, :], mod_ref[1:2, :]).astype(BF16)
        if n_scaled:
            o_ref[rs, :n_scaled] = (
                jnp.dot(h, w_ref[:, col0:col0 + n_scaled], preferred_element_type=F32) * q_scale).astype(BF16)
        o_ref[rs, n_scaled:] = jnp.dot(h, w_ref[:, col0 + n_scaled:], preferred_element_type=F32).astype(BF16)


X_RING = 3


def _ring_copy(x_hbm, xbuf, sem, s, *, tm, n_batch):
    slot = lax.rem(s, X_RING)
    src = x_hbm.at[lax.rem(s, n_batch), pl.ds(lax.div(s, n_batch) * tm, tm), :]
    return pltpu.make_async_copy(src, xbuf.at[slot], sem.at[slot])


def _proj_pre_kernel(x_hbm, c_ref, modx_ref, modc_ref, g_ref, w_ref, qkv_ref, ckv_ref, xbuf, sem, *,
                     nt, d, q_scale, n_sub):
    t = pl.program_id(0)
    n_batch = pl.num_programs(1)
    tm = xbuf.shape[1]
    s = t * n_batch + pl.program_id(1)
    n_lat = nt * n_batch
    copy = functools.partial(_ring_copy, x_hbm, xbuf, sem, tm=tm, n_batch=n_batch)

    @pl.when(s == 0)
    def _():
        for k in range(X_RING - 1):
            copy(jnp.int32(k)).start()

    @pl.when(t < nt)
    def _():
        @pl.when(s + (X_RING - 1) < n_lat)
        def _():
            copy(s + (X_RING - 1)).start()

        copy(s).wait()
        _proj_tile(xbuf.at[lax.rem(s, X_RING)], modx_ref, g_ref, w_ref, qkv_ref, col0=0, n_scaled=d,
                   q_scale=q_scale, n_sub=n_sub)

    @pl.when(t == nt)
    def _():
        _proj_tile(c_ref, modc_ref, g_ref, w_ref, ckv_ref, col0=d, n_scaled=0, q_scale=1.0, n_sub=1)


def _proj_pre_call(x, ctx, mod_x, mod_c, g, w, layer, mixer, tm, q_scale, n_sub):
    b, l, d = x.shape
    lc = ctx.shape[1]
    n = w.shape[2]
    nt = l // tm
    assert nt * b >= X_RING - 1
    kern = functools.partial(_proj_pre_kernel, nt=nt, d=d, q_scale=q_scale, n_sub=n_sub)
    return pl.pallas_call(
        kern,
        grid=(nt + 1, b),
        in_specs=[
            pl.BlockSpec(memory_space=pl.ANY),
            _ctx_spec(lc, d, nt),
            _mod_spec(mod_x, layer),
            _mod_spec(mod_c, layer),
            _layer_spec(g, layer),
            _layer_spec(w, mixer),
        ],
        out_specs=[_tile_spec(tm, n, nt, b), _ctx_spec(lc, n - d, nt)],
        out_shape=[jax.ShapeDtypeStruct((b, l, n), BF16), jax.ShapeDtypeStruct((b, lc, n - d), BF16)],
        scratch_shapes=[pltpu.VMEM((X_RING, tm, d), F32), pltpu.SemaphoreType.DMA((X_RING,))],
        compiler_params=pltpu.CompilerParams(
            dimension_semantics=("arbitrary", "arbitrary"), vmem_limit_bytes=VMEM_LIMIT),
    )(x, ctx, mod_x, mod_c, g, w)


def _post_tile(x_ref, y_ref, mod_ref, g_ref, wo_ref, w1_ref, w2_ref, fg_ref, o_ref, *, ff_chunk, n_sub):
    gate1 = mod_ref[2:3, :]
    shift2 = mod_ref[3:4, :]
    scale2 = mod_ref[4:5, :]
    gate2 = mod_ref[5:6, :]
    ff = w1_ref.shape[1]
    sub = x_ref.shape[0] // n_sub

    def out_proj(i):
        rs = slice(i * sub, (i + 1) * sub)
        return x_ref[rs, :] + gate1 * jnp.dot(y_ref[rs, :], wo_ref[...], preferred_element_type=F32)

    def mlp(x1):
        h = _norm_mod(x1, g_ref[...], shift2, scale2).astype(BF16)
        acc = None
        for f in range(0, ff, ff_chunk):
            a = jnp.dot(h, w1_ref[:, f:f + ff_chunk], preferred_element_type=F32)
            a = jnp.maximum(a, 0.0)
            a = (a * a).astype(BF16)
            part = jnp.dot(a, w2_ref[f:f + ff_chunk, :], preferred_element_type=F32)
            acc = part if acc is None else acc + part
        return x1 + gate2 * acc

    def finish(i, x2):
        if fg_ref is not None:
            inv = lax.rsqrt(jnp.mean(x2 * x2, axis=-1, keepdims=True) + RMS_EPS)
            x2 = (x2 * inv) * fg_ref[...]
        o_ref[i * sub:(i + 1) * sub, :] = x2

    x1_next = out_proj(0)
    for i in range(n_sub):
        x1 = x1_next
        if i + 1 < n_sub:
            x1_next = out_proj(i + 1)
        finish(i, mlp(x1))


def _post_kernel(*refs, nt, ff_chunk, n_sub, with_ctx, final):
    refs = list(refs)
    x_ref, y_ref = refs[:2]
    del refs[:2]
    if with_ctx:
        c_ref, yc_ref = refs[:2]
        del refs[:2]
    modx_ref = refs.pop(0)
    modc_ref = refs.pop(0) if with_ctx else None
    g_ref, wo_ref, w1_ref, w2_ref = refs[:4]
    del refs[:4]
    fg_ref = refs.pop(0) if final else None
    o_ref = refs.pop(0)
    oc_ref = refs.pop(0) if with_ctx else None
    t = pl.program_id(0)

    @pl.when(t < nt)
    def _():
        _post_tile(x_ref, y_ref, modx_ref, g_ref, wo_ref, w1_ref, w2_ref, fg_ref, o_ref,
                   ff_chunk=ff_chunk, n_sub=n_sub)

    if with_ctx:
        @pl.when(t == nt)
        def _():
            _post_tile(c_ref, yc_ref, modc_ref, g_ref, wo_ref, w1_ref, w2_ref, fg_ref, oc_ref,
                       ff_chunk=ff_chunk, n_sub=1)


def _post_call(x, y, ctx, yc, mod_x, mod_c, g, w_out, w1, w2, final_g, layer, mixer, tm, n_sub, ff_chunk=1024):
    b, l, d = x.shape
    nt = l // tm
    with_ctx = ctx is not None
    final = final_g is not None
    kern = functools.partial(_post_kernel, nt=nt, ff_chunk=ff_chunk, n_sub=n_sub, with_ctx=with_ctx, final=final)
    in_specs = [_tile_spec(tm, d, nt, b), _tile_spec(tm, d, nt, b)]
    args = [x, y]
    out_specs = [_tile_spec(tm, d, nt, b)]
    out_shape = [jax.ShapeDtypeStruct((b, l, d), F32)]
    if with_ctx:
        lc = ctx.shape[1]
        in_specs += [_ctx_spec(lc, d, nt), _ctx_spec(lc, d, nt)]
        args += [ctx, yc]
        out_specs.append(_ctx_spec(lc, d, nt))
        out_shape.append(jax.ShapeDtypeStruct((b, lc, d), F32))
    in_specs.append(_mod_spec(mod_x, layer))
    args.append(mod_x)
    if with_ctx:
        in_specs.append(_mod_spec(mod_c, layer))
        args.append(mod_c)
    in_specs += [_layer_spec(g, layer), _layer_spec(w_out, mixer), _layer_spec(w1, layer), _layer_spec(w2, layer)]
    args += [g, w_out, w1, w2]
    if final:
        in_specs.append(_layer_spec(final_g, 0))
        args.append(final_g)
    return pl.pallas_call(
        kern,
        grid=(nt + (1 if with_ctx else 0), b),
        in_specs=in_specs,
        out_specs=out_specs,
        out_shape=out_shape,
        compiler_params=pltpu.CompilerParams(
            dimension_semantics=("arbitrary", "arbitrary"), vmem_limit_bytes=VMEM_LIMIT),
    )(*args)


def _key_row_start(g, rows):
    return min(max(Q_ROWS * g - WIN_ROWS // 2, 0), rows - K_ROWS)


def _key_col_start(cb):
    lo = min(max(Q_COLS * cb - WIN_COLS // 2, 0), GRID_W - WIN_COLS)
    hi = min(max(Q_COLS * (cb + 1) - 1 - WIN_COLS // 2, 0), GRID_W - WIN_COLS) + WIN_COLS
    start = min(lo // HALO * HALO, GRID_W - K_COLS)
    assert start <= lo and hi <= start + K_COLS
    return start


def _window_row(r, rows):
    return min(max(r - WIN_ROWS // 2, 0), rows - WIN_ROWS)


def _visible_tiles(g, blk, rows):
    rel = [_window_row(Q_ROWS * g + BLOCK_ROWS * blk + i, rows) - _key_row_start(g, rows)
           for i in range(BLOCK_ROWS)]
    return min(rel) // ROWS_PER_TILE, -(-(max(rel) + WIN_ROWS) // ROWS_PER_TILE)


def _rpb_lane_rows(rpb):
    nc = rpb.shape[-1]
    placed = [jnp.pad(rpb, ((0, 0), (0, 0), (K_COLS * j, LANES - K_COLS * j - nc))) for j in range(ROWS_PER_TILE)]
    return jnp.stack(placed, axis=2)


def _fill_bias(rv_ref, bias_ref, *, rows):
    n_groups = rows // Q_ROWS
    shape = (Q_COLS, LANES)
    qcl = lax.broadcasted_iota(jnp.int32, shape, 0)
    lane = lax.broadcasted_iota(jnp.int32, shape, 1)
    t = lane & (K_COLS - 1)
    neg = jnp.full(shape, NEG_INF, F32)
    for cls, g in enumerate((0, 1, n_groups - 1)):
        kr0 = _key_row_start(g, rows)
        for cb in range(GRID_W // Q_COLS):
            kstart = _key_col_start(cb)
            cs = jnp.clip(qcl + (Q_COLS * cb - WIN_COLS // 2), 0, GRID_W - WIN_COLS) - kstart
            col_ok = (t >= cs) & (t < cs + WIN_COLS)
            delta = kstart - Q_COLS * cb + WIN_COLS - 1
            for hh in range(HEADS_PER_STEP):
                for qr in range(Q_ROWS):
                    r = Q_ROWS * g + qr
                    rs = _window_row(r, rows)
                    t_lo, t_hi = _visible_tiles(g, qr // BLOCK_ROWS, rows)
                    for tile in range(t_lo, t_hi):
                        krow = [kr0 + ROWS_PER_TILE * tile + j for j in range(ROWS_PER_TILE)]
                        seen = [j for j in range(ROWS_PER_TILE) if rs <= krow[j] < rs + WIN_ROWS]
                        dst = (cls, hh, cb, slice(qr * Q_COLS, (qr + 1) * Q_COLS),
                               slice(tile * LANES, (tile + 1) * LANES))
                        if not seen:
                            bias_ref[dst] = neg
                            continue
                        v = None
                        for j in range(ROWS_PER_TILE):
                            dr = min(max(krow[j] - r + WIN_ROWS - 1, 0), 2 * WIN_ROWS - 2)
                            piece = rv_ref[hh, dr, j:j + 1, :]
                            v = piece if v is None else v + piece
                        rolled = pltpu.roll(jnp.broadcast_to(v * LOG2E, shape), (-delta) % LANES, axis=1,
                                            stride=1, stride_axis=0)
                        ok = col_ok & (lane >= K_COLS * seen[0]) & (lane < K_COLS * (seen[-1] + 1))
                        bias_ref[dst] = jnp.where(ok, rolled, neg)


def _attn_kernel(q_ref, k_ref, v_ref, kc_ref, vc_ref, rv_ref, o_ref, bias_ref, kf_ref, vf_ref, *,
                 rows, head_dim):
    n_batch = q_ref.shape[0]
    n_groups = rows // Q_ROWS
    n_cb = GRID_W // Q_COLS
    nq = Q_ROWS * Q_COLS
    nk = K_ROWS * K_COLS

    @pl.when(pl.program_id(1) == 0)
    def _():
        _fill_bias(rv_ref, bias_ref, rows=rows)

    lane = lax.broadcasted_iota(jnp.int32, (1, HEADS_PER_STEP * head_dim), 1)
    nt = (((1,), (1,)), ((), ()))

    converted = {}

    def window(src_ref, dst_ref, bi, g, cb):
        r0 = _key_row_start(g, rows)
        c0 = _key_col_start(cb)
        done = converted.get((id(dst_ref), bi), 0)
        if done < r0 + K_ROWS:
            tok = slice(done * GRID_W, (r0 + K_ROWS) * GRID_W)
            dst_ref[bi, tok, :] = src_ref[bi, tok, :].astype(F32)
            converted[(id(dst_ref), bi)] = r0 + K_ROWS
        pieces = [dst_ref[bi, (r0 + j) * GRID_W + c0:(r0 + j) * GRID_W + c0 + K_COLS, :]
                  for j in range(K_ROWS)]
        return jnp.concatenate(pieces, axis=0).astype(BF16)

    def scores(bi, g, cb):
        pieces = [q_ref[bi, (Q_ROWS * g + qr) * GRID_W + Q_COLS * cb:
                        (Q_ROWS * g + qr) * GRID_W + Q_COLS * (cb + 1), :] for qr in range(Q_ROWS)]
        q = jnp.concatenate(pieces, axis=0)
        zero = jnp.zeros_like(q)
        qh = jnp.concatenate(
            [jnp.where((lane >= hh * head_dim) & (lane < (hh + 1) * head_dim), q, zero)
             for hh in range(HEADS_PER_STEP)], axis=0)
        s_loc = lax.dot_general(qh, window(k_ref, kf_ref, bi, g, cb), nt, preferred_element_type=F32)
        s_ctx = lax.dot_general(qh, kc_ref[bi], nt, preferred_element_type=F32)
        return s_loc, s_ctx

    def attend(bi, g, cb, s_loc, s_ctx):
        cls = 0 if g == 0 else (2 if g == n_groups - 1 else 1)
        p_blocks = []
        for hh in range(HEADS_PER_STEP):
            for blk in range(Q_ROWS // BLOCK_ROWS):
                t_lo, t_hi = _visible_tiles(g, blk, rows)
                nb = BLOCK_ROWS * Q_COLS
                qsl = slice(blk * nb, (blk + 1) * nb)
                rsl = slice(hh * nq + qsl.start, hh * nq + qsl.stop)
                tiles = [s_loc[rsl, t * LANES:(t + 1) * LANES] + bias_ref[cls, hh, cb, qsl, t * LANES:(t + 1) * LANES]
                         for t in range(t_lo, t_hi)]
                tiles += [s_ctx[rsl, t * LANES:(t + 1) * LANES] for t in range(s_ctx.shape[1] // LANES)]
                m = jnp.max(functools.reduce(jnp.maximum, tiles), axis=-1, keepdims=True)
                probs = [jnp.exp2(t - m).astype(BF16) for t in tiles]
                n_loc = t_hi - t_lo
                pieces = [jnp.zeros((nb, LANES), BF16)] * t_lo + probs[:n_loc]
                pieces += [jnp.zeros((nb, LANES), BF16)] * (nk // LANES - t_hi) + probs[n_loc:]
                p_blocks.append(jnp.concatenate(pieces, axis=1))
        p = jnp.concatenate(p_blocks, axis=0)
        vw = jnp.concatenate([window(v_ref, vf_ref, bi, g, cb), jnp.ones((nk, LANES), BF16)], axis=1)
        vc = jnp.concatenate([vc_ref[bi], jnp.ones((vc_ref.shape[1], LANES), BF16)], axis=1)
        o = (jnp.dot(p[:, :nk], vw, preferred_element_type=F32)
             + jnp.dot(p[:, nk:], vc, preferred_element_type=F32))
        o = o[:, :LANES] / o[:, LANES:]
        out = o[(HEADS_PER_STEP - 1) * nq:]
        for hh in range(HEADS_PER_STEP - 2, -1, -1):
            out = jnp.where(lane < (hh + 1) * head_dim, o[hh * nq:(hh + 1) * nq], out)
        out = out.astype(BF16)
        for qr in range(Q_ROWS):
            tok = (Q_ROWS * g + qr) * GRID_W + Q_COLS * cb
            o_ref[bi, tok:tok + Q_COLS, :] = out[qr * Q_COLS:(qr + 1) * Q_COLS]

    units = [(bi, g, cb) for bi in range(n_batch) for g in range(n_groups) for cb in range(n_cb)]
    nxt = scores(*units[0])
    for u, unit in enumerate(units):
        cur = nxt
        if u + 1 < len(units):
            nxt = scores(*units[u + 1])
        attend(*unit, *cur)


def _attn_call(qkv, ckv, rpb_rows):
    b, s, d3 = qkv.shape
    d = d3 // 3
    ctx_len = ckv.shape[1]
    head_dim = d // N_HEADS
    w = HEADS_PER_STEP * head_dim
    assert w == LANES
    n_hp = d // w
    rows = s // GRID_W
    n_dr = rpb_rows.shape[1]
    bb = ATTN_BATCH if b % ATTN_BATCH == 0 else 1
    kern = functools.partial(_attn_kernel, rows=rows, head_dim=head_dim)
    return pl.pallas_call(
        kern,
        grid=(n_hp, b // bb),
        in_specs=[
            pl.BlockSpec((bb, s, w), lambda h, i: (i, 0, h)),
            pl.BlockSpec((bb, s, w), lambda h, i: (i, 0, n_hp + h)),
            pl.BlockSpec((bb, s, w), lambda h, i: (i, 0, 2 * n_hp + h)),
            pl.BlockSpec((bb, ctx_len, w), lambda h, i: (i, 0, h)),
            pl.BlockSpec((bb, ctx_len, w), lambda h, i: (i, 0, n_hp + h)),
            pl.BlockSpec((HEADS_PER_STEP, n_dr, ROWS_PER_TILE, LANES), lambda h, i: (h, 0, 0, 0)),
        ],
        out_specs=pl.BlockSpec((bb, s, w), lambda h, i: (i, 0, h)),
        out_shape=jax.ShapeDtypeStruct((b, s, d), BF16),
        scratch_shapes=[
            pltpu.VMEM((3, HEADS_PER_STEP, GRID_W // Q_COLS, Q_ROWS * Q_COLS, K_ROWS * K_COLS), F32),
            pltpu.VMEM((bb, s, w), F32),
            pltpu.VMEM((bb, s, w), F32),
        ],
        compiler_params=pltpu.CompilerParams(
            dimension_semantics=("arbitrary", "arbitrary"), vmem_limit_bytes=VMEM_LIMIT),
    )(qkv, qkv, qkv, ckv, ckv, rpb_rows)


def kernel(x, c, ctx, c_ctx, norm1_g, norm2_g, ada_w, ada_b, conv_in_w, conv_w, conv_out_w,
           attn_qkv_w, attn_rpb, attn_out_w, mlp_w1, mlp_w2, final_g):
    b, s, d = x.shape
    depth = ada_w.shape[0]
    assert depth == 2 and s % (GRID_W * Q_ROWS) == 0 and s // GRID_W >= K_ROWS and d % N_HEADS == 0
    head_dim = d // N_HEADS
    tm_pre, tm_post = 1024, 1024

    pad = (-(b + 1)) % 8
    cond = jnp.concatenate([c, c_ctx[None, :], jnp.zeros((pad, d), F32)], axis=0)
    mods = _ada_call(cond, ada_w, ada_b).reshape(depth, b + 1 + pad, 6, d)
    mod_lat = mods[:, :b]
    mod_ctx = mods[:, b:b + 1]

    n1 = norm1_g.reshape(depth, 1, d)
    n2 = norm2_g.reshape(depth, 1, d)
    fg = final_g.reshape(1, 1, d)
    w_in = conv_in_w.astype(BF16)

    y, yc, (w_co, w1, w2, w_qkv, w_ao) = _conv_pre_call(
        x, ctx, mod_lat, mod_ctx, n1, w_in, conv_w, 0, 0, tm_pre, 2,
        to_cast=(conv_out_w, mlp_w1, mlp_w2, attn_qkv_w, attn_out_w))
    x, ctx = _post_call(x, y, ctx, yc, mod_lat, mod_ctx, n2, w_co, w1, w2, None, 0, 0, tm_post, 2)

    qkv, ckv = _proj_pre_call(x, ctx, mod_lat, mod_ctx, n1, w_qkv, 1, 0, tm_pre, head_dim ** -0.5 * LOG2E, 2)
    o = _attn_call(qkv, ckv, _rpb_lane_rows(attn_rpb[0]))
    (out,) = _post_call(x, o, None, None, mod_lat, None, n2, w_ao, w1, w2, fg, 1, 0, tm_post, 2)
    return out
```
